```python
import math
import jax, jax.numpy as jnp
from jax import lax
import numpy as np

D_MODEL = 1024
BATCH = 8
SEQ = 2048
DEPTH = 2
DEC_BATCH = 32
DEC_SEQ = 16
PAST_LEN = 4096

CHUNK = 64
Q_BLOCK = 128
SB_HEADS = 8
SB_HEAD_DIM = 64
SB_WIDTH = SB_HEADS * SB_HEAD_DIM
MLA_HEADS = 8
MLA_NOPE_DIM = 64
MLA_ROPE_DIM = 32
MLA_V_DIM = 64
MLA_Q_RANK = 384
MLA_KV_RANK = 256
MLA_WIDTH = MLA_HEADS * MLA_V_DIM
MLA_SCALE = (MLA_NOPE_DIM + MLA_ROPE_DIM) ** -0.5
ROPE_BASE = 10000.0
IN_SIZES = (SB_WIDTH, SB_WIDTH, SB_WIDTH, MLA_Q_RANK, MLA_KV_RANK, MLA_ROPE_DIM, D_MODEL, D_MODEL)
N_IN = SB_WIDTH * 3 + MLA_Q_RANK + MLA_KV_RANK + MLA_ROPE_DIM + 2 * D_MODEL
D_FF = 2816
N_EXPERTS = 8
TOP_K = 2
D_EXPERT = 3584
EXPERT_BLOCK = 128
N_DENSE = (DEPTH + 1) // 2
N_MOE = DEPTH // 2
RMS_EPS = 1e-6
NEG_INF = -1e30

kernel_name = "sb_mla_gated_hybrid_stream_step"


def rmsnorm(x, g):
    xf = x.astype(jnp.float32)
    y = xf * lax.rsqrt(jnp.mean(xf * xf, axis=-1, keepdims=True) + RMS_EPS)
    return (y * g.astype(jnp.float32)).astype(x.dtype)


def rope(x, pos):
    half = x.shape[-1] // 2
    inv_freq = jnp.power(ROPE_BASE, -jnp.arange(half, dtype=jnp.float32) / half)
    ang = pos.astype(jnp.float32)[:, None] * inv_freq[None, :]
    shape = (1, pos.shape[0]) + (1,) * (x.ndim - 3) + (half,)
    cos = jnp.cos(ang).reshape(shape).astype(x.dtype)
    sin = jnp.sin(ang).reshape(shape).astype(x.dtype)
    x1, x2 = x[..., :half], x[..., half:]
    return jnp.concatenate([x1 * cos - x2 * sin, x2 * cos + x1 * sin], axis=-1)


def stick_breaking(q, q_pos, k, v, k_pos):
    z = jnp.einsum("bqhd,bkhd->bhqk", q, k).astype(jnp.float32) * (SB_HEAD_DIM ** -0.5)
    seen = k_pos[None, :] < q_pos[:, None]
    log_keep = jnp.where(seen, jax.nn.log_sigmoid(-z), 0.0)
    log_later = lax.cumsum(log_keep, axis=3, reverse=True) - log_keep
    w = jnp.where(seen, jnp.exp(jax.nn.log_sigmoid(z) + log_later), 0.0)
    return jnp.einsum("bhqk,bkhd->bqhd", w.astype(v.dtype), v)


def mla_attend(q_lat, q_rope, q_pos, ckv, krope, k_pos):
    s = (jnp.einsum("bqhc,bkc->bhqk", q_lat, ckv).astype(jnp.float32)
         + jnp.einsum("bqhr,bkr->bhqk", q_rope, krope).astype(jnp.float32)) * MLA_SCALE
    visible = (k_pos // CHUNK)[None, :] <= (q_pos // CHUNK)[:, None]
    p = jax.nn.softmax(jnp.where(visible, s, NEG_INF), axis=-1)
    return jnp.einsum("bhqk,bkc->bqhc", p.astype(ckv.dtype), ckv)


def sweep_queries(attend, q_parts, q_pos):
    n_q = q_pos.shape[0]
    if n_q <= Q_BLOCK:
        return attend(q_parts, q_pos)
    nb = n_q // Q_BLOCK

    def to_blocks(a):
        return jnp.moveaxis(a.reshape((a.shape[0], nb, Q_BLOCK) + a.shape[2:]), 1, 0)

    out = lax.map(lambda xs: attend(xs[0], xs[1]),
                  (tuple(to_blocks(a) for a in q_parts), q_pos.reshape(nb, Q_BLOCK)))
    out = jnp.moveaxis(out, 0, 1)
    return out.reshape((out.shape[0], n_q) + out.shape[3:])


def token_mixer(h, pos, past, w_in, q_norm, w_uq, kv_norm, w_uk, w_uv, w_sb_o, w_mla_o, w_out):
    B, S, _ = h.shape
    offsets = [sum(IN_SIZES[:i + 1]) for i in range(len(IN_SIZES) - 1)]
    sb_q, sb_k, sb_v, c_q, c_kv, k_rope, g_sb, g_mla = jnp.split(h @ w_in, offsets, axis=-1)
    sb_q = sb_q.reshape(B, S, SB_HEADS, SB_HEAD_DIM)
    sb_k = sb_k.reshape(B, S, SB_HEADS, SB_HEAD_DIM)
    sb_v = sb_v.reshape(B, S, SB_HEADS, SB_HEAD_DIM)
    q = (rmsnorm(c_q, q_norm) @ w_uq).reshape(B, S, MLA_HEADS, MLA_NOPE_DIM + MLA_ROPE_DIM)
    q_nope, q_rope = q[..., :MLA_NOPE_DIM], rope(q[..., MLA_NOPE_DIM:], pos)
    q_lat = jnp.einsum("bshn,chn->bshc", q_nope, w_uk)
    c_kv = rmsnorm(c_kv, kv_norm)
    k_rope = rope(k_rope, pos)
    new_rows = (sb_k, sb_v, c_kv, k_rope)
    if past is None:
        k_all, v_all, ckv_all, krope_all = new_rows
        k_pos = pos
    else:
        k_all, v_all, ckv_all, krope_all = [jnp.concatenate([p_, n_], axis=1) for p_, n_ in zip(past, new_rows)]
        k_pos = jnp.arange(past[0].shape[1] + S, dtype=jnp.int32)
    o_sb = sweep_queries(lambda qp, qpos: stick_breaking(qp[0], qpos, k_all, v_all, k_pos), (sb_q,), pos)
    o_lat = sweep_queries(lambda qp, qpos: mla_attend(qp[0], qp[1], qpos, ckv_all, krope_all, k_pos),
                          (q_lat, q_rope), pos)
    o_mla = jnp.einsum("bshc,chv->bshv", o_lat, w_uv)
    merged = (jax.nn.sigmoid(g_sb) * (o_sb.reshape(B, S, SB_WIDTH) @ w_sb_o)
              + jax.nn.sigmoid(g_mla) * (o_mla.reshape(B, S, MLA_WIDTH) @ w_mla_o))
    return merged @ w_out, new_rows


def dense_swiglu(x, w_gate, w_up, w_down):
    return (jax.nn.silu(x @ w_gate) * (x @ w_up)) @ w_down


def moe_swiglu(x, router_w, router_b, w_gate, w_up, w_down):
    B, S, D = x.shape
    xt = x.reshape(-1, D)
    T = xt.shape[0]
    logits = (xt @ router_w).astype(jnp.float32) + router_b.astype(jnp.float32)
    top_logit, top_e = lax.top_k(logits, TOP_K)
    gate = jax.nn.softmax(top_logit, axis=-1).astype(x.dtype)
    n_assign = T * TOP_K
    e_flat = top_e.reshape(-1)
    tok_flat = jnp.repeat(jnp.arange(T, dtype=jnp.int32), TOP_K)
    order = jnp.argsort(e_flat)
    e_sorted = e_flat[order]
    counts = jnp.bincount(e_flat, length=N_EXPERTS)
    padded = (counts + EXPERT_BLOCK - 1) // EXPERT_BLOCK * EXPERT_BLOCK
    start = jnp.cumsum(counts) - counts
    pad_end = jnp.cumsum(padded)
    pad_start = pad_end - padded
    dest = pad_start[e_sorted] + (jnp.arange(n_assign, dtype=jnp.int32) - start[e_sorted])
    n_rows = -(-(n_assign + N_EXPERTS * (EXPERT_BLOCK - 1)) // EXPERT_BLOCK) * EXPERT_BLOCK
    n_blocks = n_rows // EXPERT_BLOCK
    row_tok = jnp.full((n_rows,), T, dtype=jnp.int32).at[dest].set(tok_flat[order])
    blk_e = jnp.minimum(jnp.searchsorted(pad_end, jnp.arange(n_blocks, dtype=jnp.int32) * EXPERT_BLOCK,
                                         side="right"), N_EXPERTS - 1)
    x_pad = jnp.concatenate([xt, jnp.zeros((1, D), xt.dtype)], axis=0)
    xb = x_pad[row_tok].reshape(n_blocks, EXPERT_BLOCK, D)

    def expert_block(args):
        xi, e = args
        return dense_swiglu(xi, w_gate[e], w_up[e], w_down[e])

    yb = lax.map(expert_block, (xb, blk_e)).reshape(n_rows, D)
    y_sorted = yb[dest]
    y_assign = jnp.zeros_like(y_sorted).at[order].set(y_sorted)
    y = jnp.sum(y_assign.reshape(T, TOP_K, D) * gate[..., None], axis=1)
    return y.reshape(B, S, D)


def setup_inputs(seed: int = 0) -> dict:
    key = jax.random.key(seed)
    ks = jax.random.split(key, 32)

    def nrm(k, shape, scale):
        return jax.random.normal(k, shape, jnp.float32) * scale

    def gain(k, shape):
        return 1.0 + 0.02 * jax.random.normal(k, shape, jnp.float32)

    return {
        "x_prompt": nrm(ks[0], (BATCH, SEQ, D_MODEL), 1.0),
        "x_sample": nrm(ks[1], (DEC_BATCH, DEC_SEQ, D_MODEL), 1.0),
        "cache_sb_k": nrm(ks[2], (DEPTH, DEC_BATCH, PAST_LEN, SB_HEADS, SB_HEAD_DIM), 1.0),
        "cache_sb_v": nrm(ks[3], (DEPTH, DEC_BATCH, PAST_LEN, SB_HEADS, SB_HEAD_DIM), 1.0),
        "cache_mla_ckv": nrm(ks[4], (DEPTH, DEC_BATCH, PAST_LEN, MLA_KV_RANK), 1.0),
        "cache_mla_krope": nrm(ks[5], (DEPTH, DEC_BATCH, PAST_LEN, MLA_ROPE_DIM), 1.0),
        "norm_mix": gain(ks[6], (DEPTH, D_MODEL)),
        "w_in": nrm(ks[7], (DEPTH, D_MODEL, N_IN), D_MODEL ** -0.5),
        "mla_q_norm": gain(ks[8], (DEPTH, MLA_Q_RANK)),
        "w_uq": nrm(ks[9], (DEPTH, MLA_Q_RANK, MLA_HEADS * (MLA_NOPE_DIM + MLA_ROPE_DIM)), MLA_Q_RANK ** -0.5),
        "mla_kv_norm": gain(ks[10], (DEPTH, MLA_KV_RANK)),
        "w_uk": nrm(ks[11], (DEPTH, MLA_KV_RANK, MLA_HEADS, MLA_NOPE_DIM), MLA_KV_RANK ** -0.5),
        "w_uv": nrm(ks[12], (DEPTH, MLA_KV_RANK, MLA_HEADS, MLA_V_DIM), MLA_KV_RANK ** -0.5),
        "w_sb_o": nrm(ks[13], (DEPTH, SB_WIDTH, D_MODEL), SB_WIDTH ** -0.5),
        "w_mla_o": nrm(ks[14], (DEPTH, MLA_WIDTH, D_MODEL), MLA_WIDTH ** -0.5),
        "w_out": nrm(ks[15], (DEPTH, D_MODEL, D_MODEL), D_MODEL ** -0.5),
        "norm_ffn": gain(ks[16], (DEPTH, D_MODEL)),
        "ffn_w_gate": nrm(ks[17], (N_DENSE, D_MODEL, D_FF), D_MODEL ** -0.5),
        "ffn_w_up": nrm(ks[18], (N_DENSE, D_MODEL, D_FF), D_MODEL ** -0.5),
        "ffn_w_down": nrm(ks[19], (N_DENSE, D_FF, D_MODEL), D_FF ** -0.5),
        "router_w": nrm(ks[20], (N_MOE, D_MODEL, N_EXPERTS), D_MODEL ** -0.5),
        "router_b": nrm(ks[21], (N_MOE, N_EXPERTS), 0.01),
        "moe_w_gate": nrm(ks[22], (N_MOE, N_EXPERTS, D_MODEL, D_EXPERT), D_MODEL ** -0.5),
        "moe_w_up": nrm(ks[23], (N_MOE, N_EXPERTS, D_MODEL, D_EXPERT), D_MODEL ** -0.5),
        "moe_w_down": nrm(ks[24], (N_MOE, N_EXPERTS, D_EXPERT, D_MODEL), D_EXPERT ** -0.5),
        "norm_final": gain(ks[25], (D_MODEL,)),
    }


def reference(x_prompt, x_sample, cache_sb_k, cache_sb_v, cache_mla_ckv, cache_mla_krope,
              norm_mix, w_in, mla_q_norm, w_uq, mla_kv_norm, w_uk, w_uv, w_sb_o, w_mla_o, w_out,
              norm_ffn, ffn_w_gate, ffn_w_up, ffn_w_down, router_w, router_b,
              moe_w_gate, moe_w_up, moe_w_down, norm_final):
    past_len = cache_sb_k.shape[2]
    pos_p = jnp.arange(x_prompt.shape[1], dtype=jnp.int32)
    pos_s = past_len + jnp.arange(x_sample.shape[1], dtype=jnp.int32)
    hp, hs = x_prompt, x_sample
    rows_p, rows_s = [], []
    for l in range(DEPTH):
        mix_w = (w_in[l], mla_q_norm[l], w_uq[l], mla_kv_norm[l], w_uk[l], w_uv[l],
                 w_sb_o[l], w_mla_o[l], w_out[l])
        mp, rp = token_mixer(rmsnorm(hp, norm_mix[l]), pos_p, None, *mix_w)
        past = (cache_sb_k[l], cache_sb_v[l], cache_mla_ckv[l], cache_mla_krope[l])
        ms, rs = token_mixer(rmsnorm(hs, norm_mix[l]), pos_s, past, *mix_w)
        hp = hp + mp
        hs = hs + ms
        rows_p.append(rp)
        rows_s.append(rs)
        i = l // 2
        if l % 2 == 0:
            hp = hp + dense_swiglu(rmsnorm(hp, norm_ffn[l]), ffn_w_gate[i], ffn_w_up[i], ffn_w_down[i])
            hs = hs + dense_swiglu(rmsnorm(hs, norm_ffn[l]), ffn_w_gate[i], ffn_w_up[i], ffn_w_down[i])
        else:
            hp = hp + moe_swiglu(rmsnorm(hp, norm_ffn[l]), router_w[i], router_b[i],
                                 moe_w_gate[i], moe_w_up[i], moe_w_down[i])
            hs = hs + moe_swiglu(rmsnorm(hs, norm_ffn[l]), router_w[i], router_b[i],
                                 moe_w_gate[i], moe_w_up[i], moe_w_down[i])
    y_prompt = rmsnorm(hp, norm_final)
    y_sample = rmsnorm(hs, norm_final)
    new_sb_k_prompt = jnp.stack([r[0] for r in rows_p])
    new_sb_v_prompt = jnp.stack([r[1] for r in rows_p])
    new_mla_ckv_prompt = jnp.stack([r[2] for r in rows_p])
    new_mla_krope_prompt = jnp.stack([r[3] for r in rows_p])
    new_sb_k_sample = jnp.stack([r[0] for r in rows_s])
    new_sb_v_sample = jnp.stack([r[1] for r in rows_s])
    new_mla_ckv_sample = jnp.stack([r[2] for r in rows_s])
    new_mla_krope_sample = jnp.stack([r[3] for r in rows_s])
    return (y_prompt, y_sample, new_sb_k_prompt, new_sb_v_prompt, new_mla_ckv_prompt, new_mla_krope_prompt,
            new_sb_k_sample, new_sb_v_sample, new_mla_ckv_sample, new_mla_krope_sample)
```

```python
import functools

import jax
import jax.numpy as jnp
from jax import lax
from jax.experimental import pallas as pl
from jax.experimental.pallas import tpu as pltpu

F32 = jnp.float32
BF16 = jnp.bfloat16

D_MODEL = 1024
CHUNK = 64
SB_HEADS = 8
SB_HEAD_DIM = 64
SB_WIDTH = SB_HEADS * SB_HEAD_DIM
MLA_HEADS = 8
MLA_NOPE_DIM = 64
MLA_ROPE_DIM = 32
MLA_V_DIM = 64
MLA_Q_RANK = 384
MLA_KV_RANK = 256
MLA_WIDTH = MLA_HEADS * MLA_V_DIM
MLA_SCALE = (MLA_NOPE_DIM + MLA_ROPE_DIM) ** -0.5
SB_SCALE = SB_HEAD_DIM ** -0.5
ROPE_BASE = 10000.0
ROPE_HALF = MLA_ROPE_DIM // 2
N_EXPERTS = 8
TOP_K = 2
RMS_EPS = 1e-6
NEG_INF = -1e30

LANES = 128
MXU_DIM = 256
VMEM_LIMIT_BYTES = 56 * 1024 * 1024

TOKEN_TILE = 256
Q_TILE = 128
KEY_TILE = MXU_DIM
DECODE_KEY_BLOCK = 1024
EXPERT_ROW_TILE = 256
EXPERT_F_SPLIT = 2
IN_PROJ_COLS = 3 * SB_WIDTH + MLA_Q_RANK + MLA_KV_RANK + LANES


def _params(semantics):
    return pltpu.CompilerParams(dimension_semantics=semantics, vmem_limit_bytes=VMEM_LIMIT_BYTES)


def _rms(x, g):
    return x * lax.rsqrt(jnp.mean(x * x, axis=-1, keepdims=True) + RMS_EPS) * g


def _dot(a, b):
    return jnp.dot(a, b, preferred_element_type=F32)


def _dot_nt(a, b):
    return lax.dot_general(a, b, (((1,), (1,)), ((), ())), preferred_element_type=F32)


def _softplus(z):
    return jnp.maximum(z, 0.0) + jnp.log1p(jnp.exp(-jnp.abs(z)))


def _later_sum_matrix(n):
    r = lax.broadcasted_iota(jnp.int32, (n, n), 0)
    c = lax.broadcasted_iota(jnp.int32, (n, n), 1)
    return jnp.where(r > c, 1.0, 0.0).astype(BF16)


def _later_sums(lk, tri):
    hi = lk.astype(BF16)
    lo = (lk - hi.astype(F32)).astype(BF16)
    return _dot(hi, tri) + _dot(lo, tri)


def _const_spec(shape):
    nd = len(shape)
    return pl.BlockSpec(shape, lambda *_: (0,) * nd)


def _proj_kernel(x_ref, nm_ref, w1_ref, qn_ref, wuq_ref, kvn_ref, wukt_ref, cos_ref, sin_ref,
                 sbq_ref, sbk_ref, sbv_ref, ckv_ref, krope_ref, qlat_ref, qrope_ref):
    xn = _rms(x_ref[...], nm_ref[...]).astype(BF16)
    p = _dot(xn, w1_ref[...])
    o = 0
    sbq_ref[...] = (p[:, o:o + SB_WIDTH] * SB_SCALE).astype(BF16)
    o += SB_WIDTH
    sbk_ref[...] = p[:, o:o + SB_WIDTH]
    o += SB_WIDTH
    sbv_ref[...] = p[:, o:o + SB_WIDTH]
    o += SB_WIDTH
    cq = p[:, o:o + MLA_Q_RANK]
    o += MLA_Q_RANK
    ckv_ref[...] = _rms(p[:, o:o + MLA_KV_RANK], kvn_ref[...])
    o += MLA_KV_RANK
    kr = p[:, o:o + LANES]
    cos = cos_ref[...]
    sin = sin_ref[...]
    c16 = cos[:, :ROPE_HALF]
    s16 = sin[:, :ROPE_HALF]
    k1 = kr[:, :ROPE_HALF]
    k2 = kr[:, ROPE_HALF:MLA_ROPE_DIM]
    krope_ref[...] = jnp.concatenate([k1 * c16 - k2 * s16, k2 * c16 + k1 * s16], axis=-1)

    cqn = _rms(cq, qn_ref[...]).astype(BF16)
    q = _dot(cqn, wuq_ref[...])
    nope_w = MLA_HEADS * MLA_NOPE_DIM
    x1 = q[:, nope_w:nope_w + LANES]
    x2 = q[:, nope_w + LANES:nope_w + 2 * LANES]
    r1 = (x1 * cos - x2 * sin) * MLA_SCALE
    r2 = (x2 * cos + x1 * sin) * MLA_SCALE
    for h in range(MLA_HEADS):
        qn_h = q[:, h * MLA_NOPE_DIM:(h + 1) * MLA_NOPE_DIM].astype(BF16)
        qlat_ref[h] = (_dot(qn_h, wukt_ref[h]) * MLA_SCALE).astype(BF16)
        qrope_ref[h] = jnp.concatenate(
            [r1[:, h * ROPE_HALF:(h + 1) * ROPE_HALF], r2[:, h * ROPE_HALF:(h + 1) * ROPE_HALF]],
            axis=-1).astype(BF16)


def _proj_call(x, mw, cos_tab, sin_tab):
    t = x.shape[0]
    tm = TOKEN_TILE
    n_pos_tiles = cos_tab.shape[0] // tm
    row = lambda i: (i, 0)
    pos = lambda i: (i % n_pos_tiles, 0)
    head_row = lambda i: (0, i, 0)
    return pl.pallas_call(
        _proj_kernel,
        grid=(t // tm,),
        in_specs=[
            pl.BlockSpec((tm, D_MODEL), row),
            _const_spec((1, D_MODEL)),
            _const_spec((D_MODEL, IN_PROJ_COLS)),
            _const_spec((1, MLA_Q_RANK)),
            _const_spec((MLA_Q_RANK, MLA_HEADS * (MLA_NOPE_DIM + MLA_ROPE_DIM))),
            _const_spec((1, MLA_KV_RANK)),
            _const_spec((MLA_HEADS, MLA_NOPE_DIM, MLA_KV_RANK)),
            pl.BlockSpec((tm, LANES), pos),
            pl.BlockSpec((tm, LANES), pos),
        ],
        out_specs=[
            pl.BlockSpec((tm, SB_WIDTH), row),
            pl.BlockSpec((tm, SB_WIDTH), row),
            pl.BlockSpec((tm, SB_WIDTH), row),
            pl.BlockSpec((tm, MLA_KV_RANK), row),
            pl.BlockSpec((tm, MLA_ROPE_DIM), row),
            pl.BlockSpec((MLA_HEADS, tm, MLA_KV_RANK), head_row),
            pl.BlockSpec((MLA_HEADS, tm, MLA_ROPE_DIM), head_row),
        ],
        out_shape=[
            jax.ShapeDtypeStruct((t, SB_WIDTH), BF16),
            jax.ShapeDtypeStruct((t, SB_WIDTH), F32),
            jax.ShapeDtypeStruct((t, SB_WIDTH), F32),
            jax.ShapeDtypeStruct((t, MLA_KV_RANK), F32),
            jax.ShapeDtypeStruct((t, MLA_ROPE_DIM), F32),
            jax.ShapeDtypeStruct((MLA_HEADS, t, MLA_KV_RANK), BF16),
            jax.ShapeDtypeStruct((MLA_HEADS, t, MLA_ROPE_DIM), BF16),
        ],
        compiler_params=_params(("parallel",)),
        name="in_proj",
    )(x, mw["norm_mix"], mw["w1"], mw["q_norm"], mw["w_uq"], mw["kv_norm"], mw["w_ukt"],
      cos_tab, sin_tab)


def _sb_weights(z, seen, tri, run):
    sp = _softplus(z)
    lk = -sp if seen is None else jnp.where(seen, -sp, 0.0)
    later = _later_sums(lk, tri) + run
    w = jnp.exp(z - sp + later)
    if seen is not None:
        w = jnp.where(seen, w, 0.0)
    return w.astype(BF16), jnp.sum(lk, axis=1, keepdims=True)


def _sb_prompt_kernel(q_ref, k_ref, v_ref, o_ref, kbf_ref, vbf_ref):
    qi = pl.program_id(1)

    @pl.when(qi == 0)
    def _():
        kbf_ref[...] = k_ref[...].astype(BF16)
        vbf_ref[...] = v_ref[...].astype(BF16)

    tri = _later_sum_matrix(KEY_TILE)
    n_kb = (qi * Q_TILE + Q_TILE + KEY_TILE - 1) // KEY_TILE
    lane = lax.broadcasted_iota(jnp.int32, (Q_TILE, LANES), 1)
    first_head = lane < SB_HEAD_DIM
    q_pos = qi * Q_TILE + lax.broadcasted_iota(jnp.int32, (Q_TILE, KEY_TILE), 0)
    k_off = lax.broadcasted_iota(jnp.int32, (Q_TILE, KEY_TILE), 1)

    for pair in range(SB_HEADS // 2):
        cols = slice(pair * LANES, (pair + 1) * LANES)
        q_pair = q_ref[:, cols]
        accs = []
        for sel in (first_head, jnp.logical_not(first_head)):
            qm = jnp.where(sel, q_pair, jnp.zeros_like(q_pair))

            def body(j, carry, qm=qm, cols=cols):
                acc, run = carry
                start = pl.multiple_of((n_kb - 1 - j) * KEY_TILE, KEY_TILE)
                kb = kbf_ref[pl.ds(start, KEY_TILE), cols]
                vb = vbf_ref[pl.ds(start, KEY_TILE), cols]
                z = _dot_nt(qm, kb)
                seen = (start + k_off) < q_pos
                w, tot = _sb_weights(z, seen, tri, run)
                return acc + _dot(w, vb), run + tot

            acc, _ = lax.fori_loop(
                0, n_kb, body,
                (jnp.zeros((Q_TILE, LANES), F32), jnp.zeros((Q_TILE, 1), F32)))
            accs.append(acc)
        o_ref[:, cols] = jnp.where(first_head, accs[0], accs[1]).astype(BF16)


def _sb_prompt_call(q, k, v, n_streams, seq):
    nq = seq // Q_TILE
    return pl.pallas_call(
        _sb_prompt_kernel,
        grid=(n_streams, nq),
        in_specs=[
            pl.BlockSpec((Q_TILE, SB_WIDTH), lambda b, i: (b * nq + i, 0)),
            pl.BlockSpec((seq, SB_WIDTH), lambda b, i: (b, 0)),
            pl.BlockSpec((seq, SB_WIDTH), lambda b, i: (b, 0)),
        ],
        out_specs=pl.BlockSpec((Q_TILE, SB_WIDTH), lambda b, i: (b * nq + i, 0)),
        out_shape=jax.ShapeDtypeStruct(q.shape, BF16),
        scratch_shapes=[pltpu.VMEM((seq, SB_WIDTH), BF16), pltpu.VMEM((seq, SB_WIDTH), BF16)],
        compiler_params=_params(("parallel", "arbitrary")),
        name="sb_prompt",
    )(q, k, v)


def _mla_prompt_kernel(ql_ref, qr_ref, ckv_ref, kr_ref, o_ref, cbf_ref, rbf_ref, m_ref, l_ref, acc_ref):
    qi = pl.program_id(1)
    rows = MLA_HEADS * Q_TILE

    @pl.when(qi == 0)
    def _():
        cbf_ref[...] = ckv_ref[...].astype(BF16)
        rbf_ref[...] = kr_ref[...].astype(BF16)

    ql = ql_ref[...].reshape(rows, MLA_KV_RANK)
    qr = qr_ref[...].reshape(rows, MLA_ROPE_DIM)
    m_ref[...] = jnp.full((rows, 1), NEG_INF, F32)
    l_ref[...] = jnp.zeros((rows, 1), F32)
    acc_ref[...] = jnp.zeros((rows, MLA_KV_RANK), F32)
    n_kb = (qi * Q_TILE + Q_TILE + KEY_TILE - 1) // KEY_TILE
    q_row = lax.broadcasted_iota(jnp.int32, (rows, KEY_TILE), 0) % Q_TILE
    q_chunk = (qi * Q_TILE + q_row) // CHUNK
    k_off = lax.broadcasted_iota(jnp.int32, (rows, KEY_TILE), 1)

    def body(j, _):
        start = pl.multiple_of(j * KEY_TILE, KEY_TILE)
        c = cbf_ref[pl.ds(start, KEY_TILE), :]
        r = rbf_ref[pl.ds(start, KEY_TILE), :]
        s = _dot_nt(ql, c) + _dot_nt(qr, r)
        s = jnp.where((start + k_off) // CHUNK <= q_chunk, s, NEG_INF)
        m_old = m_ref[...]
        m_new = jnp.maximum(m_old, jnp.max(s, axis=1, keepdims=True))
        p = jnp.exp(s - m_new)
        alpha = jnp.exp(m_old - m_new)
        l_ref[...] = alpha * l_ref[...] + jnp.sum(p, axis=1, keepdims=True)
        acc_ref[...] = alpha * acc_ref[...] + _dot(p.astype(BF16), c)
        m_ref[...] = m_new
        return 0

    lax.fori_loop(0, n_kb, body, 0)
    o = acc_ref[...] / l_ref[...]
    o_ref[...] = o.reshape(MLA_HEADS, Q_TILE, MLA_KV_RANK).astype(BF16)


def _mla_prompt_call(qlat, qrope, ckv, krope, n_streams, seq):
    nq = seq // Q_TILE
    rows = MLA_HEADS * Q_TILE
    qmap = lambda b, i: (0, b * nq + i, 0)
    return pl.pallas_call(
        _mla_prompt_kernel,
        grid=(n_streams, nq),
        in_specs=[
            pl.BlockSpec((MLA_HEADS, Q_TILE, MLA_KV_RANK), qmap),
            pl.BlockSpec((MLA_HEADS, Q_TILE, MLA_ROPE_DIM), qmap),
            pl.BlockSpec((seq, MLA_KV_RANK), lambda b, i: (b, 0)),
            pl.BlockSpec((seq, MLA_ROPE_DIM), lambda b, i: (b, 0)),
        ],
        out_specs=pl.BlockSpec((MLA_HEADS, Q_TILE, MLA_KV_RANK), qmap),
        out_shape=jax.ShapeDtypeStruct(qlat.shape, BF16),
        scratch_shapes=[
            pltpu.VMEM((seq, MLA_KV_RANK), BF16),
            pltpu.VMEM((seq, MLA_ROPE_DIM), BF16),
            pltpu.VMEM((rows, 1), F32),
            pltpu.VMEM((rows, 1), F32),
            pltpu.VMEM((rows, MLA_KV_RANK), F32),
        ],
        compiler_params=_params(("parallel", "arbitrary")),
        name="mla_prompt",
    )(qlat, qrope, ckv, krope)


def _head_block_mask(n_q, width, per_head):
    r = lax.broadcasted_iota(jnp.int32, (SB_HEADS * n_q, width), 0) // n_q
    c = lax.broadcasted_iota(jnp.int32, (SB_HEADS * n_q, width), 1) // per_head
    return r == c


def _sb_decode_kernel(q_ref, kn_ref, vn_ref, kc_ref, vc_ref, o_ref, acc_ref, run_ref):
    kb = pl.program_id(1)
    n_q = q_ref.shape[0]
    rows = SB_HEADS * n_q
    own = _head_block_mask(n_q, SB_WIDTH, SB_HEAD_DIM)
    q = q_ref[...]
    q_all = jnp.concatenate([q] * SB_HEADS, axis=0)
    qbd = jnp.where(own, q_all, jnp.zeros_like(q_all))
    tri = _later_sum_matrix(KEY_TILE)

    @pl.when(kb == 0)
    def _():
        pad = jnp.zeros((LANES - n_q, SB_WIDTH), BF16)
        kn = jnp.concatenate([kn_ref[...].astype(BF16), pad], axis=0)
        vn = jnp.concatenate([vn_ref[...].astype(BF16), pad], axis=0)
        z = _dot_nt(qbd, kn)
        i = lax.broadcasted_iota(jnp.int32, (rows, LANES), 0) % n_q
        j = lax.broadcasted_iota(jnp.int32, (rows, LANES), 1)
        w, tot = _sb_weights(z, j < i, tri[:LANES, :LANES], jnp.zeros((rows, 1), F32))
        acc_ref[...] = _dot(w, vn)
        run_ref[...] = tot

    n_sub = kc_ref.shape[0] // KEY_TILE
    for sub in range(n_sub - 1, -1, -1):
        ks = kc_ref[sub * KEY_TILE:(sub + 1) * KEY_TILE, :].astype(BF16)
        vs = vc_ref[sub * KEY_TILE:(sub + 1) * KEY_TILE, :].astype(BF16)
        z = _dot_nt(qbd, ks)
        w, tot = _sb_weights(z, None, tri, run_ref[...])
        acc_ref[...] += _dot(w, vs)
        run_ref[...] += tot

    @pl.when(kb == pl.num_programs(1) - 1)
    def _():
        a = jnp.where(own, acc_ref[...], 0.0)
        o = a[0:n_q]
        for h in range(1, SB_HEADS):
            o = o + a[h * n_q:(h + 1) * n_q]
        o_ref[...] = o.astype(BF16)


def _sb_decode_call(q, k_new, v_new, cache_k, cache_v, layer, n_q):
    t = q.shape[0]
    n_streams = t // n_q
    past = cache_k.shape[2]
    n_kb = past // DECODE_KEY_BLOCK
    row = lambda s, j: (s, 0)
    cmap = lambda s, j: (layer, s, n_kb - 1 - j, 0)
    rows = SB_HEADS * n_q
    return pl.pallas_call(
        _sb_decode_kernel,
        grid=(n_streams, n_kb),
        in_specs=[
            pl.BlockSpec((n_q, SB_WIDTH), row),
            pl.BlockSpec((n_q, SB_WIDTH), row),
            pl.BlockSpec((n_q, SB_WIDTH), row),
            pl.BlockSpec((None, None, DECODE_KEY_BLOCK, SB_WIDTH), cmap),
            pl.BlockSpec((None, None, DECODE_KEY_BLOCK, SB_WIDTH), cmap),
        ],
        out_specs=pl.BlockSpec((n_q, SB_WIDTH), row),
        out_shape=jax.ShapeDtypeStruct(q.shape, BF16),
        scratch_shapes=[pltpu.VMEM((rows, SB_WIDTH), F32), pltpu.VMEM((rows, 1), F32)],
        compiler_params=_params(("parallel", "arbitrary")),
        name="sb_decode",
    )(q, k_new, v_new, cache_k, cache_v)


def _mla_decode_kernel(ql_ref, qr_ref, cn_ref, rn_ref, cc_ref, rc_ref, o_ref, m_ref, l_ref, acc_ref,
                       *, past):
    kb = pl.program_id(1)
    n_q = ql_ref.shape[1]
    rows = MLA_HEADS * n_q
    ql = ql_ref[...].reshape(rows, MLA_KV_RANK)
    qr = qr_ref[...].reshape(rows, MLA_ROPE_DIM)

    def update(s, c):
        m_old = m_ref[...]
        m_new = jnp.maximum(m_old, jnp.max(s, axis=1, keepdims=True))
        p = jnp.exp(s - m_new)
        alpha = jnp.exp(m_old - m_new)
        l_ref[...] = alpha * l_ref[...] + jnp.sum(p, axis=1, keepdims=True)
        acc_ref[...] = alpha * acc_ref[...] + _dot(p.astype(BF16), c)
        m_ref[...] = m_new

    @pl.when(kb == 0)
    def _():
        m_ref[...] = jnp.full((rows, 1), NEG_INF, F32)
        l_ref[...] = jnp.zeros((rows, 1), F32)
        acc_ref[...] = jnp.zeros((rows, MLA_KV_RANK), F32)
        c = jnp.concatenate([cn_ref[...].astype(BF16), jnp.zeros((LANES - n_q, MLA_KV_RANK), BF16)], axis=0)
        r = jnp.concatenate([rn_ref[...].astype(BF16), jnp.zeros((LANES - n_q, MLA_ROPE_DIM), BF16)], axis=0)
        s = _dot_nt(ql, c) + _dot_nt(qr, r)
        i = lax.broadcasted_iota(jnp.int32, (rows, LANES), 0) % n_q
        j = lax.broadcasted_iota(jnp.int32, (rows, LANES), 1)
        vis = jnp.logical_and(j < n_q, (past + j) // CHUNK <= (past + i) // CHUNK)
        update(jnp.where(vis, s, NEG_INF), c)

    c = cc_ref[...].astype(BF16)
    r = rc_ref[...].astype(BF16)
    update(_dot_nt(ql, c) + _dot_nt(qr, r), c)

    @pl.when(kb == pl.num_programs(1) - 1)
    def _():
        o = acc_ref[...] / l_ref[...]
        o_ref[...] = o.reshape(MLA_HEADS, n_q, MLA_KV_RANK).astype(BF16)


def _mla_decode_call(qlat, qrope, ckv_new, krope_new, cache_ckv, cache_krope, layer, n_q):
    t = qlat.shape[1]
    n_streams = t // n_q
    past = cache_ckv.shape[2]
    n_kb = past // DECODE_KEY_BLOCK
    rows = MLA_HEADS * n_q
    qmap = lambda s, j: (0, s, 0)
    row = lambda s, j: (s, 0)
    cmap = lambda s, j: (layer, s, j, 0)
    return pl.pallas_call(
        functools.partial(_mla_decode_kernel, past=past),
        grid=(n_streams, n_kb),
        in_specs=[
            pl.BlockSpec((MLA_HEADS, n_q, MLA_KV_RANK), qmap),
            pl.BlockSpec((MLA_HEADS, n_q, MLA_ROPE_DIM), qmap),
            pl.BlockSpec((n_q, MLA_KV_RANK), row),
            pl.BlockSpec((n_q, MLA_ROPE_DIM), row),
            pl.BlockSpec((None, None, DECODE_KEY_BLOCK, MLA_KV_RANK), cmap),
            pl.BlockSpec((None, None, DECODE_KEY_BLOCK, MLA_ROPE_DIM), cmap),
        ],
        out_specs=pl.BlockSpec((MLA_HEADS, n_q, MLA_KV_RANK), qmap),
        out_shape=jax.ShapeDtypeStruct(qlat.shape, BF16),
        scratch_shapes=[
            pltpu.VMEM((rows, 1), F32),
            pltpu.VMEM((rows, 1), F32),
            pltpu.VMEM((rows, MLA_KV_RANK), F32),
        ],
        compiler_params=_params(("parallel", "arbitrary")),
        name="mla_decode",
    )(qlat, qrope, ckv_new, krope_new, cache_ckv, cache_krope)


def _merge_kernel(x_ref, osb_ref, olat_ref, nm_ref, wg_ref, wuv_ref, wsbo_ref, wmlao_ref, wout_ref, h_ref):
    x = x_ref[...]
    xn = _rms(x, nm_ref[...]).astype(BF16)
    g = _dot(xn, wg_ref[...])
    a = _dot(osb_ref[...], wsbo_ref[...])
    pieces = []
    for pair in range(MLA_HEADS // 2):
        pieces.append(_dot(olat_ref[2 * pair], wuv_ref[2 * pair])
                      + _dot(olat_ref[2 * pair + 1], wuv_ref[2 * pair + 1]))
    o_mla = jnp.concatenate(pieces, axis=-1).astype(BF16)
    b = _dot(o_mla, wmlao_ref[...])
    merged = jax.nn.sigmoid(g[:, :D_MODEL]) * a + jax.nn.sigmoid(g[:, D_MODEL:]) * b
    h_ref[...] = x + _dot(merged.astype(BF16), wout_ref[...])


def _merge_call(x, o_sb, o_lat, mw):
    t = x.shape[0]
    tm = TOKEN_TILE
    row = lambda i: (i, 0)
    return pl.pallas_call(
        _merge_kernel,
        grid=(t // tm,),
        in_specs=[
            pl.BlockSpec((tm, D_MODEL), row),
            pl.BlockSpec((tm, SB_WIDTH), row),
            pl.BlockSpec((MLA_HEADS, tm, MLA_KV_RANK), lambda i: (0, i, 0)),
            _const_spec((1, D_MODEL)),
            _const_spec((D_MODEL, 2 * D_MODEL)),
            _const_spec((MLA_HEADS, MLA_KV_RANK, LANES)),
            _const_spec((SB_WIDTH, D_MODEL)),
            _const_spec((MLA_WIDTH, D_MODEL)),
            _const_spec((D_MODEL, D_MODEL)),
        ],
        out_specs=pl.BlockSpec((tm, D_MODEL), row),
        out_shape=jax.ShapeDtypeStruct((t, D_MODEL), F32),
        compiler_params=_params(("parallel",)),
        name="merge_out",
    )(x, o_sb, o_lat, mw["norm_mix"], mw["w_gates"], mw["w_uv"], mw["w_sb_o"], mw["w_mla_o"], mw["w_out"])


def _dense_ffn_kernel(x_ref, nf_ref, wg_ref, wu_ref, wd_ref, nfin_ref, o_ref, *, n_split, final_norm):
    x = x_ref[...]
    xn = _rms(x, nf_ref[...]).astype(BF16)
    fc = wg_ref.shape[1] // n_split
    acc = x
    for c in range(n_split):
        gt = _dot(xn, wg_ref[:, c * fc:(c + 1) * fc])
        up = _dot(xn, wu_ref[:, c * fc:(c + 1) * fc])
        hid = (gt * jax.nn.sigmoid(gt) * up).astype(BF16)
        acc = acc + _dot(hid, wd_ref[c * fc:(c + 1) * fc, :])
    o_ref[...] = _rms(acc, nfin_ref[...]) if final_norm else acc


def _dense_ffn_call(x, norm_ffn, wg, wu, wd, norm_final, final_norm):
    t = x.shape[0]
    tm = TOKEN_TILE
    d_ff = wg.shape[1]
    n_split = 2 if d_ff % (2 * LANES) == 0 else 1
    row = lambda i: (i, 0)
    return pl.pallas_call(
        functools.partial(_dense_ffn_kernel, n_split=n_split, final_norm=final_norm),
        grid=(t // tm,),
        in_specs=[
            pl.BlockSpec((tm, D_MODEL), row),
            _const_spec((1, D_MODEL)),
            _const_spec((D_MODEL, d_ff)),
            _const_spec((D_MODEL, d_ff)),
            _const_spec((d_ff, D_MODEL)),
            _const_spec((1, D_MODEL)),
        ],
        out_specs=pl.BlockSpec((tm, D_MODEL), row),
        out_shape=jax.ShapeDtypeStruct((t, D_MODEL), F32),
        compiler_params=_params(("parallel",)),
        name="dense_ffn",
    )(x, norm_ffn, wg, wu, wd, norm_final)


def _router_kernel(x_ref, nf_ref, rw_ref, rb_ref, xw_ref, e_ref, g_ref):
    xn = _rms(x_ref[...], nf_ref[...])
    bits = pltpu.bitcast(xn.astype(BF16).astype(F32), jnp.uint32)
    half = D_MODEL // 2
    xw_ref[...] = (bits[:, :half] >> 16) | (bits[:, half:] & jnp.uint32(0xFFFF0000))

    logits = jnp.dot(xn, rw_ref[...], preferred_element_type=F32, precision=lax.Precision.HIGHEST)
    logits = logits + rb_ref[...]
    lane = lax.broadcasted_iota(jnp.int32, logits.shape, 1).astype(F32)
    lg = jnp.where(lane < N_EXPERTS, logits, -jnp.inf)
    m1 = jnp.max(lg, axis=1, keepdims=True)
    i1 = jnp.min(jnp.where(lg == m1, lane, float(LANES)), axis=1, keepdims=True)
    lg2 = jnp.where(lane == i1, -jnp.inf, lg)
    m2 = jnp.max(lg2, axis=1, keepdims=True)
    i2 = jnp.min(jnp.where(lg2 == m2, lane, float(LANES)), axis=1, keepdims=True)
    e2 = jnp.exp(m2 - m1)
    den = 1.0 + e2
    e_ref[...] = jnp.concatenate([i1, i2], axis=1).astype(jnp.int32)
    g_ref[...] = jnp.concatenate([1.0 / den, e2 / den], axis=1)


def _router_call(x, norm_ffn, rw, rb):
    t = x.shape[0]
    tm = TOKEN_TILE
    row = lambda i: (i, 0)
    return pl.pallas_call(
        _router_kernel,
        grid=(t // tm,),
        in_specs=[
            pl.BlockSpec((tm, D_MODEL), row),
            _const_spec((1, D_MODEL)),
            _const_spec((D_MODEL, LANES)),
            _const_spec((1, LANES)),
        ],
        out_specs=[
            pl.BlockSpec((tm, D_MODEL // 2), row),
            pl.BlockSpec((tm, TOP_K), row),
            pl.BlockSpec((tm, TOP_K), row),
        ],
        out_shape=[
            jax.ShapeDtypeStruct((t, D_MODEL // 2), jnp.uint32),
            jax.ShapeDtypeStruct((t, TOP_K), jnp.int32),
            jax.ShapeDtypeStruct((t, TOP_K), F32),
        ],
        compiler_params=_params(("parallel",)),
        name="router",
    )(x, norm_ffn, rw, rb)


def _row_copy(src, dst, s, d, sem):
    return pltpu.make_async_copy(src.at[pl.ds(s, 1), :], dst.at[pl.ds(d, 1), :], sem)


def _dispatch_kernel(dest_ref, xw_ref, xs_in_ref, xs_ref, sem):
    del xs_in_ref
    tm = xw_ref.shape[0]

    def start(t, _):
        for k in range(TOP_K):
            _row_copy(xw_ref, xs_ref, t, dest_ref[TOP_K * t + k], sem).start()
        return 0

    def wait(t, _):
        for k in range(TOP_K):
            _row_copy(xw_ref, xs_ref, 0, 0, sem).wait()
        return 0

    lax.fori_loop(0, tm, start, 0)
    lax.fori_loop(0, tm, wait, 0)


def _dispatch_call(dest_flat, xw, xs):
    t = xw.shape[0]
    tm = TOKEN_TILE
    return pl.pallas_call(
        _dispatch_kernel,
        grid=(t // tm,),
        in_specs=[
            pl.BlockSpec((tm * TOP_K,), lambda i: (i,), memory_space=pltpu.SMEM),
            pl.BlockSpec((tm, D_MODEL // 2), lambda i: (i, 0)),
            pl.BlockSpec(memory_space=pl.ANY),
        ],
        out_specs=pl.BlockSpec(memory_space=pl.ANY),
        out_shape=jax.ShapeDtypeStruct(xs.shape, xs.dtype),
        scratch_shapes=[pltpu.SemaphoreType.DMA],
        input_output_aliases={2: 0},
        compiler_params=_params(("arbitrary",)),
        name="moe_dispatch",
    )(dest_flat, xw, xs)


def _expert_kernel(tile_e_ref, n_used_ref, xs_ref, wg_ref, wu_ref, wd_ref, y_ref, x_ref):
    r = pl.program_id(0)
    f = pl.program_id(1)

    @pl.when(f == 0)
    def _():
        w = xs_ref[...]
        lo = pltpu.bitcast(w << 16, F32).astype(BF16)
        hi = pltpu.bitcast(w & jnp.uint32(0xFFFF0000), F32).astype(BF16)
        x_ref[...] = jnp.concatenate([lo, hi], axis=-1)
        y_ref[...] = jnp.zeros_like(y_ref)

    @pl.when(r < n_used_ref[0])
    def _():
        x = x_ref[...]
        gt = _dot(x, wg_ref[...])
        up = _dot(x, wu_ref[...])
        hid = (gt * jax.nn.sigmoid(gt) * up).astype(BF16)
        y_ref[...] += _dot(hid, wd_ref[...])


def _expert_call(tile_e, n_used, xs, wg, wu, wd):
    n_rows = xs.shape[0]
    rt = EXPERT_ROW_TILE
    nf = EXPERT_F_SPLIT
    d_e = wg.shape[2]
    fc = d_e // nf

    def fidx(r, f):
        return jnp.where(r % 2 == 0, f, nf - 1 - f)

    grid_spec = pltpu.PrefetchScalarGridSpec(
        num_scalar_prefetch=2,
        grid=(n_rows // rt, nf),
        in_specs=[
            pl.BlockSpec((rt, D_MODEL // 2), lambda r, f, te, nu: (r, 0)),
            pl.BlockSpec((None, D_MODEL, fc), lambda r, f, te, nu: (te[r], 0, fidx(r, f))),
            pl.BlockSpec((None, D_MODEL, fc), lambda r, f, te, nu: (te[r], 0, fidx(r, f))),
            pl.BlockSpec((None, fc, D_MODEL), lambda r, f, te, nu: (te[r], fidx(r, f), 0)),
        ],
        out_specs=pl.BlockSpec((rt, D_MODEL), lambda r, f, te, nu: (r, 0)),
        scratch_shapes=[pltpu.VMEM((rt, D_MODEL), BF16)],
    )
    return pl.pallas_call(
        _expert_kernel,
        grid_spec=grid_spec,
        out_shape=jax.ShapeDtypeStruct((n_rows, D_MODEL), F32),
        compiler_params=_params(("arbitrary", "arbitrary")),
        name="moe_experts",
    )(tile_e, n_used, xs, wg, wu, wd)


def _combine_kernel(dest_ref, h_ref, g_ref, y_ref, nfin_ref, o_ref, buf0, buf1, sem, *, final_norm):
    tm = h_ref.shape[0]

    def start(t, _):
        _row_copy(y_ref, buf0, dest_ref[TOP_K * t], t, sem).start()
        _row_copy(y_ref, buf1, dest_ref[TOP_K * t + 1], t, sem).start()
        return 0

    def wait(t, _):
        _row_copy(y_ref, buf0, 0, 0, sem).wait()
        _row_copy(y_ref, buf1, 0, 0, sem).wait()
        return 0

    lax.fori_loop(0, tm, start, 0)
    lax.fori_loop(0, tm, wait, 0)
    g = g_ref[...]
    y = h_ref[...] + (buf0[...] * g[:, 0:1] + buf1[...] * g[:, 1:2])
    o_ref[...] = _rms(y, nfin_ref[...]) if final_norm else y


def _combine_call(dest_flat, h, gate, yb, norm_final, final_norm):
    t = h.shape[0]
    tm = TOKEN_TILE
    row = lambda i: (i, 0)
    return pl.pallas_call(
        functools.partial(_combine_kernel, final_norm=final_norm),
        grid=(t // tm,),
        in_specs=[
            pl.BlockSpec((tm * TOP_K,), lambda i: (i,), memory_space=pltpu.SMEM),
            pl.BlockSpec((tm, D_MODEL), row),
            pl.BlockSpec((tm, TOP_K), row),
            pl.BlockSpec(memory_space=pl.ANY),
            _const_spec((1, D_MODEL)),
        ],
        out_specs=pl.BlockSpec((tm, D_MODEL), row),
        out_shape=jax.ShapeDtypeStruct((t, D_MODEL), F32),
        scratch_shapes=[pltpu.VMEM((tm, D_MODEL), F32), pltpu.VMEM((tm, D_MODEL), F32),
                        pltpu.SemaphoreType.DMA],
        compiler_params=_params(("arbitrary",)),
        name="moe_combine",
    )(dest_flat, h, gate, yb, norm_final)


def _moe_layout(top_e):
    rt = EXPERT_ROW_TILE
    e_flat = top_e.reshape(-1)
    n_assign = e_flat.shape[0]
    onehot = (e_flat[:, None] == jnp.arange(N_EXPERTS, dtype=jnp.int32)[None, :]).astype(jnp.int32)
    incl = jnp.cumsum(onehot, axis=0)
    counts = incl[-1]
    rank = jnp.sum((incl - onehot) * onehot, axis=1)
    padded = (counts + rt - 1) // rt * rt
    pad_end = jnp.cumsum(padded)
    pad_start = pad_end - padded
    dest = (jnp.sum(onehot * pad_start[None, :], axis=1) + rank).astype(jnp.int32)
    n_rows = -(-(n_assign + N_EXPERTS * (rt - 1)) // rt) * rt
    n_tiles = n_rows // rt
    tile_e = jnp.minimum(
        jnp.searchsorted(pad_end, jnp.arange(n_tiles, dtype=jnp.int32) * rt, side="right"),
        N_EXPERTS - 1).astype(jnp.int32)
    n_used = (pad_end[-1] // rt).astype(jnp.int32).reshape(1)
    return dest, tile_e, n_used, n_rows


def _moe(streams, norm_ffn, rw, rb, wg, wu, wd, norm_final, final_norm):
    routed = [_router_call(h, norm_ffn, rw, rb) for h in streams]
    top_e = jnp.concatenate([r[1] for r in routed], axis=0)
    dest, tile_e, n_used, n_rows = _moe_layout(top_e)
    xs = jnp.zeros((n_rows, D_MODEL // 2), jnp.uint32)
    dests = []
    off = 0
    for h, (xw, _, _) in zip(streams, routed):
        n = h.shape[0] * TOP_K
        dests.append(dest[off:off + n])
        off += n
        xs = _dispatch_call(dests[-1], xw, xs)
    yb = _expert_call(tile_e, n_used, xs, wg, wu, wd)
    return [_combine_call(d, h, r[2], yb, norm_final, final_norm)
            for d, h, r in zip(dests, streams, routed)]


def _mixer_weights(l, norm_mix, w_in, mla_q_norm, w_uq, mla_kv_norm, w_uk, w_uv, w_sb_o, w_mla_o, w_out):
    w = w_in[l]
    o_gate = 3 * SB_WIDTH + MLA_Q_RANK + MLA_KV_RANK + MLA_ROPE_DIM
    w1 = jnp.concatenate(
        [w[:, :o_gate], jnp.zeros((D_MODEL, LANES - MLA_ROPE_DIM), F32)], axis=1).astype(BF16)
    uq = w_uq[l].reshape(MLA_Q_RANK, MLA_HEADS, MLA_NOPE_DIM + MLA_ROPE_DIM)
    uq = jnp.concatenate([
        uq[:, :, :MLA_NOPE_DIM].reshape(MLA_Q_RANK, -1),
        uq[:, :, MLA_NOPE_DIM:MLA_NOPE_DIM + ROPE_HALF].reshape(MLA_Q_RANK, -1),
        uq[:, :, MLA_NOPE_DIM + ROPE_HALF:].reshape(MLA_Q_RANK, -1)], axis=1).astype(BF16)
    uv = jnp.transpose(w_uv[l], (1, 0, 2))
    zeros = jnp.zeros_like(uv)
    even = (jnp.arange(MLA_HEADS) % 2 == 0)[:, None, None]
    uv = jnp.concatenate([jnp.where(even, uv, zeros), jnp.where(even, zeros, uv)], axis=2).astype(BF16)
    return {
        "norm_mix": norm_mix[l].reshape(1, D_MODEL),
        "w1": w1,
        "w_gates": w[:, o_gate:].astype(BF16),
        "q_norm": mla_q_norm[l].reshape(1, MLA_Q_RANK),
        "w_uq": uq,
        "kv_norm": mla_kv_norm[l].reshape(1, MLA_KV_RANK),
        "w_ukt": jnp.transpose(w_uk[l], (1, 2, 0)).astype(BF16),
        "w_uv": uv,
        "w_sb_o": w_sb_o[l].astype(BF16),
        "w_mla_o": w_mla_o[l].astype(BF16),
        "w_out": w_out[l].astype(BF16),
    }


def _rope_tables(pos, rows):
    inv_freq = jnp.power(ROPE_BASE, -jnp.arange(ROPE_HALF, dtype=F32) / ROPE_HALF)
    ang = pos.astype(F32)[:, None] * inv_freq[None, :]
    reps = (rows // pos.shape[0], LANES // ROPE_HALF)
    return jnp.tile(jnp.cos(ang), reps), jnp.tile(jnp.sin(ang), reps)


def kernel(x_prompt, x_sample, cache_sb_k, cache_sb_v, cache_mla_ckv, cache_mla_krope, norm_mix, w_in,
           mla_q_norm, w_uq, mla_kv_norm, w_uk, w_uv, w_sb_o, w_mla_o, w_out, norm_ffn, ffn_w_gate,
           ffn_w_up, ffn_w_down, router_w, router_b, moe_w_gate, moe_w_up, moe_w_down, norm_final):
    n_p, seq, _ = x_prompt.shape
    n_s, n_q, _ = x_sample.shape
    depth = w_in.shape[0]
    past = cache_sb_k.shape[2]
    assert seq % TOKEN_TILE == 0 and (n_s * n_q) % TOKEN_TILE == 0 and TOKEN_TILE % n_q == 0
    assert seq % KEY_TILE == 0 and past % DECODE_KEY_BLOCK == 0 and n_q <= LANES

    cos_p, sin_p = _rope_tables(jnp.arange(seq, dtype=jnp.int32), seq)
    cos_s, sin_s = _rope_tables(past + jnp.arange(n_q, dtype=jnp.int32), TOKEN_TILE)
    cache_k = cache_sb_k.reshape(depth, n_s, past, SB_WIDTH)
    cache_v = cache_sb_v.reshape(depth, n_s, past, SB_WIDTH)
    nfin = norm_final.reshape(1, D_MODEL)

    hp = x_prompt.reshape(n_p * seq, D_MODEL)
    hs = x_sample.reshape(n_s * n_q, D_MODEL)
    rows_p, rows_s = [], []
    for l in range(depth):
        mw = _mixer_weights(l, norm_mix, w_in, mla_q_norm, w_uq, mla_kv_norm, w_uk, w_uv,
                            w_sb_o, w_mla_o, w_out)
        sbq, sbk, sbv, ckv, krope, qlat, qrope = _proj_call(hp, mw, cos_p, sin_p)
        o_sb = _sb_prompt_call(sbq, sbk, sbv, n_p, seq)
        o_lat = _mla_prompt_call(qlat, qrope, ckv, krope, n_p, seq)
        hp = _merge_call(hp, o_sb, o_lat, mw)
        rows_p.append((sbk, sbv, ckv, krope))

        sbq, sbk, sbv, ckv, krope, qlat, qrope = _proj_call(hs, mw, cos_s, sin_s)
        o_sb = _sb_decode_call(sbq, sbk, sbv, cache_k, cache_v, l, n_q)
        o_lat = _mla_decode_call(qlat, qrope, ckv, krope, cache_mla_ckv, cache_mla_krope, l, n_q)
        hs = _merge_call(hs, o_sb, o_lat, mw)
        rows_s.append((sbk, sbv, ckv, krope))

        last = l == depth - 1
        nf = norm_ffn[l].reshape(1, D_MODEL)
        i = l // 2
        if l % 2 == 0:
            wg, wu, wd = ffn_w_gate[i].astype(BF16), ffn_w_up[i].astype(BF16), ffn_w_down[i].astype(BF16)
            hp = _dense_ffn_call(hp, nf, wg, wu, wd, nfin, last)
            hs = _dense_ffn_call(hs, nf, wg, wu, wd, nfin, last)
        else:
            rw = jnp.concatenate([router_w[i], jnp.zeros((D_MODEL, LANES - N_EXPERTS), F32)], axis=1)
            rb = jnp.concatenate([router_b[i], jnp.zeros((LANES - N_EXPERTS,), F32)]).reshape(1, LANES)
            hp, hs = _moe([hp, hs], nf, rw, rb, moe_w_gate[i].astype(BF16), moe_w_up[i].astype(BF16),
                          moe_w_down[i].astype(BF16), nfin, last)

    def stack(rows, idx, shape):
        return jnp.stack([r[idx] for r in rows]).reshape((depth,) + shape)

    kv_p = (n_p, seq, SB_HEADS, SB_HEAD_DIM)
    kv_s = (n_s, n_q, SB_HEADS, SB_HEAD_DIM)
    return (hp.reshape(n_p, seq, D_MODEL), hs.reshape(n_s, n_q, D_MODEL),
            stack(rows_p, 0, kv_p), stack(rows_p, 1, kv_p),
            stack(rows_p, 2, (n_p, seq, MLA_KV_RANK)), stack(rows_p, 3, (n_p, seq, MLA_ROPE_DIM)),
            stack(rows_s, 0, kv_s), stack(rows_s, 1, kv_s),
            stack(rows_s, 2, (n_s, n_q, MLA_KV_RANK)), stack(rows_s, 3, (n_s, n_q, MLA_ROPE_DIM)))
```

```python
import functools

import jax
import jax.numpy as jnp
from jax import lax
from jax.experimental import pallas as pl
from jax.experimental.pallas import tpu as pltpu

F32 = jnp.float32
BF16 = jnp.bfloat16

D_MODEL = 1024
CHUNK = 64
SB_HEADS = 8
SB_HEAD_DIM = 64
SB_WIDTH = SB_HEADS * SB_HEAD_DIM
MLA_HEADS = 8
MLA_NOPE_DIM = 64
MLA_ROPE_DIM = 32
MLA_V_DIM = 64
MLA_Q_RANK = 384
MLA_KV_RANK = 256
MLA_WIDTH = MLA_HEADS * MLA_V_DIM
MLA_SCALE = (MLA_NOPE_DIM + MLA_ROPE_DIM) ** -0.5
SB_SCALE = SB_HEAD_DIM ** -0.5
ROPE_BASE = 10000.0
ROPE_HALF = MLA_ROPE_DIM // 2
N_EXPERTS = 8
TOP_K = 2
RMS_EPS = 1e-6
NEG_INF = -1e30
NEG_LOG2_E = -1.4426950408889634

LANES = 128
MXU_DIM = 256
VMEM_LIMIT_BYTES = 56 * 1024 * 1024

TOKEN_TILE = 256
Q_TILE = 128
KEY_TILE = MXU_DIM
DECODE_KEY_BLOCK = 1024
EXPERT_ROW_TILE = 512
EXPERT_F_SPLIT = 2
IN_PROJ_COLS = 3 * SB_WIDTH + MLA_Q_RANK + MLA_KV_RANK + LANES


def _params(semantics):
    return pltpu.CompilerParams(dimension_semantics=semantics, vmem_limit_bytes=VMEM_LIMIT_BYTES)


def _rms(x, g):
    return x * lax.rsqrt(jnp.mean(x * x, axis=-1, keepdims=True) + RMS_EPS) * g


def _dot(a, b):
    return jnp.dot(a, b, preferred_element_type=F32)


def _dot_nt(a, b):
    return lax.dot_general(a, b, (((1,), (1,)), ((), ())), preferred_element_type=F32)


def _softplus(z):
    return jnp.maximum(z, 0.0) + jnp.log(1.0 + jnp.exp2(jnp.abs(z) * NEG_LOG2_E))


def _later_sum_matrix(n):
    r = lax.broadcasted_iota(jnp.int32, (n, n), 0)
    c = lax.broadcasted_iota(jnp.int32, (n, n), 1)
    return jnp.where(r > c, 1.0, 0.0).astype(BF16)


def _later_sums(lk, tri):
    hi = lk.astype(BF16)
    lo = (lk - hi.astype(F32)).astype(BF16)
    return _dot(hi, tri) + _dot(lo, tri)


def _const_spec(shape):
    nd = len(shape)
    return pl.BlockSpec(shape, lambda *_: (0,) * nd)


def _proj_kernel(x_ref, nm_ref, w1_ref, qn_ref, wuq_ref, kvn_ref, wukt_ref, cos_ref, sin_ref,
                 sbq_ref, sbk_ref, sbv_ref, ckv_ref, krope_ref, qlat_ref, qrope_ref):
    xn = _rms(x_ref[...], nm_ref[...]).astype(BF16)
    p = _dot(xn, w1_ref[...])
    o = 0
    sbq_ref[...] = (p[:, o:o + SB_WIDTH] * SB_SCALE).astype(BF16)
    o += SB_WIDTH
    sbk_ref[...] = p[:, o:o + SB_WIDTH]
    o += SB_WIDTH
    sbv_ref[...] = p[:, o:o + SB_WIDTH]
    o += SB_WIDTH
    cq = p[:, o:o + MLA_Q_RANK]
    o += MLA_Q_RANK
    ckv_ref[...] = _rms(p[:, o:o + MLA_KV_RANK], kvn_ref[...])
    o += MLA_KV_RANK
    kr = p[:, o:o + LANES]
    cos = cos_ref[...]
    sin = sin_ref[...]
    c16 = cos[:, :ROPE_HALF]
    s16 = sin[:, :ROPE_HALF]
    k1 = kr[:, :ROPE_HALF]
    k2 = kr[:, ROPE_HALF:MLA_ROPE_DIM]
    krope_ref[...] = jnp.concatenate([k1 * c16 - k2 * s16, k2 * c16 + k1 * s16], axis=-1)

    cqn = _rms(cq, qn_ref[...]).astype(BF16)
    q = _dot(cqn, wuq_ref[...])
    nope_w = MLA_HEADS * MLA_NOPE_DIM
    x1 = q[:, nope_w:nope_w + LANES]
    x2 = q[:, nope_w + LANES:nope_w + 2 * LANES]
    r1 = (x1 * cos - x2 * sin) * MLA_SCALE
    r2 = (x2 * cos + x1 * sin) * MLA_SCALE
    for h in range(MLA_HEADS):
        qn_h = q[:, h * MLA_NOPE_DIM:(h + 1) * MLA_NOPE_DIM].astype(BF16)
        qlat_ref[h] = (_dot(qn_h, wukt_ref[h]) * MLA_SCALE).astype(BF16)
        qrope_ref[h] = jnp.concatenate(
            [r1[:, h * ROPE_HALF:(h + 1) * ROPE_HALF], r2[:, h * ROPE_HALF:(h + 1) * ROPE_HALF]],
            axis=-1).astype(BF16)


def _proj_call(x, mw, cos_tab, sin_tab):
    t = x.shape[0]
    tm = TOKEN_TILE
    n_pos_tiles = cos_tab.shape[0] // tm
    row = lambda i: (i, 0)
    pos = lambda i: (i % n_pos_tiles, 0)
    head_row = lambda i: (0, i, 0)
    return pl.pallas_call(
        _proj_kernel,
        grid=(t // tm,),
        in_specs=[
            pl.BlockSpec((tm, D_MODEL), row),
            _const_spec((1, D_MODEL)),
            _const_spec((D_MODEL, IN_PROJ_COLS)),
            _const_spec((1, MLA_Q_RANK)),
            _const_spec((MLA_Q_RANK, MLA_HEADS * (MLA_NOPE_DIM + MLA_ROPE_DIM))),
            _const_spec((1, MLA_KV_RANK)),
            _const_spec((MLA_HEADS, MLA_NOPE_DIM, MLA_KV_RANK)),
            pl.BlockSpec((tm, LANES), pos),
            pl.BlockSpec((tm, LANES), pos),
        ],
        out_specs=[
            pl.BlockSpec((tm, SB_WIDTH), row),
            pl.BlockSpec((tm, SB_WIDTH), row),
            pl.BlockSpec((tm, SB_WIDTH), row),
            pl.BlockSpec((tm, MLA_KV_RANK), row),
            pl.BlockSpec((tm, MLA_ROPE_DIM), row),
            pl.BlockSpec((MLA_HEADS, tm, MLA_KV_RANK), head_row),
            pl.BlockSpec((MLA_HEADS, tm, MLA_ROPE_DIM), head_row),
        ],
        out_shape=[
            jax.ShapeDtypeStruct((t, SB_WIDTH), BF16),
            jax.ShapeDtypeStruct((t, SB_WIDTH), F32),
            jax.ShapeDtypeStruct((t, SB_WIDTH), F32),
            jax.ShapeDtypeStruct((t, MLA_KV_RANK), F32),
            jax.ShapeDtypeStruct((t, MLA_ROPE_DIM), F32),
            jax.ShapeDtypeStruct((MLA_HEADS, t, MLA_KV_RANK), BF16),
            jax.ShapeDtypeStruct((MLA_HEADS, t, MLA_ROPE_DIM), BF16),
        ],
        compiler_params=_params(("parallel",)),
        name="in_proj",
    )(x, mw["norm_mix"], mw["w1"], mw["q_norm"], mw["w_uq"], mw["kv_norm"], mw["w_ukt"],
      cos_tab, sin_tab)


def _sb_weights_many(zs, seen, tri, runs):
    sps = [_softplus(z) for z in zs]
    ts = sps if seen is None else [jnp.where(seen, sp, 0.0) for sp in sps]
    his = [t.astype(BF16) for t in ts]
    los = [(t - hi.astype(F32)).astype(BF16) for t, hi in zip(ts, his)]
    laters = [_dot(hi, tri) + _dot(lo, tri) for hi, lo in zip(his, los)]
    ws = []
    for z, sp, later, run in zip(zs, sps, laters, runs):
        w = jnp.exp((z - sp) - (later + run))
        if seen is not None:
            w = jnp.where(seen, w, 0.0)
        ws.append(w.astype(BF16))
    return ws, [jnp.sum(t, axis=1, keepdims=True) for t in ts]


def _sb_weights(z, seen, tri, run):
    ws, tots = _sb_weights_many([z], seen, tri, [run])
    return ws[0], tots[0]


def _sb_prompt_kernel(q_ref, k_ref, v_ref, o_ref, kbf_ref, vbf_ref, qm_ref, acc_ref, run_ref):
    qi = pl.program_id(1)
    n_pairs = SB_HEADS // 2

    @pl.when(qi == 0)
    def _():
        kbf_ref[...] = k_ref[...].astype(BF16)
        vbf_ref[...] = v_ref[...].astype(BF16)

    tri = _later_sum_matrix(KEY_TILE)
    n_kb = (qi * Q_TILE + Q_TILE + KEY_TILE - 1) // KEY_TILE
    lane = lax.broadcasted_iota(jnp.int32, (Q_TILE, LANES), 1)
    first_head = lane < SB_HEAD_DIM
    for pair in range(n_pairs):
        q_pair = q_ref[:, pair * LANES:(pair + 1) * LANES]
        zero = jnp.zeros_like(q_pair)
        qm_ref[2 * pair] = jnp.where(first_head, q_pair, zero)
        qm_ref[2 * pair + 1] = jnp.where(first_head, zero, q_pair)
    acc_ref[...] = jnp.zeros(acc_ref.shape, F32)
    run_ref[...] = jnp.zeros(run_ref.shape, F32)

    def block(start, seen):
        heads = range(SB_HEADS)
        runs = [run_ref[h] for h in heads]
        zs = [_dot_nt(qm_ref[h], kbf_ref[pl.ds(start, KEY_TILE), (h // 2) * LANES:(h // 2 + 1) * LANES])
              for h in heads]
        ws, tots = _sb_weights_many(zs, seen, tri, runs)
        pv = [_dot(ws[h], vbf_ref[pl.ds(start, KEY_TILE), (h // 2) * LANES:(h // 2 + 1) * LANES])
              for h in heads]
        for h in heads:
            run_ref[h] = runs[h] + tots[h]
        for pair in range(n_pairs):
            acc_ref[pair] += jnp.where(first_head, pv[2 * pair], pv[2 * pair + 1])

    last = pl.multiple_of((n_kb - 1) * KEY_TILE, KEY_TILE)
    q_pos = qi * Q_TILE + lax.broadcasted_iota(jnp.int32, (Q_TILE, KEY_TILE), 0)
    k_pos = last + lax.broadcasted_iota(jnp.int32, (Q_TILE, KEY_TILE), 1)
    block(last, k_pos < q_pos)

    def body(j, _):
        block(pl.multiple_of((n_kb - 1 - j) * KEY_TILE, KEY_TILE), None)
        return 0

    lax.fori_loop(1, n_kb, body, 0)
    for pair in range(n_pairs):
        o_ref[:, pair * LANES:(pair + 1) * LANES] = acc_ref[pair].astype(BF16)


def _sb_prompt_call(q, k, v, n_streams, seq):
    nq = seq // Q_TILE
    return pl.pallas_call(
        _sb_prompt_kernel,
        grid=(n_streams, nq),
        in_specs=[
            pl.BlockSpec((Q_TILE, SB_WIDTH), lambda b, i: (b * nq + i, 0)),
            pl.BlockSpec((seq, SB_WIDTH), lambda b, i: (b, 0)),
            pl.BlockSpec((seq, SB_WIDTH), lambda b, i: (b, 0)),
        ],
        out_specs=pl.BlockSpec((Q_TILE, SB_WIDTH), lambda b, i: (b * nq + i, 0)),
        out_shape=jax.ShapeDtypeStruct(q.shape, BF16),
        scratch_shapes=[
            pltpu.VMEM((seq, SB_WIDTH), BF16),
            pltpu.VMEM((seq, SB_WIDTH), BF16),
            pltpu.VMEM((SB_HEADS, Q_TILE, LANES), BF16),
            pltpu.VMEM((SB_HEADS // 2, Q_TILE, LANES), F32),
            pltpu.VMEM((SB_HEADS, Q_TILE, 1), F32),
        ],
        compiler_params=_params(("parallel", "arbitrary")),
        name="sb_prompt",
    )(q, k, v)


def _mla_prompt_kernel(ql_ref, qr_ref, ckv_ref, kr_ref, o_ref, cbf_ref, rbf_ref, m_ref, l_ref, acc_ref):
    qi = pl.program_id(1)

    @pl.when(qi == 0)
    def _():
        cbf_ref[...] = ckv_ref[...].astype(BF16)
        rbf_ref[...] = kr_ref[...].astype(BF16)

    m_ref[...] = jnp.full(m_ref.shape, NEG_INF, F32)
    l_ref[...] = jnp.zeros(l_ref.shape, F32)
    acc_ref[...] = jnp.zeros(acc_ref.shape, F32)
    n_kb = (qi * Q_TILE + Q_TILE + KEY_TILE - 1) // KEY_TILE

    def block(start, visible):
        c = cbf_ref[pl.ds(start, KEY_TILE), :]
        r = rbf_ref[pl.ds(start, KEY_TILE), :]
        heads = range(MLA_HEADS)
        m_old = [m_ref[h] for h in heads]
        l_old = [l_ref[h] for h in heads]
        ss = [_dot_nt(ql_ref[h], c) + _dot_nt(qr_ref[h], r) for h in heads]
        if visible is not None:
            ss = [jnp.where(visible, s, NEG_INF) for s in ss]
        m_new = [jnp.maximum(m_old[h], jnp.max(jnp.maximum(ss[h][:, :LANES], ss[h][:, LANES:]),
                                               axis=1, keepdims=True)) for h in heads]
        p_lo = [jnp.exp(ss[h][:, :LANES] - m_new[h]) for h in heads]
        p_hi = [jnp.exp(ss[h][:, LANES:] - m_new[h]) for h in heads]
        alpha = [jnp.exp(m_old[h] - m_new[h]) for h in heads]
        pv = [_dot(jnp.concatenate([p_lo[h], p_hi[h]], axis=1).astype(BF16), c) for h in heads]
        for h in heads:
            m_ref[h] = m_new[h]
            l_ref[h] = alpha[h][:, :1] * l_old[h] + jnp.sum(p_lo[h] + p_hi[h], axis=1, keepdims=True)
        for h in heads:
            acc_ref[h] = jnp.concatenate([alpha[h], alpha[h]], axis=1) * acc_ref[h] + pv[h]

    def body(j, _):
        block(pl.multiple_of(j * KEY_TILE, KEY_TILE), None)
        return 0

    lax.fori_loop(0, n_kb - 1, body, 0)
    last = pl.multiple_of((n_kb - 1) * KEY_TILE, KEY_TILE)
    q_chunk = (qi * Q_TILE + lax.broadcasted_iota(jnp.int32, (Q_TILE, KEY_TILE), 0)) // CHUNK
    k_chunk = (last + lax.broadcasted_iota(jnp.int32, (Q_TILE, KEY_TILE), 1)) // CHUNK
    block(last, k_chunk <= q_chunk)
    for h in range(MLA_HEADS):
        o_ref[h] = (acc_ref[h] / l_ref[h]).astype(BF16)


def _mla_prompt_call(qlat, qrope, ckv, krope, n_streams, seq):
    nq = seq // Q_TILE
    qmap = lambda b, i: (0, b * nq + i, 0)
    return pl.pallas_call(
        _mla_prompt_kernel,
        grid=(n_streams, nq),
        in_specs=[
            pl.BlockSpec((MLA_HEADS, Q_TILE, MLA_KV_RANK), qmap),
            pl.BlockSpec((MLA_HEADS, Q_TILE, MLA_ROPE_DIM), qmap),
            pl.BlockSpec((seq, MLA_KV_RANK), lambda b, i: (b, 0)),
            pl.BlockSpec((seq, MLA_ROPE_DIM), lambda b, i: (b, 0)),
        ],
        out_specs=pl.BlockSpec((MLA_HEADS, Q_TILE, MLA_KV_RANK), qmap),
        out_shape=jax.ShapeDtypeStruct(qlat.shape, BF16),
        scratch_shapes=[
            pltpu.VMEM((seq, MLA_KV_RANK), BF16),
            pltpu.VMEM((seq, MLA_ROPE_DIM), BF16),
            pltpu.VMEM((MLA_HEADS, Q_TILE, LANES), F32),
            pltpu.VMEM((MLA_HEADS, Q_TILE, 1), F32),
            pltpu.VMEM((MLA_HEADS, Q_TILE, MLA_KV_RANK), F32),
        ],
        compiler_params=_params(("parallel", "arbitrary")),
        name="mla_prompt",
    )(qlat, qrope, ckv, krope)


def _sb_weights_chain(zs, tri, run):
    sps = [_softplus(z) for z in zs]
    his = [sp.astype(BF16) for sp in sps]
    los = [(sp - hi.astype(F32)).astype(BF16) for sp, hi in zip(sps, his)]
    laters = [_dot(hi, tri) + _dot(lo, tri) for hi, lo in zip(his, los)]
    ws = []
    for z, sp, later in zip(zs, sps, laters):
        ws.append(jnp.exp((z - sp) - (later + run)).astype(BF16))
        run = run + jnp.sum(sp, axis=1, keepdims=True)
    return ws, run


def _sb_decode_kernel(q_ref, kn_ref, vn_ref, kc_ref, vc_ref, o_ref, acc_ref, run_ref):
    kb = pl.program_id(1)
    n_q = q_ref.shape[0]
    rows = SB_HEADS * n_q
    heads = range(SB_HEADS)
    tri = _later_sum_matrix(KEY_TILE)
    q = q_ref[...]
    qh = [q[:, h * SB_HEAD_DIM:(h + 1) * SB_HEAD_DIM] for h in heads]

    def scores(keys_of):
        return jnp.concatenate([_dot_nt(qh[h], keys_of(h)) for h in heads], axis=0)

    def weighted(w, values_of):
        return jnp.concatenate([_dot(w[h * n_q:(h + 1) * n_q], values_of(h)) for h in heads], axis=0)

    @pl.when(kb == 0)
    def _():
        pad = jnp.zeros((LANES - n_q, SB_HEAD_DIM), BF16)
        kn = kn_ref[...].astype(BF16)
        vn = vn_ref[...].astype(BF16)
        head = lambda a, h: jnp.concatenate([a[:, h * SB_HEAD_DIM:(h + 1) * SB_HEAD_DIM], pad], axis=0)
        z = scores(lambda h: head(kn, h))
        i = lax.broadcasted_iota(jnp.int32, (rows, LANES), 0) % n_q
        j = lax.broadcasted_iota(jnp.int32, (rows, LANES), 1)
        w, tot = _sb_weights(z, j < i, tri[:LANES, :LANES], jnp.zeros((rows, 1), F32))
        acc_ref[...] = weighted(w, lambda h: head(vn, h))
        run_ref[...] = tot

    n_keys = kc_ref.shape[0] // SB_HEADS
    n_sub = n_keys // KEY_TILE
    of_head = lambda ref, h: ref[pl.ds(h, n_keys, stride=SB_HEADS), :].astype(BF16)
    z = scores(lambda h: of_head(kc_ref, h))
    subs = list(range(n_sub - 1, -1, -1))
    ws, run = _sb_weights_chain([z[:, s * KEY_TILE:(s + 1) * KEY_TILE] for s in subs], tri, run_ref[...])
    w = jnp.concatenate(ws[::-1], axis=1)
    acc_ref[...] += weighted(w, lambda h: of_head(vc_ref, h))
    run_ref[...] = run

    @pl.when(kb == pl.num_programs(1) - 1)
    def _():
        a = acc_ref[...]
        o_ref[...] = jnp.concatenate([a[h * n_q:(h + 1) * n_q] for h in heads], axis=1).astype(BF16)


def _sb_decode_call(q, k_new, v_new, cache_k, cache_v, layer, n_q):
    t = q.shape[0]
    n_streams = t // n_q
    depth, _, past = cache_k.shape[:3]
    cache_k = cache_k.reshape(depth, n_streams, past * SB_HEADS, SB_HEAD_DIM)
    cache_v = cache_v.reshape(depth, n_streams, past * SB_HEADS, SB_HEAD_DIM)
    n_kb = past // DECODE_KEY_BLOCK
    row = lambda s, j: (s, 0)
    cmap = lambda s, j: (layer, s, n_kb - 1 - j, 0)
    rows = SB_HEADS * n_q
    cache_block = (None, None, DECODE_KEY_BLOCK * SB_HEADS, SB_HEAD_DIM)
    return pl.pallas_call(
        _sb_decode_kernel,
        grid=(n_streams, n_kb),
        in_specs=[
            pl.BlockSpec((n_q, SB_WIDTH), row),
            pl.BlockSpec((n_q, SB_WIDTH), row),
            pl.BlockSpec((n_q, SB_WIDTH), row),
            pl.BlockSpec(cache_block, cmap),
            pl.BlockSpec(cache_block, cmap),
        ],
        out_specs=pl.BlockSpec((n_q, SB_WIDTH), row),
        out_shape=jax.ShapeDtypeStruct(q.shape, BF16),
        scratch_shapes=[pltpu.VMEM((rows, SB_HEAD_DIM), F32), pltpu.VMEM((rows, 1), F32)],
        compiler_params=_params(("parallel", "arbitrary")),
        name="sb_decode",
    )(q, k_new, v_new, cache_k, cache_v)


def _mla_decode_kernel(ql_ref, qr_ref, cn_ref, rn_ref, cc_ref, rc_ref, o_ref, m_ref, l_ref, acc_ref,
                       *, past):
    kb = pl.program_id(1)
    n_q = ql_ref.shape[1]
    rows = MLA_HEADS * n_q
    ql = ql_ref[...].reshape(rows, MLA_KV_RANK)
    qr = qr_ref[...].reshape(rows, MLA_ROPE_DIM)

    def update(s, c):
        m_old = m_ref[...]
        m_new = jnp.maximum(m_old, jnp.max(s, axis=1, keepdims=True))
        p = jnp.exp(s - m_new)
        alpha = jnp.exp(m_old - m_new)
        l_ref[...] = alpha * l_ref[...] + jnp.sum(p, axis=1, keepdims=True)
        acc_ref[...] = alpha * acc_ref[...] + _dot(p.astype(BF16), c)
        m_ref[...] = m_new

    @pl.when(kb == 0)
    def _():
        m_ref[...] = jnp.full((rows, 1), NEG_INF, F32)
        l_ref[...] = jnp.zeros((rows, 1), F32)
        acc_ref[...] = jnp.zeros((rows, MLA_KV_RANK), F32)
        c = jnp.concatenate([cn_ref[...].astype(BF16), jnp.zeros((LANES - n_q, MLA_KV_RANK), BF16)], axis=0)
        r = jnp.concatenate([rn_ref[...].astype(BF16), jnp.zeros((LANES - n_q, MLA_ROPE_DIM), BF16)], axis=0)
        s = _dot_nt(ql, c) + _dot_nt(qr, r)
        i = lax.broadcasted_iota(jnp.int32, (rows, LANES), 0) % n_q
        j = lax.broadcasted_iota(jnp.int32, (rows, LANES), 1)
        vis = jnp.logical_and(j < n_q, (past + j) // CHUNK <= (past + i) // CHUNK)
        update(jnp.where(vis, s, NEG_INF), c)

    c = cc_ref[...].astype(BF16)
    r = rc_ref[...].astype(BF16)
    update(_dot_nt(ql, c) + _dot_nt(qr, r), c)

    @pl.when(kb == pl.num_programs(1) - 1)
    def _():
        o = acc_ref[...] / l_ref[...]
        o_ref[...] = o.reshape(MLA_HEADS, n_q, MLA_KV_RANK).astype(BF16)


def _mla_decode_call(qlat, qrope, ckv_new, krope_new, cache_ckv, cache_krope, layer, n_q):
    t = qlat.shape[1]
    n_streams = t // n_q
    past = cache_ckv.shape[2]
    n_kb = past // DECODE_KEY_BLOCK
    rows = MLA_HEADS * n_q
    qmap = lambda s, j: (0, s, 0)
    row = lambda s, j: (s, 0)
    cmap = lambda s, j: (layer, s, j, 0)
    return pl.pallas_call(
        functools.partial(_mla_decode_kernel, past=past),
        grid=(n_streams, n_kb),
        in_specs=[
            pl.BlockSpec((MLA_HEADS, n_q, MLA_KV_RANK), qmap),
            pl.BlockSpec((MLA_HEADS, n_q, MLA_ROPE_DIM), qmap),
            pl.BlockSpec((n_q, MLA_KV_RANK), row),
            pl.BlockSpec((n_q, MLA_ROPE_DIM), row),
            pl.BlockSpec((None, None, DECODE_KEY_BLOCK, MLA_KV_RANK), cmap),
            pl.BlockSpec((None, None, DECODE_KEY_BLOCK, MLA_ROPE_DIM), cmap),
        ],
        out_specs=pl.BlockSpec((MLA_HEADS, n_q, MLA_KV_RANK), qmap),
        out_shape=jax.ShapeDtypeStruct(qlat.shape, BF16),
        scratch_shapes=[
            pltpu.VMEM((rows, 1), F32),
            pltpu.VMEM((rows, 1), F32),
            pltpu.VMEM((rows, MLA_KV_RANK), F32),
        ],
        compiler_params=_params(("parallel", "arbitrary")),
        name="mla_decode",
    )(qlat, qrope, ckv_new, krope_new, cache_ckv, cache_krope)


def _merge_kernel(x_ref, osb_ref, olat_ref, nm_ref, wg_ref, wuv_ref, wsbo_ref, wmlao_ref, wout_ref, h_ref):
    x = x_ref[...]
    xn = _rms(x, nm_ref[...]).astype(BF16)
    g = _dot(xn, wg_ref[...])
    a = _dot(osb_ref[...], wsbo_ref[...])
    pieces = []
    for pair in range(MLA_HEADS // 2):
        pieces.append(_dot(olat_ref[2 * pair], wuv_ref[2 * pair])
                      + _dot(olat_ref[2 * pair + 1], wuv_ref[2 * pair + 1]))
    o_mla = jnp.concatenate(pieces, axis=-1).astype(BF16)
    b = _dot(o_mla, wmlao_ref[...])
    merged = jax.nn.sigmoid(g[:, :D_MODEL]) * a + jax.nn.sigmoid(g[:, D_MODEL:]) * b
    h_ref[...] = x + _dot(merged.astype(BF16), wout_ref[...])


def _merge_call(x, o_sb, o_lat, mw):
    t = x.shape[0]
    tm = TOKEN_TILE
    row = lambda i: (i, 0)
    return pl.pallas_call(
        _merge_kernel,
        grid=(t // tm,),
        in_specs=[
            pl.BlockSpec((tm, D_MODEL), row),
            pl.BlockSpec((tm, SB_WIDTH), row),
            pl.BlockSpec((MLA_HEADS, tm, MLA_KV_RANK), lambda i: (0, i, 0)),
            _const_spec((1, D_MODEL)),
            _const_spec((D_MODEL, 2 * D_MODEL)),
            _const_spec((MLA_HEADS, MLA_KV_RANK, LANES)),
            _const_spec((SB_WIDTH, D_MODEL)),
            _const_spec((MLA_WIDTH, D_MODEL)),
            _const_spec((D_MODEL, D_MODEL)),
        ],
        out_specs=pl.BlockSpec((tm, D_MODEL), row),
        out_shape=jax.ShapeDtypeStruct((t, D_MODEL), F32),
        compiler_params=_params(("parallel",)),
        name="merge_out",
    )(x, o_sb, o_lat, mw["norm_mix"], mw["w_gates"], mw["w_uv"], mw["w_sb_o"], mw["w_mla_o"], mw["w_out"])


def _dense_ffn_kernel(x_ref, nf_ref, wg_ref, wu_ref, wd_ref, nfin_ref, o_ref, *, n_split, final_norm):
    x = x_ref[...]
    xn = _rms(x, nf_ref[...]).astype(BF16)
    fc = wg_ref.shape[1] // n_split
    acc = x
    for c in range(n_split):
        gt = _dot(xn, wg_ref[:, c * fc:(c + 1) * fc])
        up = _dot(xn, wu_ref[:, c * fc:(c + 1) * fc])
        hid = (gt * jax.nn.sigmoid(gt) * up).astype(BF16)
        acc = acc + _dot(hid, wd_ref[c * fc:(c + 1) * fc, :])
    o_ref[...] = _rms(acc, nfin_ref[...]) if final_norm else acc


def _dense_ffn_call(x, norm_ffn, wg, wu, wd, norm_final, final_norm):
    t = x.shape[0]
    tm = TOKEN_TILE
    d_ff = wg.shape[1]
    n_split = 2 if d_ff % (2 * LANES) == 0 else 1
    row = lambda i: (i, 0)
    return pl.pallas_call(
        functools.partial(_dense_ffn_kernel, n_split=n_split, final_norm=final_norm),
        grid=(t // tm,),
        in_specs=[
            pl.BlockSpec((tm, D_MODEL), row),
            _const_spec((1, D_MODEL)),
            _const_spec((D_MODEL, d_ff)),
            _const_spec((D_MODEL, d_ff)),
            _const_spec((d_ff, D_MODEL)),
            _const_spec((1, D_MODEL)),
        ],
        out_specs=pl.BlockSpec((tm, D_MODEL), row),
        out_shape=jax.ShapeDtypeStruct((t, D_MODEL), F32),
        compiler_params=_params(("parallel",)),
        name="dense_ffn",
    )(x, norm_ffn, wg, wu, wd, norm_final)


def _router_kernel(x_ref, nf_ref, rw_ref, rb_ref, xw_ref, e_ref, g_ref):
    xn = _rms(x_ref[...], nf_ref[...])
    xw_ref[...] = xn

    logits = jnp.dot(xn, rw_ref[...], preferred_element_type=F32, precision=lax.Precision.HIGHEST)
    logits = logits + rb_ref[...]
    lane = lax.broadcasted_iota(jnp.int32, logits.shape, 1).astype(F32)
    lg = jnp.where(lane < N_EXPERTS, logits, -jnp.inf)
    m1 = jnp.max(lg, axis=1, keepdims=True)
    i1 = jnp.min(jnp.where(lg == m1, lane, float(LANES)), axis=1, keepdims=True)
    lg2 = jnp.where(lane == i1, -jnp.inf, lg)
    m2 = jnp.max(lg2, axis=1, keepdims=True)
    i2 = jnp.min(jnp.where(lg2 == m2, lane, float(LANES)), axis=1, keepdims=True)
    e2 = jnp.exp(m2 - m1)
    den = 1.0 + e2
    e_ref[...] = jnp.concatenate([i1, i2], axis=1).astype(jnp.int32)
    g_ref[...] = jnp.concatenate([1.0 / den, e2 / den], axis=1)


def _router_call(x, norm_ffn, rw, rb):
    t = x.shape[0]
    tm = TOKEN_TILE
    row = lambda i: (i, 0)
    return pl.pallas_call(
        _router_kernel,
        grid=(t // tm,),
        in_specs=[
            pl.BlockSpec((tm, D_MODEL), row),
            _const_spec((1, D_MODEL)),
            _const_spec((D_MODEL, LANES)),
            _const_spec((1, LANES)),
        ],
        out_specs=[
            pl.BlockSpec((tm, D_MODEL), row),
            pl.BlockSpec((tm, TOP_K), row),
            pl.BlockSpec((tm, TOP_K), row),
        ],
        out_shape=[
            jax.ShapeDtypeStruct((t, D_MODEL), F32),
            jax.ShapeDtypeStruct((t, TOP_K), jnp.int32),
            jax.ShapeDtypeStruct((t, TOP_K), F32),
        ],
        compiler_params=_params(("parallel",)),
        name="router",
    )(x, norm_ffn, rw, rb)


def _row_copy(src, dst, s, d, sem):
    return pltpu.make_async_copy(src.at[pl.ds(s, 1), :], dst.at[pl.ds(d, 1), :], sem)


def _dispatch_kernel(dest_ref, xw_ref, xs_in_ref, xs_ref, sem):
    del xs_in_ref
    tm = xw_ref.shape[0]

    def start(t, _):
        for k in range(TOP_K):
            _row_copy(xw_ref, xs_ref, t, dest_ref[TOP_K * t + k], sem).start()
        return 0

    def wait(t, _):
        for k in range(TOP_K):
            _row_copy(xw_ref, xs_ref, 0, 0, sem).wait()
        return 0

    lax.fori_loop(0, tm, start, 0, unroll=8)
    lax.fori_loop(0, tm, wait, 0, unroll=8)


def _dispatch_call(dest_flat, xw, xs):
    t = xw.shape[0]
    tm = TOKEN_TILE
    return pl.pallas_call(
        _dispatch_kernel,
        grid=(t // tm,),
        in_specs=[
            pl.BlockSpec((tm * TOP_K,), lambda i: (i,), memory_space=pltpu.SMEM),
            pl.BlockSpec((tm, D_MODEL), lambda i: (i, 0)),
            pl.BlockSpec(memory_space=pl.ANY),
        ],
        out_specs=pl.BlockSpec(memory_space=pl.ANY),
        out_shape=jax.ShapeDtypeStruct(xs.shape, xs.dtype),
        scratch_shapes=[pltpu.SemaphoreType.DMA],
        input_output_aliases={2: 0},
        compiler_params=_params(("arbitrary",)),
        name="moe_dispatch",
    )(dest_flat, xw, xs)


def _expert_kernel(tile_e_ref, n_used_ref, xs_ref, wg_ref, wu_ref, wd_ref, y_ref, x_ref):
    r = pl.program_id(0)
    f = pl.program_id(1)

    @pl.when(f == 0)
    def _():
        x_ref[...] = xs_ref[...].astype(BF16)
        y_ref[...] = jnp.zeros_like(y_ref)

    @pl.when(r < n_used_ref[0])
    def _():
        x = x_ref[...]
        gt = _dot(x, wg_ref[...])
        up = _dot(x, wu_ref[...])
        hid = (gt * jax.nn.sigmoid(gt) * up).astype(BF16)
        y_ref[...] += _dot(hid, wd_ref[...])


def _expert_call(tile_e, n_used, xs, wg, wu, wd):
    n_rows = xs.shape[0]
    rt = EXPERT_ROW_TILE
    nf = EXPERT_F_SPLIT
    d_e = wg.shape[2]
    fc = d_e // nf

    def fidx(r, f):
        return jnp.where(r % 2 == 0, f, nf - 1 - f)

    grid_spec = pltpu.PrefetchScalarGridSpec(
        num_scalar_prefetch=2,
        grid=(n_rows // rt, nf),
        in_specs=[
            pl.BlockSpec((rt, D_MODEL), lambda r, f, te, nu: (r, 0)),
            pl.BlockSpec((None, D_MODEL, fc), lambda r, f, te, nu: (te[r], 0, fidx(r, f))),
            pl.BlockSpec((None, D_MODEL, fc), lambda r, f, te, nu: (te[r], 0, fidx(r, f))),
            pl.BlockSpec((None, fc, D_MODEL), lambda r, f, te, nu: (te[r], fidx(r, f), 0)),
        ],
        out_specs=pl.BlockSpec((rt, D_MODEL), lambda r, f, te, nu: (r, 0)),
        scratch_shapes=[pltpu.VMEM((rt, D_MODEL), BF16)],
    )
    return pl.pallas_call(
        _expert_kernel,
        grid_spec=grid_spec,
        out_shape=jax.ShapeDtypeStruct((n_rows, D_MODEL), F32),
        compiler_params=_params(("arbitrary", "arbitrary")),
        name="moe_experts",
    )(tile_e, n_used, xs, wg, wu, wd)


def _combine_kernel(dest_ref, h_ref, g_ref, y_ref, nfin_ref, o_ref, buf0, buf1, sem, *, final_norm):
    tm = h_ref.shape[0]

    def start(t, _):
        _row_copy(y_ref, buf0, dest_ref[TOP_K * t], t, sem).start()
        _row_copy(y_ref, buf1, dest_ref[TOP_K * t + 1], t, sem).start()
        return 0

    def wait(t, _):
        _row_copy(y_ref, buf0, 0, 0, sem).wait()
        _row_copy(y_ref, buf1, 0, 0, sem).wait()
        return 0

    lax.fori_loop(0, tm, start, 0, unroll=8)
    lax.fori_loop(0, tm, wait, 0, unroll=8)
    g = g_ref[...]
    y = h_ref[...] + (buf0[...] * g[:, 0:1] + buf1[...] * g[:, 1:2])
    o_ref[...] = _rms(y, nfin_ref[...]) if final_norm else y


def _combine_call(dest_flat, h, gate, yb, norm_final, final_norm):
    t = h.shape[0]
    tm = TOKEN_TILE
    row = lambda i: (i, 0)
    return pl.pallas_call(
        functools.partial(_combine_kernel, final_norm=final_norm),
        grid=(t // tm,),
        in_specs=[
            pl.BlockSpec((tm * TOP_K,), lambda i: (i,), memory_space=pltpu.SMEM),
            pl.BlockSpec((tm, D_MODEL), row),
            pl.BlockSpec((tm, TOP_K), row),
            pl.BlockSpec(memory_space=pl.ANY),
            _const_spec((1, D_MODEL)),
        ],
        out_specs=pl.BlockSpec((tm, D_MODEL), row),
        out_shape=jax.ShapeDtypeStruct((t, D_MODEL), F32),
        scratch_shapes=[pltpu.VMEM((tm, D_MODEL), F32), pltpu.VMEM((tm, D_MODEL), F32),
                        pltpu.SemaphoreType.DMA],
        compiler_params=_params(("arbitrary",)),
        name="moe_combine",
    )(dest_flat, h, gate, yb, norm_final)


def _moe_layout(top_e):
    rt = EXPERT_ROW_TILE
    e_flat = top_e.reshape(-1)
    n_assign = e_flat.shape[0]
    onehot = (e_flat[:, None] == jnp.arange(N_EXPERTS, dtype=jnp.int32)[None, :]).astype(jnp.int32)
    incl = jnp.cumsum(onehot, axis=0)
    counts = incl[-1]
    rank = jnp.sum((incl - onehot) * onehot, axis=1)
    padded = (counts + rt - 1) // rt * rt
    pad_end = jnp.cumsum(padded)
    pad_start = pad_end - padded
    dest = (jnp.sum(onehot * pad_start[None, :], axis=1) + rank).astype(jnp.int32)
    n_rows = -(-(n_assign + N_EXPERTS * (rt - 1)) // rt) * rt
    n_tiles = n_rows // rt
    tile_start = jnp.arange(n_tiles, dtype=jnp.int32) * rt
    tile_e = jnp.minimum(
        jnp.sum((pad_end[None, :] <= tile_start[:, None]).astype(jnp.int32), axis=1),
        N_EXPERTS - 1).astype(jnp.int32)
    n_used = (pad_end[-1] // rt).astype(jnp.int32).reshape(1)
    return dest, tile_e, n_used, n_rows


def _moe(streams, norm_ffn, rw, rb, wg, wu, wd, norm_final, final_norm):
    routed = [_router_call(h, norm_ffn, rw, rb) for h in streams]
    top_e = jnp.concatenate([r[1] for r in routed], axis=0)
    dest, tile_e, n_used, n_rows = _moe_layout(top_e)
    xs = jnp.zeros((n_rows, D_MODEL), F32)
    dests = []
    off = 0
    for h, (xw, _, _) in zip(streams, routed):
        n = h.shape[0] * TOP_K
        dests.append(dest[off:off + n])
        off += n
        xs = _dispatch_call(dests[-1], xw, xs)
    yb = _expert_call(tile_e, n_used, xs, wg, wu, wd)
    return [_combine_call(d, h, r[2], yb, norm_final, final_norm)
            for d, h, r in zip(dests, streams, routed)]


def _mixer_weights(l, norm_mix, w_in, mla_q_norm, w_uq, mla_kv_norm, w_uk, w_uv, w_sb_o, w_mla_o, w_out):
    w = w_in[l]
    o_gate = 3 * SB_WIDTH + MLA_Q_RANK + MLA_KV_RANK + MLA_ROPE_DIM
    w1 = jnp.concatenate(
        [w[:, :o_gate], jnp.zeros((D_MODEL, LANES - MLA_ROPE_DIM), F32)], axis=1).astype(BF16)
    uq = w_uq[l].reshape(MLA_Q_RANK, MLA_HEADS, MLA_NOPE_DIM + MLA_ROPE_DIM)
    uq = jnp.concatenate([
        uq[:, :, :MLA_NOPE_DIM].reshape(MLA_Q_RANK, -1),
        uq[:, :, MLA_NOPE_DIM:MLA_NOPE_DIM + ROPE_HALF].reshape(MLA_Q_RANK, -1),
        uq[:, :, MLA_NOPE_DIM + ROPE_HALF:].reshape(MLA_Q_RANK, -1)], axis=1).astype(BF16)
    uv = jnp.transpose(w_uv[l], (1, 0, 2))
    zeros = jnp.zeros_like(uv)
    even = (jnp.arange(MLA_HEADS) % 2 == 0)[:, None, None]
    uv = jnp.concatenate([jnp.where(even, uv, zeros), jnp.where(even, zeros, uv)], axis=2).astype(BF16)
    return {
        "norm_mix": norm_mix[l].reshape(1, D_MODEL),
        "w1": w1,
        "w_gates": w[:, o_gate:].astype(BF16),
        "q_norm": mla_q_norm[l].reshape(1, MLA_Q_RANK),
        "w_uq": uq,
        "kv_norm": mla_kv_norm[l].reshape(1, MLA_KV_RANK),
        "w_ukt": jnp.transpose(w_uk[l], (1, 2, 0)).astype(BF16),
        "w_uv": uv,
        "w_sb_o": w_sb_o[l].astype(BF16),
        "w_mla_o": w_mla_o[l].astype(BF16),
        "w_out": w_out[l].astype(BF16),
    }


def _rope_tables(pos, rows):
    inv_freq = jnp.power(ROPE_BASE, -jnp.arange(ROPE_HALF, dtype=F32) / ROPE_HALF)
    ang = pos.astype(F32)[:, None] * inv_freq[None, :]
    reps = (rows // pos.shape[0], LANES // ROPE_HALF)
    return jnp.tile(jnp.cos(ang), reps), jnp.tile(jnp.sin(ang), reps)


def kernel(x_prompt, x_sample, cache_sb_k, cache_sb_v, cache_mla_ckv, cache_mla_krope, norm_mix, w_in,
           mla_q_norm, w_uq, mla_kv_norm, w_uk, w_uv, w_sb_o, w_mla_o, w_out, norm_ffn, ffn_w_gate,
           ffn_w_up, ffn_w_down, router_w, router_b, moe_w_gate, moe_w_up, moe_w_down, norm_final):
    n_p, seq, _ = x_prompt.shape
    n_s, n_q, _ = x_sample.shape
    depth = w_in.shape[0]
    past = cache_sb_k.shape[2]
    assert seq % TOKEN_TILE == 0 and (n_s * n_q) % TOKEN_TILE == 0 and TOKEN_TILE % n_q == 0
    assert seq % KEY_TILE == 0 and past % DECODE_KEY_BLOCK == 0 and n_q <= LANES

    cos_p, sin_p = _rope_tables(jnp.arange(seq, dtype=jnp.int32), seq)
    cos_s, sin_s = _rope_tables(past + jnp.arange(n_q, dtype=jnp.int32), TOKEN_TILE)
    nfin = norm_final.reshape(1, D_MODEL)

    hp = x_prompt.reshape(n_p * seq, D_MODEL)
    hs = x_sample.reshape(n_s * n_q, D_MODEL)
    rows_p, rows_s = [], []
    for l in range(depth):
        mw = _mixer_weights(l, norm_mix, w_in, mla_q_norm, w_uq, mla_kv_norm, w_uk, w_uv,
                            w_sb_o, w_mla_o, w_out)
        sbq, sbk, sbv, ckv, krope, qlat, qrope = _proj_call(hp, mw, cos_p, sin_p)
        o_sb = _sb_prompt_call(sbq, sbk, sbv, n_p, seq)
        o_lat = _mla_prompt_call(qlat, qrope, ckv, krope, n_p, seq)
        hp = _merge_call(hp, o_sb, o_lat, mw)
        rows_p.append((sbk, sbv, ckv, krope))

        sbq, sbk, sbv, ckv, krope, qlat, qrope = _proj_call(hs, mw, cos_s, sin_s)
        o_sb = _sb_decode_call(sbq, sbk, sbv, cache_sb_k, cache_sb_v, l, n_q)
        o_lat = _mla_decode_call(qlat, qrope, ckv, krope, cache_mla_ckv, cache_mla_krope, l, n_q)
        hs = _merge_call(hs, o_sb, o_lat, mw)
        rows_s.append((sbk, sbv, ckv, krope))

        last = l == depth - 1
        nf = norm_ffn[l].reshape(1, D_MODEL)
        i = l // 2
        if l % 2 == 0:
            wg, wu, wd = ffn_w_gate[i].astype(BF16), ffn_w_up[i].astype(BF16), ffn_w_down[i].astype(BF16)
            hp = _dense_ffn_call(hp, nf, wg, wu, wd, nfin, last)
            hs = _dense_ffn_call(hs, nf, wg, wu, wd, nfin, last)
        else:
            rw = jnp.concatenate([router_w[i], jnp.zeros((D_MODEL, LANES - N_EXPERTS), F32)], axis=1)
            rb = jnp.concatenate([router_b[i], jnp.zeros((LANES - N_EXPERTS,), F32)]).reshape(1, LANES)
            hp, hs = _moe([hp, hs], nf, rw, rb, moe_w_gate[i].astype(BF16), moe_w_up[i].astype(BF16),
                          moe_w_down[i].astype(BF16), nfin, last)

    def stack(rows, idx, shape):
        return jnp.stack([r[idx] for r in rows]).reshape((depth,) + shape)

    kv_p = (n_p, seq, SB_HEADS, SB_HEAD_DIM)
    kv_s = (n_s, n_q, SB_HEADS, SB_HEAD_DIM)
    return (hp.reshape(n_p, seq, D_MODEL), hs.reshape(n_s, n_q, D_MODEL),
            stack(rows_p, 0, kv_p), stack(rows_p, 1, kv_p),
            stack(rows_p, 2, (n_p, seq, MLA_KV_RANK)), stack(rows_p, 3, (n_p, seq, MLA_ROPE_DIM)),
            stack(rows_s, 0, kv_s), stack(rows_s, 1, kv_s),
            stack(rows_s, 2, (n_s, n_q, MLA_KV_RANK)), stack(rows_s, 3, (n_s, n_q, MLA_ROPE_DIM)))
```

```python
import functools

import jax
import jax.numpy as jnp
from jax import lax
from jax.experimental import pallas as pl
from jax.experimental.pallas import tpu as pltpu

F32 = jnp.float32
BF16 = jnp.bfloat16

D_MODEL = 1024
CHUNK = 64
SB_HEADS = 8
SB_HEAD_DIM = 64
SB_WIDTH = SB_HEADS * SB_HEAD_DIM
MLA_HEADS = 8
MLA_NOPE_DIM = 64
MLA_ROPE_DIM = 32
MLA_V_DIM = 64
MLA_Q_RANK = 384
MLA_KV_RANK = 256
MLA_WIDTH = MLA_HEADS * MLA_V_DIM
MLA_SCALE = (MLA_NOPE_DIM + MLA_ROPE_DIM) ** -0.5
SB_SCALE = SB_HEAD_DIM ** -0.5
ROPE_BASE = 10000.0
ROPE_HALF = MLA_ROPE_DIM // 2
N_EXPERTS = 8
TOP_K = 2
RMS_EPS = 1e-6
NEG_INF = -1e30
NEG_LOG2_E = -1.4426950408889634

LANES = 128
MXU_DIM = 256
VMEM_LIMIT_BYTES = 56 * 1024 * 1024

TOKEN_TILE = 256
Q_TILE = 128
KEY_TILE = MXU_DIM
DECODE_KEY_BLOCK = 1024
EXPERT_ROW_TILE = 512
EXPERT_F_SPLIT = 2
IN_PROJ_COLS = 3 * SB_WIDTH + MLA_Q_RANK + MLA_KV_RANK + LANES


def _params(semantics):
    return pltpu.CompilerParams(dimension_semantics=semantics, vmem_limit_bytes=VMEM_LIMIT_BYTES)


def _rms(x, g):
    return x * lax.rsqrt(jnp.mean(x * x, axis=-1, keepdims=True) + RMS_EPS) * g


def _dot(a, b):
    return jnp.dot(a, b, preferred_element_type=F32)


def _dot_nt(a, b):
    return lax.dot_general(a, b, (((1,), (1,)), ((), ())), preferred_element_type=F32)


def _softplus(z):
    return jnp.maximum(z, 0.0) + jnp.log(1.0 + jnp.exp2(jnp.abs(z) * NEG_LOG2_E))


def _later_sum_matrix(n):
    r = lax.broadcasted_iota(jnp.int32, (n, n), 0)
    c = lax.broadcasted_iota(jnp.int32, (n, n), 1)
    return jnp.where(r > c, 1.0, 0.0).astype(BF16)


def _const_spec(shape):
    nd = len(shape)
    return pl.BlockSpec(shape, lambda *_: (0,) * nd)


def _proj_kernel(x_ref, nm_ref, w1_ref, qn_ref, wuq_ref, kvn_ref, wukt_ref, cos_ref, sin_ref,
                 sbq_ref, sbk_ref, sbv_ref, ckv_ref, krope_ref, qlat_ref, qrope_ref):
    xn = _rms(x_ref[...], nm_ref[...]).astype(BF16)
    p = _dot(xn, w1_ref[...])
    o = 0
    sbq_ref[...] = (p[:, o:o + SB_WIDTH] * SB_SCALE).astype(BF16)
    o += SB_WIDTH
    sbk_ref[...] = p[:, o:o + SB_WIDTH]
    o += SB_WIDTH
    sbv_ref[...] = p[:, o:o + SB_WIDTH]
    o += SB_WIDTH
    cq = p[:, o:o + MLA_Q_RANK]
    o += MLA_Q_RANK
    ckv_ref[...] = _rms(p[:, o:o + MLA_KV_RANK], kvn_ref[...])
    o += MLA_KV_RANK
    kr = p[:, o:o + LANES]
    cos = cos_ref[...]
    sin = sin_ref[...]
    c16 = cos[:, :ROPE_HALF]
    s16 = sin[:, :ROPE_HALF]
    k1 = kr[:, :ROPE_HALF]
    k2 = kr[:, ROPE_HALF:MLA_ROPE_DIM]
    krope_ref[...] = jnp.concatenate([k1 * c16 - k2 * s16, k2 * c16 + k1 * s16], axis=-1)

    cqn = _rms(cq, qn_ref[...]).astype(BF16)
    q = _dot(cqn, wuq_ref[...])
    nope_w = MLA_HEADS * MLA_NOPE_DIM
    x1 = q[:, nope_w:nope_w + LANES]
    x2 = q[:, nope_w + LANES:nope_w + 2 * LANES]
    r1 = (x1 * cos - x2 * sin) * MLA_SCALE
    r2 = (x2 * cos + x1 * sin) * MLA_SCALE
    for h in range(MLA_HEADS):
        qn_h = q[:, h * MLA_NOPE_DIM:(h + 1) * MLA_NOPE_DIM].astype(BF16)
        qlat_ref[h] = (_dot(qn_h, wukt_ref[h]) * MLA_SCALE).astype(BF16)
        qrope_ref[h] = jnp.concatenate(
            [r1[:, h * ROPE_HALF:(h + 1) * ROPE_HALF], r2[:, h * ROPE_HALF:(h + 1) * ROPE_HALF]],
            axis=-1).astype(BF16)


def _proj_call(x, mw, cos_tab, sin_tab):
    t = x.shape[0]
    tm = TOKEN_TILE
    n_pos_tiles = cos_tab.shape[0] // tm
    row = lambda i: (i, 0)
    pos = lambda i: (i % n_pos_tiles, 0)
    head_row = lambda i: (0, i, 0)
    return pl.pallas_call(
        _proj_kernel,
        grid=(t // tm,),
        in_specs=[
            pl.BlockSpec((tm, D_MODEL), row),
            _const_spec((1, D_MODEL)),
            _const_spec((D_MODEL, IN_PROJ_COLS)),
            _const_spec((1, MLA_Q_RANK)),
            _const_spec((MLA_Q_RANK, MLA_HEADS * (MLA_NOPE_DIM + MLA_ROPE_DIM))),
            _const_spec((1, MLA_KV_RANK)),
            _const_spec((MLA_HEADS, MLA_NOPE_DIM, MLA_KV_RANK)),
            pl.BlockSpec((tm, LANES), pos),
            pl.BlockSpec((tm, LANES), pos),
        ],
        out_specs=[
            pl.BlockSpec((tm, SB_WIDTH), row),
            pl.BlockSpec((tm, SB_WIDTH), row),
            pl.BlockSpec((tm, SB_WIDTH), row),
            pl.BlockSpec((tm, MLA_KV_RANK), row),
            pl.BlockSpec((tm, MLA_ROPE_DIM), row),
            pl.BlockSpec((MLA_HEADS, tm, MLA_KV_RANK), head_row),
            pl.BlockSpec((MLA_HEADS, tm, MLA_ROPE_DIM), head_row),
        ],
        out_shape=[
            jax.ShapeDtypeStruct((t, SB_WIDTH), BF16),
            jax.ShapeDtypeStruct((t, SB_WIDTH), F32),
            jax.ShapeDtypeStruct((t, SB_WIDTH), F32),
            jax.ShapeDtypeStruct((t, MLA_KV_RANK), F32),
            jax.ShapeDtypeStruct((t, MLA_ROPE_DIM), F32),
            jax.ShapeDtypeStruct((MLA_HEADS, t, MLA_KV_RANK), BF16),
            jax.ShapeDtypeStruct((MLA_HEADS, t, MLA_ROPE_DIM), BF16),
        ],
        compiler_params=_params(("parallel",)),
        name="in_proj",
    )(x, mw["norm_mix"], mw["w1"], mw["q_norm"], mw["w_uq"], mw["kv_norm"], mw["w_ukt"],
      cos_tab, sin_tab)


def _sb_weights_many(zs, seen, tri, runs):
    sps = [_softplus(z) for z in zs]
    ts = sps if seen is None else [jnp.where(seen, sp, 0.0) for sp in sps]
    laters = [_dot(t.astype(BF16), tri) for t in ts]
    ws = []
    for z, sp, later, run in zip(zs, sps, laters, runs):
        w = jnp.exp((z - sp) - (later + run))
        if seen is not None:
            w = jnp.where(seen, w, 0.0)
        ws.append(w.astype(BF16))
    return ws, [jnp.sum(t, axis=1, keepdims=True) for t in ts]


def _sb_weights(z, seen, tri, run):
    ws, tots = _sb_weights_many([z], seen, tri, [run])
    return ws[0], tots[0]


def _sb_prompt_kernel(q_ref, k_ref, v_ref, o_ref, kbf_ref, vbf_ref, qm_ref, acc_ref, run_ref):
    qi = pl.program_id(1)
    n_pairs = SB_HEADS // 2

    @pl.when(qi == 0)
    def _():
        kbf_ref[...] = k_ref[...].astype(BF16)
        vbf_ref[...] = v_ref[...].astype(BF16)

    tri = _later_sum_matrix(KEY_TILE)
    n_kb = (qi * Q_TILE + Q_TILE + KEY_TILE - 1) // KEY_TILE
    lane = lax.broadcasted_iota(jnp.int32, (Q_TILE, LANES), 1)
    first_head = lane < SB_HEAD_DIM
    for pair in range(n_pairs):
        q_pair = q_ref[:, pair * LANES:(pair + 1) * LANES]
        zero = jnp.zeros_like(q_pair)
        qm_ref[2 * pair] = jnp.where(first_head, q_pair, zero)
        qm_ref[2 * pair + 1] = jnp.where(first_head, zero, q_pair)
    acc_ref[...] = jnp.zeros(acc_ref.shape, F32)
    run_ref[...] = jnp.zeros(run_ref.shape, F32)

    def block(start, seen):
        heads = range(SB_HEADS)
        runs = [run_ref[h] for h in heads]
        zs = [_dot_nt(qm_ref[h], kbf_ref[pl.ds(start, KEY_TILE), (h // 2) * LANES:(h // 2 + 1) * LANES])
              for h in heads]
        ws, tots = _sb_weights_many(zs, seen, tri, runs)
        pv = [_dot(ws[h], vbf_ref[pl.ds(start, KEY_TILE), (h // 2) * LANES:(h // 2 + 1) * LANES])
              for h in heads]
        for h in heads:
            run_ref[h] = runs[h] + tots[h]
        for pair in range(n_pairs):
            acc_ref[pair] += jnp.where(first_head, pv[2 * pair], pv[2 * pair + 1])

    last = pl.multiple_of((n_kb - 1) * KEY_TILE, KEY_TILE)
    q_pos = qi * Q_TILE + lax.broadcasted_iota(jnp.int32, (Q_TILE, KEY_TILE), 0)
    k_pos = last + lax.broadcasted_iota(jnp.int32, (Q_TILE, KEY_TILE), 1)
    block(last, k_pos < q_pos)

    def body(j, _):
        block(pl.multiple_of((n_kb - 1 - j) * KEY_TILE, KEY_TILE), None)
        return 0

    lax.fori_loop(1, n_kb, body, 0)
    for pair in range(n_pairs):
        o_ref[:, pair * LANES:(pair + 1) * LANES] = acc_ref[pair].astype(BF16)


def _sb_prompt_call(q, k, v, n_streams, seq):
    nq = seq // Q_TILE
    return pl.pallas_call(
        _sb_prompt_kernel,
        grid=(n_streams, nq),
        in_specs=[
            pl.BlockSpec((Q_TILE, SB_WIDTH), lambda b, i: (b * nq + i, 0)),
            pl.BlockSpec((seq, SB_WIDTH), lambda b, i: (b, 0)),
            pl.BlockSpec((seq, SB_WIDTH), lambda b, i: (b, 0)),
        ],
        out_specs=pl.BlockSpec((Q_TILE, SB_WIDTH), lambda b, i: (b * nq + i, 0)),
        out_shape=jax.ShapeDtypeStruct(q.shape, BF16),
        scratch_shapes=[
            pltpu.VMEM((seq, SB_WIDTH), BF16),
            pltpu.VMEM((seq, SB_WIDTH), BF16),
            pltpu.VMEM((SB_HEADS, Q_TILE, LANES), BF16),
            pltpu.VMEM((SB_HEADS // 2, Q_TILE, LANES), F32),
            pltpu.VMEM((SB_HEADS, Q_TILE, 1), F32),
        ],
        compiler_params=_params(("parallel", "arbitrary")),
        name="sb_prompt",
    )(q, k, v)


def _mla_prompt_kernel(ql_ref, qr_ref, ckv_ref, kr_ref, o_ref, cbf_ref, rbf_ref, m_ref, l_ref, acc_ref):
    qi = pl.program_id(1)

    @pl.when(qi == 0)
    def _():
        cbf_ref[...] = ckv_ref[...].astype(BF16)
        rbf_ref[...] = kr_ref[...].astype(BF16)

    m_ref[...] = jnp.full(m_ref.shape, NEG_INF, F32)
    l_ref[...] = jnp.zeros(l_ref.shape, F32)
    acc_ref[...] = jnp.zeros(acc_ref.shape, F32)
    n_kb = (qi * Q_TILE + Q_TILE + KEY_TILE - 1) // KEY_TILE

    def block(start, visible):
        c = cbf_ref[pl.ds(start, KEY_TILE), :]
        r = rbf_ref[pl.ds(start, KEY_TILE), :]
        heads = range(MLA_HEADS)
        m_old = [m_ref[h] for h in heads]
        l_old = [l_ref[h] for h in heads]
        ss = [_dot_nt(ql_ref[h], c) + _dot_nt(qr_ref[h], r) for h in heads]
        if visible is not None:
            ss = [jnp.where(visible, s, NEG_INF) for s in ss]
        m_new = [jnp.maximum(m_old[h], jnp.max(jnp.maximum(ss[h][:, :LANES], ss[h][:, LANES:]),
                                               axis=1, keepdims=True)) for h in heads]
        p_lo = [jnp.exp(ss[h][:, :LANES] - m_new[h]) for h in heads]
        p_hi = [jnp.exp(ss[h][:, LANES:] - m_new[h]) for h in heads]
        alpha = [jnp.exp(m_old[h] - m_new[h]) for h in heads]
        pv = [_dot(jnp.concatenate([p_lo[h], p_hi[h]], axis=1).astype(BF16), c) for h in heads]
        for h in heads:
            m_ref[h] = m_new[h]
            l_ref[h] = alpha[h][:, :1] * l_old[h] + jnp.sum(p_lo[h] + p_hi[h], axis=1, keepdims=True)
        for h in heads:
            acc_ref[h] = jnp.concatenate([alpha[h], alpha[h]], axis=1) * acc_ref[h] + pv[h]

    def body(j, _):
        block(pl.multiple_of(j * KEY_TILE, KEY_TILE), None)
        return 0

    lax.fori_loop(0, n_kb - 1, body, 0)
    last = pl.multiple_of((n_kb - 1) * KEY_TILE, KEY_TILE)
    q_chunk = (qi * Q_TILE + lax.broadcasted_iota(jnp.int32, (Q_TILE, KEY_TILE), 0)) // CHUNK
    k_chunk = (last + lax.broadcasted_iota(jnp.int32, (Q_TILE, KEY_TILE), 1)) // CHUNK
    block(last, k_chunk <= q_chunk)
    for h in range(MLA_HEADS):
        o_ref[h] = (acc_ref[h] / l_ref[h]).astype(BF16)


def _mla_prompt_call(qlat, qrope, ckv, krope, n_streams, seq):
    nq = seq // Q_TILE
    qmap = lambda b, i: (0, b * nq + i, 0)
    return pl.pallas_call(
        _mla_prompt_kernel,
        grid=(n_streams, nq),
        in_specs=[
            pl.BlockSpec((MLA_HEADS, Q_TILE, MLA_KV_RANK), qmap),
            pl.BlockSpec((MLA_HEADS, Q_TILE, MLA_ROPE_DIM), qmap),
            pl.BlockSpec((seq, MLA_KV_RANK), lambda b, i: (b, 0)),
            pl.BlockSpec((seq, MLA_ROPE_DIM), lambda b, i: (b, 0)),
        ],
        out_specs=pl.BlockSpec((MLA_HEADS, Q_TILE, MLA_KV_RANK), qmap),
        out_shape=jax.ShapeDtypeStruct(qlat.shape, BF16),
        scratch_shapes=[
            pltpu.VMEM((seq, MLA_KV_RANK), BF16),
            pltpu.VMEM((seq, MLA_ROPE_DIM), BF16),
            pltpu.VMEM((MLA_HEADS, Q_TILE, LANES), F32),
            pltpu.VMEM((MLA_HEADS, Q_TILE, 1), F32),
            pltpu.VMEM((MLA_HEADS, Q_TILE, MLA_KV_RANK), F32),
        ],
        compiler_params=_params(("parallel", "arbitrary")),
        name="mla_prompt",
    )(qlat, qrope, ckv, krope)


def _sb_weights_chain(zs, tri, run):
    sps = [_softplus(z) for z in zs]
    laters = [_dot(sp.astype(BF16), tri) for sp in sps]
    ws = []
    for z, sp, later in zip(zs, sps, laters):
        ws.append(jnp.exp((z - sp) - (later + run)).astype(BF16))
        run = run + jnp.sum(sp, axis=1, keepdims=True)
    return ws, run


def _sb_decode_kernel(q_ref, kn_ref, vn_ref, kc_ref, vc_ref, o_ref, acc_ref, run_ref):
    kb = pl.program_id(1)
    n_q = q_ref.shape[0]
    rows = SB_HEADS * n_q
    heads = range(SB_HEADS)
    tri = _later_sum_matrix(KEY_TILE)
    q = q_ref[...]
    qh = [q[:, h * SB_HEAD_DIM:(h + 1) * SB_HEAD_DIM] for h in heads]

    def scores(keys_of):
        return jnp.concatenate([_dot_nt(qh[h], keys_of(h)) for h in heads], axis=0)

    def weighted(w, values_of):
        return jnp.concatenate([_dot(w[h * n_q:(h + 1) * n_q], values_of(h)) for h in heads], axis=0)

    @pl.when(kb == 0)
    def _():
        pad = jnp.zeros((LANES - n_q, SB_HEAD_DIM), BF16)
        kn = kn_ref[...].astype(BF16)
        vn = vn_ref[...].astype(BF16)
        head = lambda a, h: jnp.concatenate([a[:, h * SB_HEAD_DIM:(h + 1) * SB_HEAD_DIM], pad], axis=0)
        z = scores(lambda h: head(kn, h))
        i = lax.broadcasted_iota(jnp.int32, (rows, LANES), 0) % n_q
        j = lax.broadcasted_iota(jnp.int32, (rows, LANES), 1)
        w, tot = _sb_weights(z, j < i, tri[:LANES, :LANES], jnp.zeros((rows, 1), F32))
        acc_ref[...] = weighted(w, lambda h: head(vn, h))
        run_ref[...] = tot

    n_keys = kc_ref.shape[0]
    n_sub = n_keys // KEY_TILE
    kc_ref = kc_ref.reshape(n_keys * SB_HEADS, SB_HEAD_DIM)
    vc_ref = vc_ref.reshape(n_keys * SB_HEADS, SB_HEAD_DIM)
    of_head = lambda ref, h: ref[pl.ds(h, n_keys, stride=SB_HEADS), :].astype(BF16)
    z = scores(lambda h: of_head(kc_ref, h))
    subs = list(range(n_sub - 1, -1, -1))
    ws, run = _sb_weights_chain([z[:, s * KEY_TILE:(s + 1) * KEY_TILE] for s in subs], tri, run_ref[...])
    w = jnp.concatenate(ws[::-1], axis=1)
    acc_ref[...] += weighted(w, lambda h: of_head(vc_ref, h))
    run_ref[...] = run

    @pl.when(kb == pl.num_programs(1) - 1)
    def _():
        a = acc_ref[...]
        o_ref[...] = jnp.concatenate([a[h * n_q:(h + 1) * n_q] for h in heads], axis=1).astype(BF16)


def _sb_decode_call(q, k_new, v_new, cache_k, cache_v, layer, n_q):
    t = q.shape[0]
    n_streams = t // n_q
    past = cache_k.shape[2]
    n_kb = past // DECODE_KEY_BLOCK
    row = lambda s, j: (s, 0)
    cmap = lambda s, j: (layer, s, n_kb - 1 - j, 0, 0)
    rows = SB_HEADS * n_q
    cache_block = (None, None, DECODE_KEY_BLOCK, SB_HEADS, SB_HEAD_DIM)
    return pl.pallas_call(
        _sb_decode_kernel,
        grid=(n_streams, n_kb),
        in_specs=[
            pl.BlockSpec((n_q, SB_WIDTH), row),
            pl.BlockSpec((n_q, SB_WIDTH), row),
            pl.BlockSpec((n_q, SB_WIDTH), row),
            pl.BlockSpec(cache_block, cmap),
            pl.BlockSpec(cache_block, cmap),
        ],
        out_specs=pl.BlockSpec((n_q, SB_WIDTH), row),
        out_shape=jax.ShapeDtypeStruct(q.shape, BF16),
        scratch_shapes=[pltpu.VMEM((rows, SB_HEAD_DIM), F32), pltpu.VMEM((rows, 1), F32)],
        compiler_params=_params(("parallel", "arbitrary")),
        name="sb_decode",
    )(q, k_new, v_new, cache_k, cache_v)


def _mla_decode_kernel(ql_ref, qr_ref, cn_ref, rn_ref, cc_ref, rc_ref, o_ref, m_ref, l_ref, acc_ref,
                       *, past):
    kb = pl.program_id(1)
    n_q = ql_ref.shape[1]
    rows = MLA_HEADS * n_q
    ql = ql_ref[...].reshape(rows, MLA_KV_RANK)
    qr = qr_ref[...].reshape(rows, MLA_ROPE_DIM)

    def update(s, c):
        m_old = m_ref[...]
        m_new = jnp.maximum(m_old, jnp.max(s, axis=1, keepdims=True))
        p = jnp.exp(s - m_new)
        alpha = jnp.exp(m_old - m_new)
        l_ref[...] = alpha * l_ref[...] + jnp.sum(p, axis=1, keepdims=True)
        acc_ref[...] = alpha * acc_ref[...] + _dot(p.astype(BF16), c)
        m_ref[...] = m_new

    @pl.when(kb == 0)
    def _():
        m_ref[...] = jnp.full((rows, 1), NEG_INF, F32)
        l_ref[...] = jnp.zeros((rows, 1), F32)
        acc_ref[...] = jnp.zeros((rows, MLA_KV_RANK), F32)
        c = jnp.concatenate([cn_ref[...].astype(BF16), jnp.zeros((LANES - n_q, MLA_KV_RANK), BF16)], axis=0)
        r = jnp.concatenate([rn_ref[...].astype(BF16), jnp.zeros((LANES - n_q, MLA_ROPE_DIM), BF16)], axis=0)
        s = _dot_nt(ql, c) + _dot_nt(qr, r)
        i = lax.broadcasted_iota(jnp.int32, (rows, LANES), 0) % n_q
        j = lax.broadcasted_iota(jnp.int32, (rows, LANES), 1)
        vis = jnp.logical_and(j < n_q, (past + j) // CHUNK <= (past + i) // CHUNK)
        update(jnp.where(vis, s, NEG_INF), c)

    c = cc_ref[...].astype(BF16)
    r = rc_ref[...].astype(BF16)
    update(_dot_nt(ql, c) + _dot_nt(qr, r), c)

    @pl.when(kb == pl.num_programs(1) - 1)
    def _():
        o = acc_ref[...] / l_ref[...]
        o_ref[...] = o.reshape(MLA_HEADS, n_q, MLA_KV_RANK).astype(BF16)


def _mla_decode_call(qlat, qrope, ckv_new, krope_new, cache_ckv, cache_krope, layer, n_q):
    t = qlat.shape[1]
    n_streams = t // n_q
    past = cache_ckv.shape[2]
    n_kb = past // DECODE_KEY_BLOCK
    rows = MLA_HEADS * n_q
    qmap = lambda s, j: (0, s, 0)
    row = lambda s, j: (s, 0)
    cmap = lambda s, j: (layer, s, j, 0)
    return pl.pallas_call(
        functools.partial(_mla_decode_kernel, past=past),
        grid=(n_streams, n_kb),
        in_specs=[
            pl.BlockSpec((MLA_HEADS, n_q, MLA_KV_RANK), qmap),
            pl.BlockSpec((MLA_HEADS, n_q, MLA_ROPE_DIM), qmap),
            pl.BlockSpec((n_q, MLA_KV_RANK), row),
            pl.BlockSpec((n_q, MLA_ROPE_DIM), row),
            pl.BlockSpec((None, None, DECODE_KEY_BLOCK, MLA_KV_RANK), cmap),
            pl.BlockSpec((None, None, DECODE_KEY_BLOCK, MLA_ROPE_DIM), cmap),
        ],
        out_specs=pl.BlockSpec((MLA_HEADS, n_q, MLA_KV_RANK), qmap),
        out_shape=jax.ShapeDtypeStruct(qlat.shape, BF16),
        scratch_shapes=[
            pltpu.VMEM((rows, 1), F32),
            pltpu.VMEM((rows, 1), F32),
            pltpu.VMEM((rows, MLA_KV_RANK), F32),
        ],
        compiler_params=_params(("parallel", "arbitrary")),
        name="mla_decode",
    )(qlat, qrope, ckv_new, krope_new, cache_ckv, cache_krope)


def _merge_kernel(x_ref, osb_ref, olat_ref, nm_ref, wg_ref, wuv_ref, wsbo_ref, wmlao_ref, wout_ref, h_ref):
    x = x_ref[...]
    xn = _rms(x, nm_ref[...]).astype(BF16)
    g = _dot(xn, wg_ref[...])
    a = _dot(osb_ref[...], wsbo_ref[...])
    pieces = []
    for pair in range(MLA_HEADS // 2):
        pieces.append(_dot(olat_ref[2 * pair], wuv_ref[2 * pair])
                      + _dot(olat_ref[2 * pair + 1], wuv_ref[2 * pair + 1]))
    o_mla = jnp.concatenate(pieces, axis=-1).astype(BF16)
    b = _dot(o_mla, wmlao_ref[...])
    merged = jax.nn.sigmoid(g[:, :D_MODEL]) * a + jax.nn.sigmoid(g[:, D_MODEL:]) * b
    h_ref[...] = x + _dot(merged.astype(BF16), wout_ref[...])


def _merge_call(x, o_sb, o_lat, mw):
    t = x.shape[0]
    tm = TOKEN_TILE
    row = lambda i: (i, 0)
    return pl.pallas_call(
        _merge_kernel,
        grid=(t // tm,),
        in_specs=[
            pl.BlockSpec((tm, D_MODEL), row),
            pl.BlockSpec((tm, SB_WIDTH), row),
            pl.BlockSpec((MLA_HEADS, tm, MLA_KV_RANK), lambda i: (0, i, 0)),
            _const_spec((1, D_MODEL)),
            _const_spec((D_MODEL, 2 * D_MODEL)),
            _const_spec((MLA_HEADS, MLA_KV_RANK, LANES)),
            _const_spec((SB_WIDTH, D_MODEL)),
            _const_spec((MLA_WIDTH, D_MODEL)),
            _const_spec((D_MODEL, D_MODEL)),
        ],
        out_specs=pl.BlockSpec((tm, D_MODEL), row),
        out_shape=jax.ShapeDtypeStruct((t, D_MODEL), F32),
        compiler_params=_params(("parallel",)),
        name="merge_out",
    )(x, o_sb, o_lat, mw["norm_mix"], mw["w_gates"], mw["w_uv"], mw["w_sb_o"], mw["w_mla_o"], mw["w_out"])


def _dense_ffn_kernel(x_ref, nf_ref, wg_ref, wu_ref, wd_ref, nfin_ref, o_ref, *, n_split, final_norm):
    x = x_ref[...]
    xn = _rms(x, nf_ref[...]).astype(BF16)
    fc = wg_ref.shape[1] // n_split
    acc = x
    for c in range(n_split):
        gt = _dot(xn, wg_ref[:, c * fc:(c + 1) * fc])
        up = _dot(xn, wu_ref[:, c * fc:(c + 1) * fc])
        hid = (gt * jax.nn.sigmoid(gt) * up).astype(BF16)
        acc = acc + _dot(hid, wd_ref[c * fc:(c + 1) * fc, :])
    o_ref[...] = _rms(acc, nfin_ref[...]) if final_norm else acc


def _dense_ffn_call(x, norm_ffn, wg, wu, wd, norm_final, final_norm):
    t = x.shape[0]
    tm = TOKEN_TILE
    d_ff = wg.shape[1]
    n_split = 2 if d_ff % (2 * LANES) == 0 else 1
    row = lambda i: (i, 0)
    return pl.pallas_call(
        functools.partial(_dense_ffn_kernel, n_split=n_split, final_norm=final_norm),
        grid=(t // tm,),
        in_specs=[
            pl.BlockSpec((tm, D_MODEL), row),
            _const_spec((1, D_MODEL)),
            _const_spec((D_MODEL, d_ff)),
            _const_spec((D_MODEL, d_ff)),
            _const_spec((d_ff, D_MODEL)),
            _const_spec((1, D_MODEL)),
        ],
        out_specs=pl.BlockSpec((tm, D_MODEL), row),
        out_shape=jax.ShapeDtypeStruct((t, D_MODEL), F32),
        compiler_params=_params(("parallel",)),
        name="dense_ffn",
    )(x, norm_ffn, wg, wu, wd, norm_final)


def _router_kernel(x_ref, nf_ref, rw_ref, rb_ref, xw_ref, e_ref, g_ref):
    xn = _rms(x_ref[...], nf_ref[...])
    xw_ref[...] = xn

    logits = jnp.dot(xn, rw_ref[...], preferred_element_type=F32, precision=lax.Precision.HIGHEST)
    logits = logits + rb_ref[...]
    lane = lax.broadcasted_iota(jnp.int32, logits.shape, 1).astype(F32)
    lg = jnp.where(lane < N_EXPERTS, logits, -jnp.inf)
    m1 = jnp.max(lg, axis=1, keepdims=True)
    i1 = jnp.min(jnp.where(lg == m1, lane, float(LANES)), axis=1, keepdims=True)
    lg2 = jnp.where(lane == i1, -jnp.inf, lg)
    m2 = jnp.max(lg2, axis=1, keepdims=True)
    i2 = jnp.min(jnp.where(lg2 == m2, lane, float(LANES)), axis=1, keepdims=True)
    e2 = jnp.exp(m2 - m1)
    den = 1.0 + e2
    e_ref[...] = jnp.concatenate([i1, i2], axis=1).astype(jnp.int32)
    g_ref[...] = jnp.concatenate([1.0 / den, e2 / den], axis=1)


def _router_call(x, norm_ffn, rw, rb):
    t = x.shape[0]
    tm = TOKEN_TILE
    row = lambda i: (i, 0)
    return pl.pallas_call(
        _router_kernel,
        grid=(t // tm,),
        in_specs=[
            pl.BlockSpec((tm, D_MODEL), row),
            _const_spec((1, D_MODEL)),
            _const_spec((D_MODEL, LANES)),
            _const_spec((1, LANES)),
        ],
        out_specs=[
            pl.BlockSpec((tm, D_MODEL), row),
            pl.BlockSpec((tm, TOP_K), row),
            pl.BlockSpec((tm, TOP_K), row),
        ],
        out_shape=[
            jax.ShapeDtypeStruct((t, D_MODEL), F32),
            jax.ShapeDtypeStruct((t, TOP_K), jnp.int32),
            jax.ShapeDtypeStruct((t, TOP_K), F32),
        ],
        compiler_params=_params(("parallel",)),
        name="router",
    )(x, norm_ffn, rw, rb)


def _row_copy(src, dst, s, d, sem):
    return pltpu.make_async_copy(src.at[pl.ds(s, 1), :], dst.at[pl.ds(d, 1), :], sem)


def _dispatch_kernel(dest_ref, xw_ref, xs_in_ref, xs_ref, sem):
    del xs_in_ref
    tm = xw_ref.shape[0]

    def start(t, _):
        for k in range(TOP_K):
            _row_copy(xw_ref, xs_ref, t, dest_ref[TOP_K * t + k], sem).start()
        return 0

    def wait(t, _):
        for k in range(TOP_K):
            _row_copy(xw_ref, xs_ref, 0, 0, sem).wait()
        return 0

    lax.fori_loop(0, tm, start, 0, unroll=8)
    lax.fori_loop(0, tm, wait, 0, unroll=8)


def _dispatch_call(dest_flat, xw, xs):
    t = xw.shape[0]
    tm = TOKEN_TILE
    return pl.pallas_call(
        _dispatch_kernel,
        grid=(t // tm,),
        in_specs=[
            pl.BlockSpec((tm * TOP_K,), lambda i: (i,), memory_space=pltpu.SMEM),
            pl.BlockSpec((tm, D_MODEL), lambda i: (i, 0)),
            pl.BlockSpec(memory_space=pl.ANY),
        ],
        out_specs=pl.BlockSpec(memory_space=pl.ANY),
        out_shape=jax.ShapeDtypeStruct(xs.shape, xs.dtype),
        scratch_shapes=[pltpu.SemaphoreType.DMA],
        input_output_aliases={2: 0},
        compiler_params=_params(("arbitrary",)),
        name="moe_dispatch",
    )(dest_flat, xw, xs)


def _expert_kernel(tile_e_ref, n_used_ref, xs_ref, wg_ref, wu_ref, wd_ref, y_ref, x_ref):
    r = pl.program_id(0)
    f = pl.program_id(1)

    @pl.when(f == 0)
    def _():
        x_ref[...] = xs_ref[...].astype(BF16)
        y_ref[...] = jnp.zeros_like(y_ref)

    @pl.when(r < n_used_ref[0])
    def _():
        x = x_ref[...]
        gt = _dot(x, wg_ref[...])
        up = _dot(x, wu_ref[...])
        hid = (gt * jax.nn.sigmoid(gt) * up).astype(BF16)
        y_ref[...] += _dot(hid, wd_ref[...])


def _expert_call(tile_e, n_used, xs, wg, wu, wd):
    n_rows = xs.shape[0]
    rt = EXPERT_ROW_TILE
    nf = EXPERT_F_SPLIT
    d_e = wg.shape[2]
    fc = d_e // nf

    def fidx(r, f):
        return jnp.where(r % 2 == 0, f, nf - 1 - f)

    grid_spec = pltpu.PrefetchScalarGridSpec(
        num_scalar_prefetch=2,
        grid=(n_rows // rt, nf),
        in_specs=[
            pl.BlockSpec((rt, D_MODEL), lambda r, f, te, nu: (r, 0)),
            pl.BlockSpec((None, D_MODEL, fc), lambda r, f, te, nu: (te[r], 0, fidx(r, f))),
            pl.BlockSpec((None, D_MODEL, fc), lambda r, f, te, nu: (te[r], 0, fidx(r, f))),
            pl.BlockSpec((None, fc, D_MODEL), lambda r, f, te, nu: (te[r], fidx(r, f), 0)),
        ],
        out_specs=pl.BlockSpec((rt, D_MODEL), lambda r, f, te, nu: (r, 0)),
        scratch_shapes=[pltpu.VMEM((rt, D_MODEL), BF16)],
    )
    return pl.pallas_call(
        _expert_kernel,
        grid_spec=grid_spec,
        out_shape=jax.ShapeDtypeStruct((n_rows, D_MODEL), F32),
        compiler_params=_params(("arbitrary", "arbitrary")),
        name="moe_experts",
    )(tile_e, n_used, xs, wg, wu, wd)


def _combine_kernel(dest_ref, h_ref, g_ref, y_ref, nfin_ref, o_ref, buf0, buf1, sem, *, final_norm):
    tm = h_ref.shape[0]

    def start(t, _):
        _row_copy(y_ref, buf0, dest_ref[TOP_K * t], t, sem).start()
        _row_copy(y_ref, buf1, dest_ref[TOP_K * t + 1], t, sem).start()
        return 0

    def wait(t, _):
        _row_copy(y_ref, buf0, 0, 0, sem).wait()
        _row_copy(y_ref, buf1, 0, 0, sem).wait()
        return 0

    lax.fori_loop(0, tm, start, 0, unroll=8)
    lax.fori_loop(0, tm, wait, 0, unroll=8)
    g = g_ref[...]
    y = h_ref[...] + (buf0[...] * g[:, 0:1] + buf1[...] * g[:, 1:2])
    o_ref[...] = _rms(y, nfin_ref[...]) if final_norm else y


def _combine_call(dest_flat, h, gate, yb, norm_final, final_norm):
    t = h.shape[0]
    tm = TOKEN_TILE
    row = lambda i: (i, 0)
    return pl.pallas_call(
        functools.partial(_combine_kernel, final_norm=final_norm),
        grid=(t // tm,),
        in_specs=[
            pl.BlockSpec((tm * TOP_K,), lambda i: (i,), memory_space=pltpu.SMEM),
            pl.BlockSpec((tm, D_MODEL), row),
            pl.BlockSpec((tm, TOP_K), row),
            pl.BlockSpec(memory_space=pl.ANY),
            _const_spec((1, D_MODEL)),
        ],
        out_specs=pl.BlockSpec((tm, D_MODEL), row),
        out_shape=jax.ShapeDtypeStruct((t, D_MODEL), F32),
        scratch_shapes=[pltpu.VMEM((tm, D_MODEL), F32), pltpu.VMEM((tm, D_MODEL), F32),
                        pltpu.SemaphoreType.DMA],
        compiler_params=_params(("arbitrary",)),
        name="moe_combine",
    )(dest_flat, h, gate, yb, norm_final)


def _moe_layout(top_e):
    rt = EXPERT_ROW_TILE
    e_flat = top_e.reshape(-1)
    n_assign = e_flat.shape[0]
    onehot = (e_flat[:, None] == jnp.arange(N_EXPERTS, dtype=jnp.int32)[None, :]).astype(jnp.int32)
    incl = jnp.cumsum(onehot, axis=0)
    counts = incl[-1]
    rank = jnp.sum((incl - onehot) * onehot, axis=1)
    padded = (counts + rt - 1) // rt * rt
    pad_end = jnp.cumsum(padded)
    pad_start = pad_end - padded
    dest = (jnp.sum(onehot * pad_start[None, :], axis=1) + rank).astype(jnp.int32)
    n_rows = -(-(n_assign + N_EXPERTS * (rt - 1)) // rt) * rt
    n_tiles = n_rows // rt
    tile_start = jnp.arange(n_tiles, dtype=jnp.int32) * rt
    tile_e = jnp.minimum(
        jnp.sum((pad_end[None, :] <= tile_start[:, None]).astype(jnp.int32), axis=1),
        N_EXPERTS - 1).astype(jnp.int32)
    n_used = (pad_end[-1] // rt).astype(jnp.int32).reshape(1)
    return dest, tile_e, n_used, n_rows


def _moe(streams, norm_ffn, rw, rb, wg, wu, wd, norm_final, final_norm):
    routed = [_router_call(h, norm_ffn, rw, rb) for h in streams]
    top_e = jnp.concatenate([r[1] for r in routed], axis=0)
    dest, tile_e, n_used, n_rows = _moe_layout(top_e)
    xs = jnp.zeros((n_rows, D_MODEL), F32)
    dests = []
    off = 0
    for h, (xw, _, _) in zip(streams, routed):
        n = h.shape[0] * TOP_K
        dests.append(dest[off:off + n])
        off += n
        xs = _dispatch_call(dests[-1], xw, xs)
    yb = _expert_call(tile_e, n_used, xs, wg, wu, wd)
    return [_combine_call(d, h, r[2], yb, norm_final, final_norm)
            for d, h, r in zip(dests, streams, routed)]


def _mixer_weights(l, norm_mix, w_in, mla_q_norm, w_uq, mla_kv_norm, w_uk, w_uv, w_sb_o, w_mla_o, w_out):
    w = w_in[l]
    o_gate = 3 * SB_WIDTH + MLA_Q_RANK + MLA_KV_RANK + MLA_ROPE_DIM
    w1 = jnp.concatenate(
        [w[:, :o_gate], jnp.zeros((D_MODEL, LANES - MLA_ROPE_DIM), F32)], axis=1).astype(BF16)
    uq = w_uq[l].reshape(MLA_Q_RANK, MLA_HEADS, MLA_NOPE_DIM + MLA_ROPE_DIM)
    uq = jnp.concatenate([
        uq[:, :, :MLA_NOPE_DIM].reshape(MLA_Q_RANK, -1),
        uq[:, :, MLA_NOPE_DIM:MLA_NOPE_DIM + ROPE_HALF].reshape(MLA_Q_RANK, -1),
        uq[:, :, MLA_NOPE_DIM + ROPE_HALF:].reshape(MLA_Q_RANK, -1)], axis=1).astype(BF16)
    uv = jnp.transpose(w_uv[l], (1, 0, 2))
    zeros = jnp.zeros_like(uv)
    even = (jnp.arange(MLA_HEADS) % 2 == 0)[:, None, None]
    uv = jnp.concatenate([jnp.where(even, uv, zeros), jnp.where(even, zeros, uv)], axis=2).astype(BF16)
    return {
        "norm_mix": norm_mix[l].reshape(1, D_MODEL),
        "w1": w1,
        "w_gates": w[:, o_gate:].astype(BF16),
        "q_norm": mla_q_norm[l].reshape(1, MLA_Q_RANK),
        "w_uq": uq,
        "kv_norm": mla_kv_norm[l].reshape(1, MLA_KV_RANK),
        "w_ukt": jnp.transpose(w_uk[l], (1, 2, 0)).astype(BF16),
        "w_uv": uv,
        "w_sb_o": w_sb_o[l].astype(BF16),
        "w_mla_o": w_mla_o[l].astype(BF16),
        "w_out": w_out[l].astype(BF16),
    }


def _rope_tables(pos, rows):
    inv_freq = jnp.power(ROPE_BASE, -jnp.arange(ROPE_HALF, dtype=F32) / ROPE_HALF)
    ang = pos.astype(F32)[:, None] * inv_freq[None, :]
    reps = (rows // pos.shape[0], LANES // ROPE_HALF)
    return jnp.tile(jnp.cos(ang), reps), jnp.tile(jnp.sin(ang), reps)


def kernel(x_prompt, x_sample, cache_sb_k, cache_sb_v, cache_mla_ckv, cache_mla_krope, norm_mix, w_in,
           mla_q_norm, w_uq, mla_kv_norm, w_uk, w_uv, w_sb_o, w_mla_o, w_out, norm_ffn, ffn_w_gate,
           ffn_w_up, ffn_w_down, router_w, router_b, moe_w_gate, moe_w_up, moe_w_down, norm_final):
    n_p, seq, _ = x_prompt.shape
    n_s, n_q, _ = x_sample.shape
    depth = w_in.shape[0]
    past = cache_sb_k.shape[2]
    assert seq % TOKEN_TILE == 0 and (n_s * n_q) % TOKEN_TILE == 0 and TOKEN_TILE % n_q == 0
    assert seq % KEY_TILE == 0 and past % DECODE_KEY_BLOCK == 0 and n_q <= LANES

    cos_p, sin_p = _rope_tables(jnp.arange(seq, dtype=jnp.int32), seq)
    cos_s, sin_s = _rope_tables(past + jnp.arange(n_q, dtype=jnp.int32), TOKEN_TILE)
    nfin = norm_final.reshape(1, D_MODEL)

    hp = x_prompt.reshape(n_p * seq, D_MODEL)
    hs = x_sample.reshape(n_s * n_q, D_MODEL)
    rows_p, rows_s = [], []
    for l in range(depth):
        mw = _mixer_weights(l, norm_mix, w_in, mla_q_norm, w_uq, mla_kv_norm, w_uk, w_uv,
                            w_sb_o, w_mla_o, w_out)
        sbq, sbk, sbv, ckv, krope, qlat, qrope = _proj_call(hp, mw, cos_p, sin_p)
        o_sb = _sb_prompt_call(sbq, sbk, sbv, n_p, seq)
        o_lat = _mla_prompt_call(qlat, qrope, ckv, krope, n_p, seq)
        hp = _merge_call(hp, o_sb, o_lat, mw)
        rows_p.append((sbk, sbv, ckv, krope))

        sbq, sbk, sbv, ckv, krope, qlat, qrope = _proj_call(hs, mw, cos_s, sin_s)
        o_sb = _sb_decode_call(sbq, sbk, sbv, cache_sb_k, cache_sb_v, l, n_q)
        o_lat = _mla_decode_call(qlat, qrope, ckv, krope, cache_mla_ckv, cache_mla_krope, l, n_q)
        hs = _merge_call(hs, o_sb, o_lat, mw)
        rows_s.append((sbk, sbv, ckv, krope))

        last = l == depth - 1
        nf = norm_ffn[l].reshape(1, D_MODEL)
        i = l // 2
        if l % 2 == 0:
            wg, wu, wd = ffn_w_gate[i].astype(BF16), ffn_w_up[i].astype(BF16), ffn_w_down[i].astype(BF16)
            hp = _dense_ffn_call(hp, nf, wg, wu, wd, nfin, last)
            hs = _dense_ffn_call(hs, nf, wg, wu, wd, nfin, last)
        else:
            rw = jnp.concatenate([router_w[i], jnp.zeros((D_MODEL, LANES - N_EXPERTS), F32)], axis=1)
            rb = jnp.concatenate([router_b[i], jnp.zeros((LANES - N_EXPERTS,), F32)]).reshape(1, LANES)
            hp, hs = _moe([hp, hs], nf, rw, rb, moe_w_gate[i].astype(BF16), moe_w_up[i].astype(BF16),
                          moe_w_down[i].astype(BF16), nfin, last)

    def stack(rows, idx, shape):
        return jnp.stack([r[idx] for r in rows]).reshape((depth,) + shape)

    kv_p = (n_p, seq, SB_HEADS, SB_HEAD_DIM)
    kv_s = (n_s, n_q, SB_HEADS, SB_HEAD_DIM)
    return (hp.reshape(n_p, seq, D_MODEL), hs.reshape(n_s, n_q, D_MODEL),
            stack(rows_p, 0, kv_p), stack(rows_p, 1, kv_p),
            stack(rows_p, 2, (n_p, seq, MLA_KV_RANK)), stack(rows_p, 3, (n_p, seq, MLA_ROPE_DIM)),
            stack(rows_s, 0, kv_s), stack(rows_s, 1, kv_s),
            stack(rows_s, 2, (n_s, n_q, MLA_KV_RANK)), stack(rows_s, 3, (n_s, n_q, MLA_ROPE_DIM)))
```

```python
import functools

import jax
import jax.numpy as jnp
from jax import lax
from jax.experimental import pallas as pl
from jax.experimental.pallas import tpu as pltpu

F32 = jnp.float32
BF16 = jnp.bfloat16

D_MODEL = 1024
CHUNK = 64
SB_HEADS = 8
SB_HEAD_DIM = 64
SB_WIDTH = SB_HEADS * SB_HEAD_DIM
MLA_HEADS = 8
MLA_NOPE_DIM = 64
MLA_ROPE_DIM = 32
MLA_V_DIM = 64
MLA_Q_RANK = 384
MLA_KV_RANK = 256
MLA_WIDTH = MLA_HEADS * MLA_V_DIM
MLA_SCALE = (MLA_NOPE_DIM + MLA_ROPE_DIM) ** -0.5
SB_SCALE = SB_HEAD_DIM ** -0.5
ROPE_BASE = 10000.0
ROPE_HALF = MLA_ROPE_DIM // 2
N_EXPERTS = 8
TOP_K = 2
RMS_EPS = 1e-6
NEG_INF = -1e30
NEG_LOG2_E = -1.4426950408889634

LANES = 128
MXU_DIM = 256
VMEM_LIMIT_BYTES = 56 * 1024 * 1024

TOKEN_TILE = 256
Q_TILE = 128
KEY_TILE = MXU_DIM
DECODE_KEY_BLOCK = 1024
EXPERT_ROW_TILE = 512
EXPERT_F_SPLIT = 2
IN_PROJ_COLS = 3 * SB_WIDTH + MLA_Q_RANK + MLA_KV_RANK + LANES


def _params(semantics):
    return pltpu.CompilerParams(dimension_semantics=semantics, vmem_limit_bytes=VMEM_LIMIT_BYTES)


def _rms(x, g):
    return x * lax.rsqrt(jnp.mean(x * x, axis=-1, keepdims=True) + RMS_EPS) * g


def _dot(a, b):
    return jnp.dot(a, b, preferred_element_type=F32)


def _dot_nt(a, b):
    return lax.dot_general(a, b, (((1,), (1,)), ((), ())), preferred_element_type=F32)


def _softplus(z):
    return jnp.maximum(z, 0.0) + jnp.log(1.0 + jnp.exp2(jnp.abs(z) * NEG_LOG2_E))


def _later_sum_matrix(n):
    r = lax.broadcasted_iota(jnp.int32, (n, n), 0)
    c = lax.broadcasted_iota(jnp.int32, (n, n), 1)
    return jnp.where(r > c, 1.0, 0.0).astype(BF16)


def _const_spec(shape):
    nd = len(shape)
    return pl.BlockSpec(shape, lambda *_: (0,) * nd)


def _proj_kernel(x_ref, nm_ref, w1_ref, qn_ref, wuq_ref, kvn_ref, wukt_ref, cos_ref, sin_ref,
                 sbq_ref, sbk_ref, sbv_ref, ckv_ref, krope_ref, qlat_ref, qrope_ref):
    xn = _rms(x_ref[...], nm_ref[...]).astype(BF16)
    p = _dot(xn, w1_ref[...])
    o = 0
    sbq_ref[...] = (p[:, o:o + SB_WIDTH] * SB_SCALE).astype(BF16)
    o += SB_WIDTH
    sbk_ref[...] = p[:, o:o + SB_WIDTH]
    o += SB_WIDTH
    sbv_ref[...] = p[:, o:o + SB_WIDTH]
    o += SB_WIDTH
    cq = p[:, o:o + MLA_Q_RANK]
    o += MLA_Q_RANK
    ckv_ref[...] = _rms(p[:, o:o + MLA_KV_RANK], kvn_ref[...])
    o += MLA_KV_RANK
    kr = p[:, o:o + LANES]
    cos = cos_ref[...]
    sin = sin_ref[...]
    c16 = cos[:, :ROPE_HALF]
    s16 = sin[:, :ROPE_HALF]
    k1 = kr[:, :ROPE_HALF]
    k2 = kr[:, ROPE_HALF:MLA_ROPE_DIM]
    krope_ref[...] = jnp.concatenate([k1 * c16 - k2 * s16, k2 * c16 + k1 * s16], axis=-1)

    cqn = _rms(cq, qn_ref[...]).astype(BF16)
    q = _dot(cqn, wuq_ref[...])
    nope_w = MLA_HEADS * MLA_NOPE_DIM
    x1 = q[:, nope_w:nope_w + LANES]
    x2 = q[:, nope_w + LANES:nope_w + 2 * LANES]
    r1 = (x1 * cos - x2 * sin) * MLA_SCALE
    r2 = (x2 * cos + x1 * sin) * MLA_SCALE
    for h in range(MLA_HEADS):
        qn_h = q[:, h * MLA_NOPE_DIM:(h + 1) * MLA_NOPE_DIM].astype(BF16)
        qlat_ref[h] = (_dot(qn_h, wukt_ref[h]) * MLA_SCALE).astype(BF16)
        qrope_ref[h] = jnp.concatenate(
            [r1[:, h * ROPE_HALF:(h + 1) * ROPE_HALF], r2[:, h * ROPE_HALF:(h + 1) * ROPE_HALF]],
            axis=-1).astype(BF16)


def _proj_call(x, mw, cos_tab, sin_tab):
    t = x.shape[0]
    tm = TOKEN_TILE
    n_pos_tiles = cos_tab.shape[0] // tm
    row = lambda i: (i, 0)
    pos = lambda i: (i % n_pos_tiles, 0)
    head_row = lambda i: (0, i, 0)
    return pl.pallas_call(
        _proj_kernel,
        grid=(t // tm,),
        in_specs=[
            pl.BlockSpec((tm, D_MODEL), row),
            _const_spec((1, D_MODEL)),
            _const_spec((D_MODEL, IN_PROJ_COLS)),
            _const_spec((1, MLA_Q_RANK)),
            _const_spec((MLA_Q_RANK, MLA_HEADS * (MLA_NOPE_DIM + MLA_ROPE_DIM))),
            _const_spec((1, MLA_KV_RANK)),
            _const_spec((MLA_HEADS, MLA_NOPE_DIM, MLA_KV_RANK)),
            pl.BlockSpec((tm, LANES), pos),
            pl.BlockSpec((tm, LANES), pos),
        ],
        out_specs=[
            pl.BlockSpec((tm, SB_WIDTH), row),
            pl.BlockSpec((tm, SB_WIDTH), row),
            pl.BlockSpec((tm, SB_WIDTH), row),
            pl.BlockSpec((tm, MLA_KV_RANK), row),
            pl.BlockSpec((tm, MLA_ROPE_DIM), row),
            pl.BlockSpec((MLA_HEADS, tm, MLA_KV_RANK), head_row),
            pl.BlockSpec((MLA_HEADS, tm, MLA_ROPE_DIM), head_row),
        ],
        out_shape=[
            jax.ShapeDtypeStruct((t, SB_WIDTH), BF16),
            jax.ShapeDtypeStruct((t, SB_WIDTH), F32),
            jax.ShapeDtypeStruct((t, SB_WIDTH), F32),
            jax.ShapeDtypeStruct((t, MLA_KV_RANK), F32),
            jax.ShapeDtypeStruct((t, MLA_ROPE_DIM), F32),
            jax.ShapeDtypeStruct((MLA_HEADS, t, MLA_KV_RANK), BF16),
            jax.ShapeDtypeStruct((MLA_HEADS, t, MLA_ROPE_DIM), BF16),
        ],
        compiler_params=_params(("parallel",)),
        name="in_proj",
    )(x, mw["norm_mix"], mw["w1"], mw["q_norm"], mw["w_uq"], mw["kv_norm"], mw["w_ukt"],
      cos_tab, sin_tab)


def _sb_weights_many(zs, seen, tri, runs):
    sps = [_softplus(z) for z in zs]
    ts = sps if seen is None else [jnp.where(seen, sp, 0.0) for sp in sps]
    laters = [_dot(t.astype(BF16), tri) for t in ts]
    ws = []
    for z, sp, later, run in zip(zs, sps, laters, runs):
        w = jnp.exp((z - sp) - (later + run))
        if seen is not None:
            w = jnp.where(seen, w, 0.0)
        ws.append(w.astype(BF16))
    return ws, [jnp.sum(t, axis=1, keepdims=True) for t in ts]


def _sb_weights(z, seen, tri, run):
    ws, tots = _sb_weights_many([z], seen, tri, [run])
    return ws[0], tots[0]


def _sb_prompt_kernel(q_ref, k_ref, v_ref, o_ref, kbf_ref, vbf_ref, qm_ref, acc_ref, run_ref):
    qi = pl.program_id(1)
    n_pairs = SB_HEADS // 2

    @pl.when(qi == 0)
    def _():
        kbf_ref[...] = k_ref[...].astype(BF16)
        vbf_ref[...] = v_ref[...].astype(BF16)

    tri = _later_sum_matrix(KEY_TILE)
    n_kb = (qi * Q_TILE + Q_TILE + KEY_TILE - 1) // KEY_TILE
    lane = lax.broadcasted_iota(jnp.int32, (Q_TILE, LANES), 1)
    first_head = lane < SB_HEAD_DIM
    for pair in range(n_pairs):
        q_pair = q_ref[:, pair * LANES:(pair + 1) * LANES]
        zero = jnp.zeros_like(q_pair)
        qm_ref[2 * pair] = jnp.where(first_head, q_pair, zero)
        qm_ref[2 * pair + 1] = jnp.where(first_head, zero, q_pair)
    acc_ref[...] = jnp.zeros(acc_ref.shape, F32)
    run_ref[...] = jnp.zeros(run_ref.shape, F32)

    def block(start, seen):
        heads = range(SB_HEADS)
        runs = [run_ref[h] for h in heads]
        zs = [_dot_nt(qm_ref[h], kbf_ref[pl.ds(start, KEY_TILE), (h // 2) * LANES:(h // 2 + 1) * LANES])
              for h in heads]
        ws, tots = _sb_weights_many(zs, seen, tri, runs)
        pv = [_dot(ws[h], vbf_ref[pl.ds(start, KEY_TILE), (h // 2) * LANES:(h // 2 + 1) * LANES])
              for h in heads]
        for h in heads:
            run_ref[h] = runs[h] + tots[h]
        for pair in range(n_pairs):
            acc_ref[pair] += jnp.where(first_head, pv[2 * pair], pv[2 * pair + 1])

    last = pl.multiple_of((n_kb - 1) * KEY_TILE, KEY_TILE)
    q_pos = qi * Q_TILE + lax.broadcasted_iota(jnp.int32, (Q_TILE, KEY_TILE), 0)
    k_pos = last + lax.broadcasted_iota(jnp.int32, (Q_TILE, KEY_TILE), 1)
    block(last, k_pos < q_pos)

    def body(j, _):
        block(pl.multiple_of((n_kb - 1 - j) * KEY_TILE, KEY_TILE), None)
        return 0

    lax.fori_loop(1, n_kb, body, 0)
    for pair in range(n_pairs):
        o_ref[:, pair * LANES:(pair + 1) * LANES] = acc_ref[pair].astype(BF16)


def _sb_prompt_call(q, k, v, n_streams, seq):
    nq = seq // Q_TILE
    return pl.pallas_call(
        _sb_prompt_kernel,
        grid=(n_streams, nq),
        in_specs=[
            pl.BlockSpec((Q_TILE, SB_WIDTH), lambda b, i: (b * nq + i, 0)),
            pl.BlockSpec((seq, SB_WIDTH), lambda b, i: (b, 0)),
            pl.BlockSpec((seq, SB_WIDTH), lambda b, i: (b, 0)),
        ],
        out_specs=pl.BlockSpec((Q_TILE, SB_WIDTH), lambda b, i: (b * nq + i, 0)),
        out_shape=jax.ShapeDtypeStruct(q.shape, BF16),
        scratch_shapes=[
            pltpu.VMEM((seq, SB_WIDTH), BF16),
            pltpu.VMEM((seq, SB_WIDTH), BF16),
            pltpu.VMEM((SB_HEADS, Q_TILE, LANES), BF16),
            pltpu.VMEM((SB_HEADS // 2, Q_TILE, LANES), F32),
            pltpu.VMEM((SB_HEADS, Q_TILE, 1), F32),
        ],
        compiler_params=_params(("parallel", "arbitrary")),
        name="sb_prompt",
    )(q, k, v)


def _mla_prompt_kernel(ql_ref, qr_ref, ckv_ref, kr_ref, o_ref, cbf_ref, rbf_ref, m_ref, l_ref, acc_ref):
    qi = pl.program_id(1)

    @pl.when(qi == 0)
    def _():
        cbf_ref[...] = ckv_ref[...].astype(BF16)
        rbf_ref[...] = kr_ref[...].astype(BF16)

    m_ref[...] = jnp.full(m_ref.shape, NEG_INF, F32)
    l_ref[...] = jnp.zeros(l_ref.shape, F32)
    acc_ref[...] = jnp.zeros(acc_ref.shape, F32)
    n_kb = (qi * Q_TILE + Q_TILE + KEY_TILE - 1) // KEY_TILE

    def block(start, visible):
        c = cbf_ref[pl.ds(start, KEY_TILE), :]
        r = rbf_ref[pl.ds(start, KEY_TILE), :]
        heads = range(MLA_HEADS)
        m_old = [m_ref[h] for h in heads]
        l_old = [l_ref[h] for h in heads]
        ss = [_dot_nt(ql_ref[h], c) + _dot_nt(qr_ref[h], r) for h in heads]
        if visible is not None:
            ss = [jnp.where(visible, s, NEG_INF) for s in ss]
        m_new = [jnp.maximum(m_old[h], jnp.max(jnp.maximum(ss[h][:, :LANES], ss[h][:, LANES:]),
                                               axis=1, keepdims=True)) for h in heads]
        p_lo = [jnp.exp(ss[h][:, :LANES] - m_new[h]) for h in heads]
        p_hi = [jnp.exp(ss[h][:, LANES:] - m_new[h]) for h in heads]
        alpha = [jnp.exp(m_old[h] - m_new[h]) for h in heads]
        pv = [_dot(jnp.concatenate([p_lo[h], p_hi[h]], axis=1).astype(BF16), c) for h in heads]
        for h in heads:
            m_ref[h] = m_new[h]
            l_ref[h] = alpha[h][:, :1] * l_old[h] + jnp.sum(p_lo[h] + p_hi[h], axis=1, keepdims=True)
        for h in heads:
            acc_ref[h] = jnp.concatenate([alpha[h], alpha[h]], axis=1) * acc_ref[h] + pv[h]

    def body(j, _):
        block(pl.multiple_of(j * KEY_TILE, KEY_TILE), None)
        return 0

    lax.fori_loop(0, n_kb - 1, body, 0)
    last = pl.multiple_of((n_kb - 1) * KEY_TILE, KEY_TILE)
    q_chunk = (qi * Q_TILE + lax.broadcasted_iota(jnp.int32, (Q_TILE, KEY_TILE), 0)) // CHUNK
    k_chunk = (last + lax.broadcasted_iota(jnp.int32, (Q_TILE, KEY_TILE), 1)) // CHUNK
    block(last, k_chunk <= q_chunk)
    for h in range(MLA_HEADS):
        o_ref[h] = (acc_ref[h] / l_ref[h]).astype(BF16)


def _mla_prompt_call(qlat, qrope, ckv, krope, n_streams, seq):
    nq = seq // Q_TILE
    qmap = lambda b, i: (0, b * nq + i, 0)
    return pl.pallas_call(
        _mla_prompt_kernel,
        grid=(n_streams, nq),
        in_specs=[
            pl.BlockSpec((MLA_HEADS, Q_TILE, MLA_KV_RANK), qmap),
            pl.BlockSpec((MLA_HEADS, Q_TILE, MLA_ROPE_DIM), qmap),
            pl.BlockSpec((seq, MLA_KV_RANK), lambda b, i: (b, 0)),
            pl.BlockSpec((seq, MLA_ROPE_DIM), lambda b, i: (b, 0)),
        ],
        out_specs=pl.BlockSpec((MLA_HEADS, Q_TILE, MLA_KV_RANK), qmap),
        out_shape=jax.ShapeDtypeStruct(qlat.shape, BF16),
        scratch_shapes=[
            pltpu.VMEM((seq, MLA_KV_RANK), BF16),
            pltpu.VMEM((seq, MLA_ROPE_DIM), BF16),
            pltpu.VMEM((MLA_HEADS, Q_TILE, LANES), F32),
            pltpu.VMEM((MLA_HEADS, Q_TILE, 1), F32),
            pltpu.VMEM((MLA_HEADS, Q_TILE, MLA_KV_RANK), F32),
        ],
        compiler_params=_params(("parallel", "arbitrary")),
        name="mla_prompt",
    )(qlat, qrope, ckv, krope)


def _sb_weights_chain(zs, tri, run):
    sps = [_softplus(z) for z in zs]
    laters = [_dot(sp.astype(BF16), tri) for sp in sps]
    ws = []
    for z, sp, later in zip(zs, sps, laters):
        ws.append(jnp.exp((z - sp) - (later + run)).astype(BF16))
        run = run + jnp.sum(sp, axis=1, keepdims=True)
    return ws, run


def _sb_decode_kernel(q_ref, kn_ref, vn_ref, kc_ref, vc_ref, o_ref, acc_ref, run_ref):
    kb = pl.program_id(1)
    n_q = q_ref.shape[0]
    rows = SB_HEADS * n_q
    heads = range(SB_HEADS)
    tri = _later_sum_matrix(KEY_TILE)
    q = q_ref[...]
    qh = [q[:, h * SB_HEAD_DIM:(h + 1) * SB_HEAD_DIM] for h in heads]

    def scores(keys_of):
        return jnp.concatenate([_dot_nt(qh[h], keys_of(h)) for h in heads], axis=0)

    def weighted(w, values_of):
        return jnp.concatenate([_dot(w[h * n_q:(h + 1) * n_q], values_of(h)) for h in heads], axis=0)

    @pl.when(kb == 0)
    def _():
        pad = jnp.zeros((LANES - n_q, SB_HEAD_DIM), BF16)
        kn = kn_ref[...].astype(BF16)
        vn = vn_ref[...].astype(BF16)
        head = lambda a, h: jnp.concatenate([a[:, h * SB_HEAD_DIM:(h + 1) * SB_HEAD_DIM], pad], axis=0)
        z = scores(lambda h: head(kn, h))
        i = lax.broadcasted_iota(jnp.int32, (rows, LANES), 0) % n_q
        j = lax.broadcasted_iota(jnp.int32, (rows, LANES), 1)
        w, tot = _sb_weights(z, j < i, tri[:LANES, :LANES], jnp.zeros((rows, 1), F32))
        acc_ref[...] = weighted(w, lambda h: head(vn, h))
        run_ref[...] = tot

    n_sub = kc_ref.shape[2] // KEY_TILE
    z = jnp.concatenate([_dot(qh[h], kc_ref[h].astype(BF16)) for h in heads], axis=0)
    subs = list(range(n_sub - 1, -1, -1))
    ws, run = _sb_weights_chain([z[:, s * KEY_TILE:(s + 1) * KEY_TILE] for s in subs], tri, run_ref[...])
    w = jnp.concatenate(ws[::-1], axis=1)
    acc_ref[...] += jnp.concatenate(
        [_dot_nt(w[h * n_q:(h + 1) * n_q], vc_ref[h].astype(BF16)) for h in heads], axis=0)
    run_ref[...] = run

    @pl.when(kb == pl.num_programs(1) - 1)
    def _():
        a = acc_ref[...]
        o_ref[...] = jnp.concatenate([a[h * n_q:(h + 1) * n_q] for h in heads], axis=1).astype(BF16)


def _sb_decode_call(q, k_new, v_new, cache_k, cache_v, layer, n_q):
    t = q.shape[0]
    n_streams = t // n_q
    past = cache_k.shape[4]
    n_kb = past // DECODE_KEY_BLOCK
    row = lambda s, j: (s, 0)
    cmap = lambda s, j: (layer, s, 0, 0, n_kb - 1 - j)
    rows = SB_HEADS * n_q
    cache_block = (None, None, SB_HEADS, SB_HEAD_DIM, DECODE_KEY_BLOCK)
    return pl.pallas_call(
        _sb_decode_kernel,
        grid=(n_streams, n_kb),
        in_specs=[
            pl.BlockSpec((n_q, SB_WIDTH), row),
            pl.BlockSpec((n_q, SB_WIDTH), row),
            pl.BlockSpec((n_q, SB_WIDTH), row),
            pl.BlockSpec(cache_block, cmap),
            pl.BlockSpec(cache_block, cmap),
        ],
        out_specs=pl.BlockSpec((n_q, SB_WIDTH), row),
        out_shape=jax.ShapeDtypeStruct(q.shape, BF16),
        scratch_shapes=[pltpu.VMEM((rows, SB_HEAD_DIM), F32), pltpu.VMEM((rows, 1), F32)],
        compiler_params=_params(("parallel", "arbitrary")),
        name="sb_decode",
    )(q, k_new, v_new, cache_k, cache_v)


def _mla_decode_kernel(ql_ref, qr_ref, cn_ref, rn_ref, cc_ref, rc_ref, o_ref, m_ref, l_ref, acc_ref,
                       *, past):
    kb = pl.program_id(1)
    n_q = ql_ref.shape[1]
    rows = MLA_HEADS * n_q
    ql = ql_ref[...].reshape(rows, MLA_KV_RANK)
    qr = qr_ref[...].reshape(rows, MLA_ROPE_DIM)

    def update(s, c):
        m_old = m_ref[...]
        m_new = jnp.maximum(m_old, jnp.max(s, axis=1, keepdims=True))
        p = jnp.exp(s - m_new)
        alpha = jnp.exp(m_old - m_new)
        l_ref[...] = alpha * l_ref[...] + jnp.sum(p, axis=1, keepdims=True)
        acc_ref[...] = alpha * acc_ref[...] + _dot(p.astype(BF16), c)
        m_ref[...] = m_new

    @pl.when(kb == 0)
    def _():
        m_ref[...] = jnp.full((rows, 1), NEG_INF, F32)
        l_ref[...] = jnp.zeros((rows, 1), F32)
        acc_ref[...] = jnp.zeros((rows, MLA_KV_RANK), F32)
        c = jnp.concatenate([cn_ref[...].astype(BF16), jnp.zeros((LANES - n_q, MLA_KV_RANK), BF16)], axis=0)
        r = jnp.concatenate([rn_ref[...].astype(BF16), jnp.zeros((LANES - n_q, MLA_ROPE_DIM), BF16)], axis=0)
        s = _dot_nt(ql, c) + _dot_nt(qr, r)
        i = lax.broadcasted_iota(jnp.int32, (rows, LANES), 0) % n_q
        j = lax.broadcasted_iota(jnp.int32, (rows, LANES), 1)
        vis = jnp.logical_and(j < n_q, (past + j) // CHUNK <= (past + i) // CHUNK)
        update(jnp.where(vis, s, NEG_INF), c)

    c = cc_ref[...].astype(BF16)
    update(_dot_nt(ql, c) + _dot(qr, rc_ref[...].astype(BF16)), c)

    @pl.when(kb == pl.num_programs(1) - 1)
    def _():
        o = acc_ref[...] / l_ref[...]
        o_ref[...] = o.reshape(MLA_HEADS, n_q, MLA_KV_RANK).astype(BF16)


def _mla_decode_call(qlat, qrope, ckv_new, krope_new, cache_ckv, cache_krope, layer, n_q):
    t = qlat.shape[1]
    n_streams = t // n_q
    past = cache_ckv.shape[2]
    n_kb = past // DECODE_KEY_BLOCK
    rows = MLA_HEADS * n_q
    qmap = lambda s, j: (0, s, 0)
    row = lambda s, j: (s, 0)
    cmap = lambda s, j: (layer, s, j, 0)
    rmap = lambda s, j: (layer, s, 0, j)
    return pl.pallas_call(
        functools.partial(_mla_decode_kernel, past=past),
        grid=(n_streams, n_kb),
        in_specs=[
            pl.BlockSpec((MLA_HEADS, n_q, MLA_KV_RANK), qmap),
            pl.BlockSpec((MLA_HEADS, n_q, MLA_ROPE_DIM), qmap),
            pl.BlockSpec((n_q, MLA_KV_RANK), row),
            pl.BlockSpec((n_q, MLA_ROPE_DIM), row),
            pl.BlockSpec((None, None, DECODE_KEY_BLOCK, MLA_KV_RANK), cmap),
            pl.BlockSpec((None, None, MLA_ROPE_DIM, DECODE_KEY_BLOCK), rmap),
        ],
        out_specs=pl.BlockSpec((MLA_HEADS, n_q, MLA_KV_RANK), qmap),
        out_shape=jax.ShapeDtypeStruct(qlat.shape, BF16),
        scratch_shapes=[
            pltpu.VMEM((rows, 1), F32),
            pltpu.VMEM((rows, 1), F32),
            pltpu.VMEM((rows, MLA_KV_RANK), F32),
        ],
        compiler_params=_params(("parallel", "arbitrary")),
        name="mla_decode",
    )(qlat, qrope, ckv_new, krope_new, cache_ckv, cache_krope)


def _merge_kernel(x_ref, osb_ref, olat_ref, nm_ref, wg_ref, wuv_ref, wsbo_ref, wmlao_ref, wout_ref, h_ref):
    x = x_ref[...]
    xn = _rms(x, nm_ref[...]).astype(BF16)
    g = _dot(xn, wg_ref[...])
    a = _dot(osb_ref[...], wsbo_ref[...])
    pieces = []
    for pair in range(MLA_HEADS // 2):
        pieces.append(_dot(olat_ref[2 * pair], wuv_ref[2 * pair])
                      + _dot(olat_ref[2 * pair + 1], wuv_ref[2 * pair + 1]))
    o_mla = jnp.concatenate(pieces, axis=-1).astype(BF16)
    b = _dot(o_mla, wmlao_ref[...])
    merged = jax.nn.sigmoid(g[:, :D_MODEL]) * a + jax.nn.sigmoid(g[:, D_MODEL:]) * b
    h_ref[...] = x + _dot(merged.astype(BF16), wout_ref[...])


def _merge_call(x, o_sb, o_lat, mw):
    t = x.shape[0]
    tm = TOKEN_TILE
    row = lambda i: (i, 0)
    return pl.pallas_call(
        _merge_kernel,
        grid=(t // tm,),
        in_specs=[
            pl.BlockSpec((tm, D_MODEL), row),
            pl.BlockSpec((tm, SB_WIDTH), row),
            pl.BlockSpec((MLA_HEADS, tm, MLA_KV_RANK), lambda i: (0, i, 0)),
            _const_spec((1, D_MODEL)),
            _const_spec((D_MODEL, 2 * D_MODEL)),
            _const_spec((MLA_HEADS, MLA_KV_RANK, LANES)),
            _const_spec((SB_WIDTH, D_MODEL)),
            _const_spec((MLA_WIDTH, D_MODEL)),
            _const_spec((D_MODEL, D_MODEL)),
        ],
        out_specs=pl.BlockSpec((tm, D_MODEL), row),
        out_shape=jax.ShapeDtypeStruct((t, D_MODEL), F32),
        compiler_params=_params(("parallel",)),
        name="merge_out",
    )(x, o_sb, o_lat, mw["norm_mix"], mw["w_gates"], mw["w_uv"], mw["w_sb_o"], mw["w_mla_o"], mw["w_out"])


def _dense_ffn_kernel(x_ref, nf_ref, wg_ref, wu_ref, wd_ref, nfin_ref, o_ref, *, n_split, final_norm):
    x = x_ref[...]
    xn = _rms(x, nf_ref[...]).astype(BF16)
    fc = wg_ref.shape[1] // n_split
    acc = x
    for c in range(n_split):
        gt = _dot(xn, wg_ref[:, c * fc:(c + 1) * fc])
        up = _dot(xn, wu_ref[:, c * fc:(c + 1) * fc])
        hid = (gt * jax.nn.sigmoid(gt) * up).astype(BF16)
        acc = acc + _dot(hid, wd_ref[c * fc:(c + 1) * fc, :])
    o_ref[...] = _rms(acc, nfin_ref[...]) if final_norm else acc


def _dense_ffn_call(x, norm_ffn, wg, wu, wd, norm_final, final_norm):
    t = x.shape[0]
    tm = TOKEN_TILE
    d_ff = wg.shape[1]
    n_split = 2 if d_ff % (2 * LANES) == 0 else 1
    row = lambda i: (i, 0)
    return pl.pallas_call(
        functools.partial(_dense_ffn_kernel, n_split=n_split, final_norm=final_norm),
        grid=(t // tm,),
        in_specs=[
            pl.BlockSpec((tm, D_MODEL), row),
            _const_spec((1, D_MODEL)),
            _const_spec((D_MODEL, d_ff)),
            _const_spec((D_MODEL, d_ff)),
            _const_spec((d_ff, D_MODEL)),
            _const_spec((1, D_MODEL)),
        ],
        out_specs=pl.BlockSpec((tm, D_MODEL), row),
        out_shape=jax.ShapeDtypeStruct((t, D_MODEL), F32),
        compiler_params=_params(("parallel",)),
        name="dense_ffn",
    )(x, norm_ffn, wg, wu, wd, norm_final)


def _router_kernel(x_ref, nf_ref, rw_ref, rb_ref, xw_ref, e_ref, g_ref):
    xn = _rms(x_ref[...], nf_ref[...])
    xw_ref[...] = xn

    logits = jnp.dot(xn, rw_ref[...], preferred_element_type=F32, precision=lax.Precision.HIGHEST)
    logits = logits + rb_ref[...]
    lane = lax.broadcasted_iota(jnp.int32, logits.shape, 1).astype(F32)
    lg = jnp.where(lane < N_EXPERTS, logits, -jnp.inf)
    m1 = jnp.max(lg, axis=1, keepdims=True)
    i1 = jnp.min(jnp.where(lg == m1, lane, float(LANES)), axis=1, keepdims=True)
    lg2 = jnp.where(lane == i1, -jnp.inf, lg)
    m2 = jnp.max(lg2, axis=1, keepdims=True)
    i2 = jnp.min(jnp.where(lg2 == m2, lane, float(LANES)), axis=1, keepdims=True)
    e2 = jnp.exp(m2 - m1)
    den = 1.0 + e2
    e_ref[...] = jnp.concatenate([i1, i2], axis=1).astype(jnp.int32)
    g_ref[...] = jnp.concatenate([1.0 / den, e2 / den], axis=1)


def _router_call(x, norm_ffn, rw, rb):
    t = x.shape[0]
    tm = TOKEN_TILE
    row = lambda i: (i, 0)
    return pl.pallas_call(
        _router_kernel,
        grid=(t // tm,),
        in_specs=[
            pl.BlockSpec((tm, D_MODEL), row),
            _const_spec((1, D_MODEL)),
            _const_spec((D_MODEL, LANES)),
            _const_spec((1, LANES)),
        ],
        out_specs=[
            pl.BlockSpec((tm, D_MODEL), row),
            pl.BlockSpec((tm, TOP_K), row),
            pl.BlockSpec((tm, TOP_K), row),
        ],
        out_shape=[
            jax.ShapeDtypeStruct((t, D_MODEL), F32),
            jax.ShapeDtypeStruct((t, TOP_K), jnp.int32),
            jax.ShapeDtypeStruct((t, TOP_K), F32),
        ],
        compiler_params=_params(("parallel",)),
        name="router",
    )(x, norm_ffn, rw, rb)


def _row_copy(src, dst, s, d, sem):
    return pltpu.make_async_copy(src.at[pl.ds(s, 1), :], dst.at[pl.ds(d, 1), :], sem)


def _dispatch_kernel(dest_ref, xw_ref, xs_in_ref, xs_ref, sem):
    del xs_in_ref
    tm = xw_ref.shape[0]

    def start(t, _):
        for k in range(TOP_K):
            _row_copy(xw_ref, xs_ref, t, dest_ref[TOP_K * t + k], sem).start()
        return 0

    def wait(t, _):
        for k in range(TOP_K):
            _row_copy(xw_ref, xs_ref, 0, 0, sem).wait()
        return 0

    lax.fori_loop(0, tm, start, 0, unroll=8)
    lax.fori_loop(0, tm, wait, 0, unroll=8)


def _dispatch_call(dest_flat, xw, xs):
    t = xw.shape[0]
    tm = TOKEN_TILE
    return pl.pallas_call(
        _dispatch_kernel,
        grid=(t // tm,),
        in_specs=[
            pl.BlockSpec((tm * TOP_K,), lambda i: (i,), memory_space=pltpu.SMEM),
            pl.BlockSpec((tm, D_MODEL), lambda i: (i, 0)),
            pl.BlockSpec(memory_space=pl.ANY),
        ],
        out_specs=pl.BlockSpec(memory_space=pl.ANY),
        out_shape=jax.ShapeDtypeStruct(xs.shape, xs.dtype),
        scratch_shapes=[pltpu.SemaphoreType.DMA],
        input_output_aliases={2: 0},
        compiler_params=_params(("arbitrary",)),
        name="moe_dispatch",
    )(dest_flat, xw, xs)


def _expert_kernel(tile_e_ref, n_used_ref, xs_ref, wg_ref, wu_ref, wd_ref, y_ref, x_ref):
    r = pl.program_id(0)
    f = pl.program_id(1)

    @pl.when(f == 0)
    def _():
        x_ref[...] = xs_ref[...].astype(BF16)
        y_ref[...] = jnp.zeros_like(y_ref)

    @pl.when(r < n_used_ref[0])
    def _():
        x = x_ref[...]
        gt = _dot(x, wg_ref[...])
        up = _dot(x, wu_ref[...])
        hid = (gt * jax.nn.sigmoid(gt) * up).astype(BF16)
        y_ref[...] += _dot(hid, wd_ref[...])


def _expert_call(tile_e, n_used, xs, wg, wu, wd):
    n_rows = xs.shape[0]
    rt = EXPERT_ROW_TILE
    nf = EXPERT_F_SPLIT
    d_e = wg.shape[2]
    fc = d_e // nf

    def fidx(r, f):
        return jnp.where(r % 2 == 0, f, nf - 1 - f)

    grid_spec = pltpu.PrefetchScalarGridSpec(
        num_scalar_prefetch=2,
        grid=(n_rows // rt, nf),
        in_specs=[
            pl.BlockSpec((rt, D_MODEL), lambda r, f, te, nu: (r, 0)),
            pl.BlockSpec((None, D_MODEL, fc), lambda r, f, te, nu: (te[r], 0, fidx(r, f))),
            pl.BlockSpec((None, D_MODEL, fc), lambda r, f, te, nu: (te[r], 0, fidx(r, f))),
            pl.BlockSpec((None, fc, D_MODEL), lambda r, f, te, nu: (te[r], fidx(r, f), 0)),
        ],
        out_specs=pl.BlockSpec((rt, D_MODEL), lambda r, f, te, nu: (r, 0)),
        scratch_shapes=[pltpu.VMEM((rt, D_MODEL), BF16)],
    )
    return pl.pallas_call(
        _expert_kernel,
        grid_spec=grid_spec,
        out_shape=jax.ShapeDtypeStruct((n_rows, D_MODEL), F32),
        compiler_params=_params(("arbitrary", "arbitrary")),
        name="moe_experts",
    )(tile_e, n_used, xs, wg, wu, wd)


def _combine_kernel(dest_ref, h_ref, g_ref, y_ref, nfin_ref, o_ref, buf0, buf1, sem, *, final_norm):
    tm = h_ref.shape[0]

    def start(t, _):
        _row_copy(y_ref, buf0, dest_ref[TOP_K * t], t, sem).start()
        _row_copy(y_ref, buf1, dest_ref[TOP_K * t + 1], t, sem).start()
        return 0

    def wait(t, _):
        _row_copy(y_ref, buf0, 0, 0, sem).wait()
        _row_copy(y_ref, buf1, 0, 0, sem).wait()
        return 0

    lax.fori_loop(0, tm, start, 0, unroll=8)
    lax.fori_loop(0, tm, wait, 0, unroll=8)
    g = g_ref[...]
    y = h_ref[...] + (buf0[...] * g[:, 0:1] + buf1[...] * g[:, 1:2])
    o_ref[...] = _rms(y, nfin_ref[...]) if final_norm else y


def _combine_call(dest_flat, h, gate, yb, norm_final, final_norm):
    t = h.shape[0]
    tm = TOKEN_TILE
    row = lambda i: (i, 0)
    return pl.pallas_call(
        functools.partial(_combine_kernel, final_norm=final_norm),
        grid=(t // tm,),
        in_specs=[
            pl.BlockSpec((tm * TOP_K,), lambda i: (i,), memory_space=pltpu.SMEM),
            pl.BlockSpec((tm, D_MODEL), row),
            pl.BlockSpec((tm, TOP_K), row),
            pl.BlockSpec(memory_space=pl.ANY),
            _const_spec((1, D_MODEL)),
        ],
        out_specs=pl.BlockSpec((tm, D_MODEL), row),
        out_shape=jax.ShapeDtypeStruct((t, D_MODEL), F32),
        scratch_shapes=[pltpu.VMEM((tm, D_MODEL), F32), pltpu.VMEM((tm, D_MODEL), F32),
                        pltpu.SemaphoreType.DMA],
        compiler_params=_params(("arbitrary",)),
        name="moe_combine",
    )(dest_flat, h, gate, yb, norm_final)


def _moe_layout(top_e):
    rt = EXPERT_ROW_TILE
    e_flat = top_e.reshape(-1)
    n_assign = e_flat.shape[0]
    onehot = (e_flat[:, None] == jnp.arange(N_EXPERTS, dtype=jnp.int32)[None, :]).astype(jnp.int32)
    incl = jnp.cumsum(onehot, axis=0)
    counts = incl[-1]
    rank = jnp.sum((incl - onehot) * onehot, axis=1)
    padded = (counts + rt - 1) // rt * rt
    pad_end = jnp.cumsum(padded)
    pad_start = pad_end - padded
    dest = (jnp.sum(onehot * pad_start[None, :], axis=1) + rank).astype(jnp.int32)
    n_rows = -(-(n_assign + N_EXPERTS * (rt - 1)) // rt) * rt
    n_tiles = n_rows // rt
    tile_start = jnp.arange(n_tiles, dtype=jnp.int32) * rt
    tile_e = jnp.minimum(
        jnp.sum((pad_end[None, :] <= tile_start[:, None]).astype(jnp.int32), axis=1),
        N_EXPERTS - 1).astype(jnp.int32)
    n_used = (pad_end[-1] // rt).astype(jnp.int32).reshape(1)
    return dest, tile_e, n_used, n_rows


def _moe(streams, norm_ffn, rw, rb, wg, wu, wd, norm_final, final_norm):
    routed = [_router_call(h, norm_ffn, rw, rb) for h in streams]
    top_e = jnp.concatenate([r[1] for r in routed], axis=0)
    dest, tile_e, n_used, n_rows = _moe_layout(top_e)
    xs = jnp.zeros((n_rows, D_MODEL), F32)
    dests = []
    off = 0
    for h, (xw, _, _) in zip(streams, routed):
        n = h.shape[0] * TOP_K
        dests.append(dest[off:off + n])
        off += n
        xs = _dispatch_call(dests[-1], xw, xs)
    yb = _expert_call(tile_e, n_used, xs, wg, wu, wd)
    return [_combine_call(d, h, r[2], yb, norm_final, final_norm)
            for d, h, r in zip(dests, streams, routed)]


def _mixer_weights(l, norm_mix, w_in, mla_q_norm, w_uq, mla_kv_norm, w_uk, w_uv, w_sb_o, w_mla_o, w_out):
    w = w_in[l]
    o_gate = 3 * SB_WIDTH + MLA_Q_RANK + MLA_KV_RANK + MLA_ROPE_DIM
    w1 = jnp.concatenate(
        [w[:, :o_gate], jnp.zeros((D_MODEL, LANES - MLA_ROPE_DIM), F32)], axis=1).astype(BF16)
    uq = w_uq[l].reshape(MLA_Q_RANK, MLA_HEADS, MLA_NOPE_DIM + MLA_ROPE_DIM)
    uq = jnp.concatenate([
        uq[:, :, :MLA_NOPE_DIM].reshape(MLA_Q_RANK, -1),
        uq[:, :, MLA_NOPE_DIM:MLA_NOPE_DIM + ROPE_HALF].reshape(MLA_Q_RANK, -1),
        uq[:, :, MLA_NOPE_DIM + ROPE_HALF:].reshape(MLA_Q_RANK, -1)], axis=1).astype(BF16)
    uv = jnp.transpose(w_uv[l], (1, 0, 2))
    zeros = jnp.zeros_like(uv)
    even = (jnp.arange(MLA_HEADS) % 2 == 0)[:, None, None]
    uv = jnp.concatenate([jnp.where(even, uv, zeros), jnp.where(even, zeros, uv)], axis=2).astype(BF16)
    return {
        "norm_mix": norm_mix[l].reshape(1, D_MODEL),
        "w1": w1,
        "w_gates": w[:, o_gate:].astype(BF16),
        "q_norm": mla_q_norm[l].reshape(1, MLA_Q_RANK),
        "w_uq": uq,
        "kv_norm": mla_kv_norm[l].reshape(1, MLA_KV_RANK),
        "w_ukt": jnp.transpose(w_uk[l], (1, 2, 0)).astype(BF16),
        "w_uv": uv,
        "w_sb_o": w_sb_o[l].astype(BF16),
        "w_mla_o": w_mla_o[l].astype(BF16),
        "w_out": w_out[l].astype(BF16),
    }


def _rope_tables(pos, rows):
    inv_freq = jnp.power(ROPE_BASE, -jnp.arange(ROPE_HALF, dtype=F32) / ROPE_HALF)
    ang = pos.astype(F32)[:, None] * inv_freq[None, :]
    reps = (rows // pos.shape[0], LANES // ROPE_HALF)
    return jnp.tile(jnp.cos(ang), reps), jnp.tile(jnp.sin(ang), reps)


def kernel(x_prompt, x_sample, cache_sb_k, cache_sb_v, cache_mla_ckv, cache_mla_krope, norm_mix, w_in,
           mla_q_norm, w_uq, mla_kv_norm, w_uk, w_uv, w_sb_o, w_mla_o, w_out, norm_ffn, ffn_w_gate,
           ffn_w_up, ffn_w_down, router_w, router_b, moe_w_gate, moe_w_up, moe_w_down, norm_final):
    n_p, seq, _ = x_prompt.shape
    n_s, n_q, _ = x_sample.shape
    depth = w_in.shape[0]
    past = cache_sb_k.shape[2]
    assert seq % TOKEN_TILE == 0 and (n_s * n_q) % TOKEN_TILE == 0 and TOKEN_TILE % n_q == 0
    assert seq % KEY_TILE == 0 and past % DECODE_KEY_BLOCK == 0 and n_q <= LANES

    cos_p, sin_p = _rope_tables(jnp.arange(seq, dtype=jnp.int32), seq)
    cos_s, sin_s = _rope_tables(past + jnp.arange(n_q, dtype=jnp.int32), TOKEN_TILE)
    nfin = norm_final.reshape(1, D_MODEL)
    cache_kt = jnp.transpose(cache_sb_k, (0, 1, 3, 4, 2))
    cache_vt = jnp.transpose(cache_sb_v, (0, 1, 3, 4, 2))
    cache_rt = jnp.transpose(cache_mla_krope, (0, 1, 3, 2))

    hp = x_prompt.reshape(n_p * seq, D_MODEL)
    hs = x_sample.reshape(n_s * n_q, D_MODEL)
    rows_p, rows_s = [], []
    for l in range(depth):
        mw = _mixer_weights(l, norm_mix, w_in, mla_q_norm, w_uq, mla_kv_norm, w_uk, w_uv,
                            w_sb_o, w_mla_o, w_out)
        sbq, sbk, sbv, ckv, krope, qlat, qrope = _proj_call(hp, mw, cos_p, sin_p)
        o_sb = _sb_prompt_call(sbq, sbk, sbv, n_p, seq)
        o_lat = _mla_prompt_call(qlat, qrope, ckv, krope, n_p, seq)
        hp = _merge_call(hp, o_sb, o_lat, mw)
        rows_p.append((sbk, sbv, ckv, krope))

        sbq, sbk, sbv, ckv, krope, qlat, qrope = _proj_call(hs, mw, cos_s, sin_s)
        o_sb = _sb_decode_call(sbq, sbk, sbv, cache_kt, cache_vt, l, n_q)
        o_lat = _mla_decode_call(qlat, qrope, ckv, krope, cache_mla_ckv, cache_rt, l, n_q)
        hs = _merge_call(hs, o_sb, o_lat, mw)
        rows_s.append((sbk, sbv, ckv, krope))

        last = l == depth - 1
        nf = norm_ffn[l].reshape(1, D_MODEL)
        i = l // 2
        if l % 2 == 0:
            wg, wu, wd = ffn_w_gate[i].astype(BF16), ffn_w_up[i].astype(BF16), ffn_w_down[i].astype(BF16)
            hp = _dense_ffn_call(hp, nf, wg, wu, wd, nfin, last)
            hs = _dense_ffn_call(hs, nf, wg, wu, wd, nfin, last)
        else:
            rw = jnp.concatenate([router_w[i], jnp.zeros((D_MODEL, LANES - N_EXPERTS), F32)], axis=1)
            rb = jnp.concatenate([router_b[i], jnp.zeros((LANES - N_EXPERTS,), F32)]).reshape(1, LANES)
            hp, hs = _moe([hp, hs], nf, rw, rb, moe_w_gate[i].astype(BF16), moe_w_up[i].astype(BF16),
                          moe_w_down[i].astype(BF16), nfin, last)

    def stack(rows, idx, shape):
        return jnp.stack([r[idx] for r in rows]).reshape((depth,) + shape)

    kv_p = (n_p, seq, SB_HEADS, SB_HEAD_DIM)
    kv_s = (n_s, n_q, SB_HEADS, SB_HEAD_DIM)
    return (hp.reshape(n_p, seq, D_MODEL), hs.reshape(n_s, n_q, D_MODEL),
            stack(rows_p, 0, kv_p), stack(rows_p, 1, kv_p),
            stack(rows_p, 2, (n_p, seq, MLA_KV_RANK)), stack(rows_p, 3, (n_p, seq, MLA_ROPE_DIM)),
            stack(rows_s, 0, kv_s), stack(rows_s, 1, kv_s),
            stack(rows_s, 2, (n_s, n_q, MLA_KV_RANK)), stack(rows_s, 3, (n_s, n_q, MLA_ROPE_DIM)))
```

```python
import functools

import jax
import jax.numpy as jnp
from jax import lax
from jax.experimental import pallas as pl
from jax.experimental.pallas import tpu as pltpu

F32 = jnp.float32
BF16 = jnp.bfloat16

D_MODEL = 1024
CHUNK = 64
SB_HEADS = 8
SB_HEAD_DIM = 64
SB_WIDTH = SB_HEADS * SB_HEAD_DIM
MLA_HEADS = 8
MLA_NOPE_DIM = 64
MLA_ROPE_DIM = 32
MLA_V_DIM = 64
MLA_Q_RANK = 384
MLA_KV_RANK = 256
MLA_WIDTH = MLA_HEADS * MLA_V_DIM
MLA_SCALE = (MLA_NOPE_DIM + MLA_ROPE_DIM) ** -0.5
SB_SCALE = SB_HEAD_DIM ** -0.5
ROPE_BASE = 10000.0
ROPE_HALF = MLA_ROPE_DIM // 2
N_EXPERTS = 8
TOP_K = 2
RMS_EPS = 1e-6
NEG_INF = -1e30
NEG_LOG2_E = -1.4426950408889634

LANES = 128
MXU_DIM = 256
VMEM_LIMIT_BYTES = 56 * 1024 * 1024

TOKEN_TILE = 256
Q_TILE = 128
KEY_TILE = MXU_DIM
DECODE_KEY_BLOCK = 2048
EXPERT_ROW_TILE = 512
EXPERT_F_SPLIT = 2
IN_PROJ_COLS = 3 * SB_WIDTH + MLA_Q_RANK + MLA_KV_RANK + LANES


def _params(semantics):
    return pltpu.CompilerParams(dimension_semantics=semantics, vmem_limit_bytes=VMEM_LIMIT_BYTES)


def _rms(x, g):
    return x * lax.rsqrt(jnp.mean(x * x, axis=-1, keepdims=True) + RMS_EPS) * g


def _dot(a, b):
    return jnp.dot(a, b, preferred_element_type=F32)


def _dot_nt(a, b):
    return lax.dot_general(a, b, (((1,), (1,)), ((), ())), preferred_element_type=F32)


def _softplus(z):
    return jnp.maximum(z, 0.0) + jnp.log(1.0 + jnp.exp2(jnp.abs(z) * NEG_LOG2_E))


def _later_sum_matrix(n):
    r = lax.broadcasted_iota(jnp.int32, (n, n), 0)
    c = lax.broadcasted_iota(jnp.int32, (n, n), 1)
    return jnp.where(r > c, 1.0, 0.0).astype(BF16)


def _const_spec(shape):
    nd = len(shape)
    return pl.BlockSpec(shape, lambda *_: (0,) * nd)


def _proj_kernel(x_ref, nm_ref, w1_ref, qn_ref, wuq_ref, kvn_ref, cos_ref, sin_ref,
                 sbq_ref, sbk_ref, sbv_ref, ckv_ref, krope_ref, qcat_ref):
    xn = _rms(x_ref[...], nm_ref[...]).astype(BF16)
    p = _dot(xn, w1_ref[...])
    o = 0
    sbq_ref[...] = (p[:, o:o + SB_WIDTH] * SB_SCALE).astype(BF16)
    o += SB_WIDTH
    sbk_ref[...] = p[:, o:o + SB_WIDTH]
    o += SB_WIDTH
    sbv_ref[...] = p[:, o:o + SB_WIDTH]
    o += SB_WIDTH
    cq = p[:, o:o + MLA_Q_RANK]
    o += MLA_Q_RANK
    ckv_ref[...] = _rms(p[:, o:o + MLA_KV_RANK], kvn_ref[...])
    o += MLA_KV_RANK
    kr = p[:, o:o + LANES]
    cos = cos_ref[...]
    sin = sin_ref[...]
    c16 = cos[:, :ROPE_HALF]
    s16 = sin[:, :ROPE_HALF]
    k1 = kr[:, :ROPE_HALF]
    k2 = kr[:, ROPE_HALF:MLA_ROPE_DIM]
    krope_ref[...] = jnp.concatenate([k1 * c16 - k2 * s16, k2 * c16 + k1 * s16], axis=-1)

    cqn = _rms(cq, qn_ref[...]).astype(BF16)
    q = _dot(cqn, wuq_ref[...])
    nope_w = MLA_HEADS * MLA_NOPE_DIM
    x1 = q[:, nope_w:nope_w + LANES]
    x2 = q[:, nope_w + LANES:nope_w + 2 * LANES]
    r1 = (x1 * cos - x2 * sin) * MLA_SCALE
    r2 = (x2 * cos + x1 * sin) * MLA_SCALE
    fill = jnp.zeros((q.shape[0], LANES - MLA_NOPE_DIM - MLA_ROPE_DIM), F32)
    for h in range(MLA_HEADS):
        qcat_ref[h] = jnp.concatenate(
            [q[:, h * MLA_NOPE_DIM:(h + 1) * MLA_NOPE_DIM] * MLA_SCALE,
             r1[:, h * ROPE_HALF:(h + 1) * ROPE_HALF], r2[:, h * ROPE_HALF:(h + 1) * ROPE_HALF], fill],
            axis=-1).astype(BF16)


def _proj_call(x, mw, cos_tab, sin_tab):
    t = x.shape[0]
    tm = TOKEN_TILE
    n_pos_tiles = cos_tab.shape[0] // tm
    row = lambda i: (i, 0)
    pos = lambda i: (i % n_pos_tiles, 0)
    head_row = lambda i: (0, i, 0)
    return pl.pallas_call(
        _proj_kernel,
        grid=(t // tm,),
        in_specs=[
            pl.BlockSpec((tm, D_MODEL), row),
            _const_spec((1, D_MODEL)),
            _const_spec((D_MODEL, IN_PROJ_COLS)),
            _const_spec((1, MLA_Q_RANK)),
            _const_spec((MLA_Q_RANK, MLA_HEADS * (MLA_NOPE_DIM + MLA_ROPE_DIM))),
            _const_spec((1, MLA_KV_RANK)),
            pl.BlockSpec((tm, LANES), pos),
            pl.BlockSpec((tm, LANES), pos),
        ],
        out_specs=[
            pl.BlockSpec((tm, SB_WIDTH), row),
            pl.BlockSpec((tm, SB_WIDTH), row),
            pl.BlockSpec((tm, SB_WIDTH), row),
            pl.BlockSpec((tm, MLA_KV_RANK), row),
            pl.BlockSpec((tm, MLA_ROPE_DIM), row),
            pl.BlockSpec((MLA_HEADS, tm, LANES), head_row),
        ],
        out_shape=[
            jax.ShapeDtypeStruct((t, SB_WIDTH), BF16),
            jax.ShapeDtypeStruct((t, SB_WIDTH), F32),
            jax.ShapeDtypeStruct((t, SB_WIDTH), F32),
            jax.ShapeDtypeStruct((t, MLA_KV_RANK), F32),
            jax.ShapeDtypeStruct((t, MLA_ROPE_DIM), F32),
            jax.ShapeDtypeStruct((MLA_HEADS, t, LANES), BF16),
        ],
        compiler_params=_params(("parallel",)),
        name="in_proj",
    )(x, mw["norm_mix"], mw["w1"], mw["q_norm"], mw["w_uq"], mw["kv_norm"], cos_tab, sin_tab)


def _sb_weights_many(zs, seen, tri, runs):
    sps = [_softplus(z) for z in zs]
    ts = sps if seen is None else [jnp.where(seen, sp, 0.0) for sp in sps]
    laters = [_dot(t.astype(BF16), tri) for t in ts]
    ws = []
    for z, sp, later, run in zip(zs, sps, laters, runs):
        w = jnp.exp((z - sp) - (later + run))
        if seen is not None:
            w = jnp.where(seen, w, 0.0)
        ws.append(w.astype(BF16))
    return ws, [jnp.sum(t, axis=1, keepdims=True) for t in ts]


def _sb_weights(z, seen, tri, run):
    ws, tots = _sb_weights_many([z], seen, tri, [run])
    return ws[0], tots[0]


def _sb_prompt_kernel(q_ref, k_ref, v_ref, o_ref, kbf_ref, vbf_ref, qm_ref, acc_ref, run_ref):
    qi = pl.program_id(1)
    n_pairs = SB_HEADS // 2

    @pl.when(qi == 0)
    def _():
        kbf_ref[...] = k_ref[...].astype(BF16)
        vbf_ref[...] = v_ref[...].astype(BF16)

    tri = _later_sum_matrix(KEY_TILE)
    n_kb = (qi * Q_TILE + Q_TILE + KEY_TILE - 1) // KEY_TILE
    lane = lax.broadcasted_iota(jnp.int32, (Q_TILE, LANES), 1)
    first_head = lane < SB_HEAD_DIM
    for pair in range(n_pairs):
        q_pair = q_ref[:, pair * LANES:(pair + 1) * LANES]
        zero = jnp.zeros_like(q_pair)
        qm_ref[2 * pair] = jnp.where(first_head, q_pair, zero)
        qm_ref[2 * pair + 1] = jnp.where(first_head, zero, q_pair)
    acc_ref[...] = jnp.zeros(acc_ref.shape, F32)
    run_ref[...] = jnp.zeros(run_ref.shape, F32)

    def block(start, seen):
        heads = range(SB_HEADS)
        runs = [run_ref[h] for h in heads]
        zs = [_dot_nt(qm_ref[h], kbf_ref[pl.ds(start, KEY_TILE), (h // 2) * LANES:(h // 2 + 1) * LANES])
              for h in heads]
        ws, tots = _sb_weights_many(zs, seen, tri, runs)
        pv = [_dot(ws[h], vbf_ref[pl.ds(start, KEY_TILE), (h // 2) * LANES:(h // 2 + 1) * LANES])
              for h in heads]
        for h in heads:
            run_ref[h] = runs[h] + tots[h]
        for pair in range(n_pairs):
            acc_ref[pair] += jnp.where(first_head, pv[2 * pair], pv[2 * pair + 1])

    last = pl.multiple_of((n_kb - 1) * KEY_TILE, KEY_TILE)
    q_pos = qi * Q_TILE + lax.broadcasted_iota(jnp.int32, (Q_TILE, KEY_TILE), 0)
    k_pos = last + lax.broadcasted_iota(jnp.int32, (Q_TILE, KEY_TILE), 1)
    block(last, k_pos < q_pos)

    def body(j, _):
        block(pl.multiple_of((n_kb - 1 - j) * KEY_TILE, KEY_TILE), None)
        return 0

    lax.fori_loop(1, n_kb, body, 0)
    for pair in range(n_pairs):
        o_ref[:, pair * LANES:(pair + 1) * LANES] = acc_ref[pair].astype(BF16)


def _sb_prompt_call(q, k, v, n_streams, seq):
    nq = seq // Q_TILE
    return pl.pallas_call(
        _sb_prompt_kernel,
        grid=(n_streams, nq),
        in_specs=[
            pl.BlockSpec((Q_TILE, SB_WIDTH), lambda b, i: (b * nq + i, 0)),
            pl.BlockSpec((seq, SB_WIDTH), lambda b, i: (b, 0)),
            pl.BlockSpec((seq, SB_WIDTH), lambda b, i: (b, 0)),
        ],
        out_specs=pl.BlockSpec((Q_TILE, SB_WIDTH), lambda b, i: (b * nq + i, 0)),
        out_shape=jax.ShapeDtypeStruct(q.shape, BF16),
        scratch_shapes=[
            pltpu.VMEM((seq, SB_WIDTH), BF16),
            pltpu.VMEM((seq, SB_WIDTH), BF16),
            pltpu.VMEM((SB_HEADS, Q_TILE, LANES), BF16),
            pltpu.VMEM((SB_HEADS // 2, Q_TILE, LANES), F32),
            pltpu.VMEM((SB_HEADS, Q_TILE, 1), F32),
        ],
        compiler_params=_params(("parallel", "arbitrary")),
        name="sb_prompt",
    )(q, k, v)


def _mla_prompt_kernel(q_ref, ckv_ref, kr_ref, wuk_ref, o_ref, cbf_ref, kcat_ref, m_ref, l_ref, acc_ref):
    qi = pl.program_id(1)

    @pl.when(qi == 0)
    def _():
        c = ckv_ref[...].astype(BF16)
        cbf_ref[...] = c
        kr = kr_ref[...]
        fill = jnp.zeros((kr.shape[0], LANES - MLA_NOPE_DIM - MLA_ROPE_DIM), F32)
        for h in range(MLA_HEADS):
            kcat_ref[h] = jnp.concatenate([_dot(c, wuk_ref[h]), kr, fill], axis=-1).astype(BF16)

    m_ref[...] = jnp.full(m_ref.shape, NEG_INF, F32)
    l_ref[...] = jnp.zeros(l_ref.shape, F32)
    acc_ref[...] = jnp.zeros(acc_ref.shape, F32)
    n_kb = (qi * Q_TILE + Q_TILE + KEY_TILE - 1) // KEY_TILE

    def block(start, visible):
        c = cbf_ref[pl.ds(start, KEY_TILE), :]
        heads = range(MLA_HEADS)
        m_old = [m_ref[h] for h in heads]
        l_old = [l_ref[h] for h in heads]
        ss = [_dot_nt(q_ref[h], kcat_ref[h, pl.ds(start, KEY_TILE), :]) for h in heads]
        if visible is not None:
            ss = [jnp.where(visible, s, NEG_INF) for s in ss]
        m_new = [jnp.maximum(m_old[h], jnp.max(jnp.maximum(ss[h][:, :LANES], ss[h][:, LANES:]),
                                               axis=1, keepdims=True)) for h in heads]
        p_lo = [jnp.exp(ss[h][:, :LANES] - m_new[h]) for h in heads]
        p_hi = [jnp.exp(ss[h][:, LANES:] - m_new[h]) for h in heads]
        alpha = [jnp.exp(m_old[h] - m_new[h]) for h in heads]
        pv = [_dot(jnp.concatenate([p_lo[h], p_hi[h]], axis=1).astype(BF16), c) for h in heads]
        for h in heads:
            m_ref[h] = m_new[h]
            l_ref[h] = alpha[h][:, :1] * l_old[h] + jnp.sum(p_lo[h] + p_hi[h], axis=1, keepdims=True)
        for h in heads:
            acc_ref[h] = jnp.concatenate([alpha[h], alpha[h]], axis=1) * acc_ref[h] + pv[h]

    def body(j, _):
        block(pl.multiple_of(j * KEY_TILE, KEY_TILE), None)
        return 0

    lax.fori_loop(0, n_kb - 1, body, 0)
    last = pl.multiple_of((n_kb - 1) * KEY_TILE, KEY_TILE)
    q_chunk = (qi * Q_TILE + lax.broadcasted_iota(jnp.int32, (Q_TILE, KEY_TILE), 0)) // CHUNK
    k_chunk = (last + lax.broadcasted_iota(jnp.int32, (Q_TILE, KEY_TILE), 1)) // CHUNK
    block(last, k_chunk <= q_chunk)
    for h in range(MLA_HEADS):
        o_ref[h] = (acc_ref[h] / l_ref[h]).astype(BF16)


def _mla_prompt_call(qcat, ckv, krope, w_uk, n_streams, seq):
    nq = seq // Q_TILE
    qmap = lambda b, i: (0, b * nq + i, 0)
    return pl.pallas_call(
        _mla_prompt_kernel,
        grid=(n_streams, nq),
        in_specs=[
            pl.BlockSpec((MLA_HEADS, Q_TILE, LANES), qmap),
            pl.BlockSpec((seq, MLA_KV_RANK), lambda b, i: (b, 0)),
            pl.BlockSpec((seq, MLA_ROPE_DIM), lambda b, i: (b, 0)),
            _const_spec((MLA_HEADS, MLA_KV_RANK, MLA_NOPE_DIM)),
        ],
        out_specs=pl.BlockSpec((MLA_HEADS, Q_TILE, MLA_KV_RANK), qmap),
        out_shape=jax.ShapeDtypeStruct((MLA_HEADS, qcat.shape[1], MLA_KV_RANK), BF16),
        scratch_shapes=[
            pltpu.VMEM((seq, MLA_KV_RANK), BF16),
            pltpu.VMEM((MLA_HEADS, seq, LANES), BF16),
            pltpu.VMEM((MLA_HEADS, Q_TILE, LANES), F32),
            pltpu.VMEM((MLA_HEADS, Q_TILE, 1), F32),
            pltpu.VMEM((MLA_HEADS, Q_TILE, MLA_KV_RANK), F32),
        ],
        compiler_params=_params(("parallel", "arbitrary")),
        name="mla_prompt",
    )(qcat, ckv, krope, w_uk)


def _sb_weights_chain(zs, tri, run):
    sps = [_softplus(z) for z in zs]
    laters = [_dot(sp.astype(BF16), tri) for sp in sps]
    ws = []
    for z, sp, later in zip(zs, sps, laters):
        ws.append(jnp.exp((z - sp) - (later + run)).astype(BF16))
        run = run + jnp.sum(sp, axis=1, keepdims=True)
    return ws, run


def _sb_decode_kernel(q_ref, kn_ref, vn_ref, kc_ref, vc_ref, o_ref, acc_ref, run_ref):
    kb = pl.program_id(1)
    n_q = q_ref.shape[0]
    rows = SB_HEADS * n_q
    heads = range(SB_HEADS)
    tri = _later_sum_matrix(KEY_TILE)
    q = q_ref[...]
    qh = [q[:, h * SB_HEAD_DIM:(h + 1) * SB_HEAD_DIM] for h in heads]

    def scores(keys_of):
        return jnp.concatenate([_dot_nt(qh[h], keys_of(h)) for h in heads], axis=0)

    def weighted(w, values_of):
        return jnp.concatenate([_dot(w[h * n_q:(h + 1) * n_q], values_of(h)) for h in heads], axis=0)

    @pl.when(kb == 0)
    def _():
        pad = jnp.zeros((LANES - n_q, SB_HEAD_DIM), BF16)
        kn = kn_ref[...].astype(BF16)
        vn = vn_ref[...].astype(BF16)
        head = lambda a, h: jnp.concatenate([a[:, h * SB_HEAD_DIM:(h + 1) * SB_HEAD_DIM], pad], axis=0)
        z = scores(lambda h: head(kn, h))
        i = lax.broadcasted_iota(jnp.int32, (rows, LANES), 0) % n_q
        j = lax.broadcasted_iota(jnp.int32, (rows, LANES), 1)
        w, tot = _sb_weights(z, j < i, tri[:LANES, :LANES], jnp.zeros((rows, 1), F32))
        acc_ref[...] = weighted(w, lambda h: head(vn, h))
        run_ref[...] = tot

    n_sub = kc_ref.shape[2] // KEY_TILE
    z = jnp.concatenate([_dot(qh[h], kc_ref[h].astype(BF16)) for h in heads], axis=0)
    subs = list(range(n_sub - 1, -1, -1))
    ws, run = _sb_weights_chain([z[:, s * KEY_TILE:(s + 1) * KEY_TILE] for s in subs], tri, run_ref[...])
    w = jnp.concatenate(ws[::-1], axis=1)
    acc_ref[...] += jnp.concatenate(
        [_dot_nt(w[h * n_q:(h + 1) * n_q], vc_ref[h].astype(BF16)) for h in heads], axis=0)
    run_ref[...] = run

    @pl.when(kb == pl.num_programs(1) - 1)
    def _():
        a = acc_ref[...]
        o_ref[...] = jnp.concatenate([a[h * n_q:(h + 1) * n_q] for h in heads], axis=1).astype(BF16)


def _sb_decode_call(q, k_new, v_new, cache_k, cache_v, layer, n_q):
    t = q.shape[0]
    n_streams = t // n_q
    past = cache_k.shape[4]
    n_kb = past // DECODE_KEY_BLOCK
    row = lambda s, j: (s, 0)
    cmap = lambda s, j: (layer, s, 0, 0, n_kb - 1 - j)
    rows = SB_HEADS * n_q
    cache_block = (None, None, SB_HEADS, SB_HEAD_DIM, DECODE_KEY_BLOCK)
    return pl.pallas_call(
        _sb_decode_kernel,
        grid=(n_streams, n_kb),
        in_specs=[
            pl.BlockSpec((n_q, SB_WIDTH), row),
            pl.BlockSpec((n_q, SB_WIDTH), row),
            pl.BlockSpec((n_q, SB_WIDTH), row),
            pl.BlockSpec(cache_block, cmap),
            pl.BlockSpec(cache_block, cmap),
        ],
        out_specs=pl.BlockSpec((n_q, SB_WIDTH), row),
        out_shape=jax.ShapeDtypeStruct(q.shape, BF16),
        scratch_shapes=[pltpu.VMEM((rows, SB_HEAD_DIM), F32), pltpu.VMEM((rows, 1), F32)],
        compiler_params=_params(("parallel", "arbitrary")),
        name="sb_decode",
    )(q, k_new, v_new, cache_k, cache_v)


def _mla_decode_kernel(q_ref, cn_ref, rn_ref, cc_ref, rc_ref, wukt_ref, o_ref, m_ref, l_ref, acc_ref,
                       *, past):
    kb = pl.program_id(1)
    n_q = q_ref.shape[1]
    rows = MLA_HEADS * n_q
    ql = jnp.concatenate([_dot(q_ref[h][:, :MLA_NOPE_DIM], wukt_ref[h]) for h in range(MLA_HEADS)],
                         axis=0).astype(BF16)
    qr = jnp.concatenate([q_ref[h][:, MLA_NOPE_DIM:MLA_NOPE_DIM + MLA_ROPE_DIM] for h in range(MLA_HEADS)],
                         axis=0)

    def update(s, c):
        m_old = m_ref[...]
        m_new = jnp.maximum(m_old, jnp.max(s, axis=1, keepdims=True))
        p = jnp.exp(s - m_new)
        alpha = jnp.exp(m_old - m_new)
        l_ref[...] = alpha * l_ref[...] + jnp.sum(p, axis=1, keepdims=True)
        acc_ref[...] = alpha * acc_ref[...] + _dot(p.astype(BF16), c)
        m_ref[...] = m_new

    @pl.when(kb == 0)
    def _():
        m_ref[...] = jnp.full((rows, 1), NEG_INF, F32)
        l_ref[...] = jnp.zeros((rows, 1), F32)
        acc_ref[...] = jnp.zeros((rows, MLA_KV_RANK), F32)
        c = jnp.concatenate([cn_ref[...].astype(BF16), jnp.zeros((LANES - n_q, MLA_KV_RANK), BF16)], axis=0)
        r = jnp.concatenate([rn_ref[...].astype(BF16), jnp.zeros((LANES - n_q, MLA_ROPE_DIM), BF16)], axis=0)
        s = _dot_nt(ql, c) + _dot_nt(qr, r)
        i = lax.broadcasted_iota(jnp.int32, (rows, LANES), 0) % n_q
        j = lax.broadcasted_iota(jnp.int32, (rows, LANES), 1)
        vis = jnp.logical_and(j < n_q, (past + j) // CHUNK <= (past + i) // CHUNK)
        update(jnp.where(vis, s, NEG_INF), c)

    c = cc_ref[...].astype(BF16)
    update(_dot_nt(ql, c) + _dot(qr, rc_ref[...].astype(BF16)), c)

    @pl.when(kb == pl.num_programs(1) - 1)
    def _():
        o = acc_ref[...] / l_ref[...]
        o_ref[...] = o.reshape(MLA_HEADS, n_q, MLA_KV_RANK).astype(BF16)


def _mla_decode_call(qcat, ckv_new, krope_new, cache_ckv, cache_krope, w_ukt, layer, n_q):
    t = qcat.shape[1]
    n_streams = t // n_q
    past = cache_ckv.shape[2]
    n_kb = past // DECODE_KEY_BLOCK
    rows = MLA_HEADS * n_q
    qmap = lambda s, j: (0, s, 0)
    row = lambda s, j: (s, 0)
    cmap = lambda s, j: (layer, s, j, 0)
    rmap = lambda s, j: (layer, s, 0, j)
    return pl.pallas_call(
        functools.partial(_mla_decode_kernel, past=past),
        grid=(n_streams, n_kb),
        in_specs=[
            pl.BlockSpec((MLA_HEADS, n_q, LANES), qmap),
            pl.BlockSpec((n_q, MLA_KV_RANK), row),
            pl.BlockSpec((n_q, MLA_ROPE_DIM), row),
            pl.BlockSpec((None, None, DECODE_KEY_BLOCK, MLA_KV_RANK), cmap),
            pl.BlockSpec((None, None, MLA_ROPE_DIM, DECODE_KEY_BLOCK), rmap),
            _const_spec((MLA_HEADS, MLA_NOPE_DIM, MLA_KV_RANK)),
        ],
        out_specs=pl.BlockSpec((MLA_HEADS, n_q, MLA_KV_RANK), qmap),
        out_shape=jax.ShapeDtypeStruct((MLA_HEADS, t, MLA_KV_RANK), BF16),
        scratch_shapes=[
            pltpu.VMEM((rows, 1), F32),
            pltpu.VMEM((rows, 1), F32),
            pltpu.VMEM((rows, MLA_KV_RANK), F32),
        ],
        compiler_params=_params(("parallel", "arbitrary")),
        name="mla_decode",
    )(qcat, ckv_new, krope_new, cache_ckv, cache_krope, w_ukt)


def _merge_kernel(x_ref, osb_ref, olat_ref, nm_ref, wg_ref, wuv_ref, wsbo_ref, wmlao_ref, wout_ref, h_ref):
    x = x_ref[...]
    xn = _rms(x, nm_ref[...]).astype(BF16)
    g = _dot(xn, wg_ref[...])
    a = _dot(osb_ref[...], wsbo_ref[...])
    pieces = []
    for pair in range(MLA_HEADS // 2):
        pieces.append(_dot(olat_ref[2 * pair], wuv_ref[2 * pair])
                      + _dot(olat_ref[2 * pair + 1], wuv_ref[2 * pair + 1]))
    o_mla = jnp.concatenate(pieces, axis=-1).astype(BF16)
    b = _dot(o_mla, wmlao_ref[...])
    merged = jax.nn.sigmoid(g[:, :D_MODEL]) * a + jax.nn.sigmoid(g[:, D_MODEL:]) * b
    h_ref[...] = x + _dot(merged.astype(BF16), wout_ref[...])


def _merge_call(x, o_sb, o_lat, mw):
    t = x.shape[0]
    tm = TOKEN_TILE
    row = lambda i: (i, 0)
    return pl.pallas_call(
        _merge_kernel,
        grid=(t // tm,),
        in_specs=[
            pl.BlockSpec((tm, D_MODEL), row),
            pl.BlockSpec((tm, SB_WIDTH), row),
            pl.BlockSpec((MLA_HEADS, tm, MLA_KV_RANK), lambda i: (0, i, 0)),
            _const_spec((1, D_MODEL)),
            _const_spec((D_MODEL, 2 * D_MODEL)),
            _const_spec((MLA_HEADS, MLA_KV_RANK, LANES)),
            _const_spec((SB_WIDTH, D_MODEL)),
            _const_spec((MLA_WIDTH, D_MODEL)),
            _const_spec((D_MODEL, D_MODEL)),
        ],
        out_specs=pl.BlockSpec((tm, D_MODEL), row),
        out_shape=jax.ShapeDtypeStruct((t, D_MODEL), F32),
        compiler_params=_params(("parallel",)),
        name="merge_out",
    )(x, o_sb, o_lat, mw["norm_mix"], mw["w_gates"], mw["w_uv"], mw["w_sb_o"], mw["w_mla_o"], mw["w_out"])


def _dense_ffn_kernel(x_ref, nf_ref, wg_ref, wu_ref, wd_ref, nfin_ref, o_ref, *, n_split, final_norm):
    x = x_ref[...]
    xn = _rms(x, nf_ref[...]).astype(BF16)
    fc = wg_ref.shape[1] // n_split
    acc = x
    for c in range(n_split):
        gt = _dot(xn, wg_ref[:, c * fc:(c + 1) * fc])
        up = _dot(xn, wu_ref[:, c * fc:(c + 1) * fc])
        hid = (gt * jax.nn.sigmoid(gt) * up).astype(BF16)
        acc = acc + _dot(hid, wd_ref[c * fc:(c + 1) * fc, :])
    o_ref[...] = _rms(acc, nfin_ref[...]) if final_norm else acc


def _dense_ffn_call(x, norm_ffn, wg, wu, wd, norm_final, final_norm):
    t = x.shape[0]
    tm = TOKEN_TILE
    d_ff = wg.shape[1]
    n_split = 2 if d_ff % (2 * LANES) == 0 else 1
    row = lambda i: (i, 0)
    return pl.pallas_call(
        functools.partial(_dense_ffn_kernel, n_split=n_split, final_norm=final_norm),
        grid=(t // tm,),
        in_specs=[
            pl.BlockSpec((tm, D_MODEL), row),
            _const_spec((1, D_MODEL)),
            _const_spec((D_MODEL, d_ff)),
            _const_spec((D_MODEL, d_ff)),
            _const_spec((d_ff, D_MODEL)),
            _const_spec((1, D_MODEL)),
        ],
        out_specs=pl.BlockSpec((tm, D_MODEL), row),
        out_shape=jax.ShapeDtypeStruct((t, D_MODEL), F32),
        compiler_params=_params(("parallel",)),
        name="dense_ffn",
    )(x, norm_ffn, wg, wu, wd, norm_final)


def _router_kernel(x_ref, nf_ref, rw_ref, rb_ref, xw_ref, e_ref, g_ref):
    xn = _rms(x_ref[...], nf_ref[...])
    xw_ref[...] = xn

    logits = jnp.dot(xn, rw_ref[...], preferred_element_type=F32, precision=lax.Precision.HIGHEST)
    logits = logits + rb_ref[...]
    lane = lax.broadcasted_iota(jnp.int32, logits.shape, 1).astype(F32)
    lg = jnp.where(lane < N_EXPERTS, logits, -jnp.inf)
    m1 = jnp.max(lg, axis=1, keepdims=True)
    i1 = jnp.min(jnp.where(lg == m1, lane, float(LANES)), axis=1, keepdims=True)
    lg2 = jnp.where(lane == i1, -jnp.inf, lg)
    m2 = jnp.max(lg2, axis=1, keepdims=True)
    i2 = jnp.min(jnp.where(lg2 == m2, lane, float(LANES)), axis=1, keepdims=True)
    e2 = jnp.exp(m2 - m1)
    den = 1.0 + e2
    e_ref[...] = jnp.concatenate([i1, i2], axis=1).astype(jnp.int32)
    g_ref[...] = jnp.concatenate([1.0 / den, e2 / den], axis=1)


def _router_call(x, norm_ffn, rw, rb):
    t = x.shape[0]
    tm = TOKEN_TILE
    row = lambda i: (i, 0)
    return pl.pallas_call(
        _router_kernel,
        grid=(t // tm,),
        in_specs=[
            pl.BlockSpec((tm, D_MODEL), row),
            _const_spec((1, D_MODEL)),
            _const_spec((D_MODEL, LANES)),
            _const_spec((1, LANES)),
        ],
        out_specs=[
            pl.BlockSpec((tm, D_MODEL), row),
            pl.BlockSpec((tm, TOP_K), row),
            pl.BlockSpec((tm, TOP_K), row),
        ],
        out_shape=[
            jax.ShapeDtypeStruct((t, D_MODEL), F32),
            jax.ShapeDtypeStruct((t, TOP_K), jnp.int32),
            jax.ShapeDtypeStruct((t, TOP_K), F32),
        ],
        compiler_params=_params(("parallel",)),
        name="router",
    )(x, norm_ffn, rw, rb)


def _row_copy(src, dst, s, d, sem):
    return pltpu.make_async_copy(src.at[pl.ds(s, 1), :], dst.at[pl.ds(d, 1), :], sem)


def _dispatch_kernel(dest_ref, xw_ref, xs_in_ref, xs_ref, sem):
    del xs_in_ref
    tm = xw_ref.shape[0]

    def start(t, _):
        for k in range(TOP_K):
            _row_copy(xw_ref, xs_ref, t, dest_ref[TOP_K * t + k], sem).start()
        return 0

    def wait(t, _):
        for k in range(TOP_K):
            _row_copy(xw_ref, xs_ref, 0, 0, sem).wait()
        return 0

    lax.fori_loop(0, tm, start, 0, unroll=8)
    lax.fori_loop(0, tm, wait, 0, unroll=8)


def _dispatch_call(dest_flat, xw, xs):
    t = xw.shape[0]
    tm = TOKEN_TILE
    return pl.pallas_call(
        _dispatch_kernel,
        grid=(t // tm,),
        in_specs=[
            pl.BlockSpec((tm * TOP_K,), lambda i: (i,), memory_space=pltpu.SMEM),
            pl.BlockSpec((tm, D_MODEL), lambda i: (i, 0)),
            pl.BlockSpec(memory_space=pl.ANY),
        ],
        out_specs=pl.BlockSpec(memory_space=pl.ANY),
        out_shape=jax.ShapeDtypeStruct(xs.shape, xs.dtype),
        scratch_shapes=[pltpu.SemaphoreType.DMA],
        input_output_aliases={2: 0},
        compiler_params=_params(("arbitrary",)),
        name="moe_dispatch",
    )(dest_flat, xw, xs)


def _expert_kernel(tile_e_ref, n_used_ref, xs_ref, wg_ref, wu_ref, wd_ref, y_ref, x_ref):
    r = pl.program_id(0)
    f = pl.program_id(1)

    @pl.when(f == 0)
    def _():
        x_ref[...] = xs_ref[...].astype(BF16)
        y_ref[...] = jnp.zeros_like(y_ref)

    @pl.when(r < n_used_ref[0])
    def _():
        x = x_ref[...]
        gt = _dot(x, wg_ref[...])
        up = _dot(x, wu_ref[...])
        hid = (gt * jax.nn.sigmoid(gt) * up).astype(BF16)
        y_ref[...] += _dot(hid, wd_ref[...])


def _expert_call(tile_e, n_used, xs, wg, wu, wd):
    n_rows = xs.shape[0]
    rt = EXPERT_ROW_TILE
    nf = EXPERT_F_SPLIT
    d_e = wg.shape[2]
    fc = d_e // nf

    def fidx(r, f):
        return jnp.where(r % 2 == 0, f, nf - 1 - f)

    grid_spec = pltpu.PrefetchScalarGridSpec(
        num_scalar_prefetch=2,
        grid=(n_rows // rt, nf),
        in_specs=[
            pl.BlockSpec((rt, D_MODEL), lambda r, f, te, nu: (r, 0)),
            pl.BlockSpec((None, D_MODEL, fc), lambda r, f, te, nu: (te[r], 0, fidx(r, f))),
            pl.BlockSpec((None, D_MODEL, fc), lambda r, f, te, nu: (te[r], 0, fidx(r, f))),
            pl.BlockSpec((None, fc, D_MODEL), lambda r, f, te, nu: (te[r], fidx(r, f), 0)),
        ],
        out_specs=pl.BlockSpec((rt, D_MODEL), lambda r, f, te, nu: (r, 0)),
        scratch_shapes=[pltpu.VMEM((rt, D_MODEL), BF16)],
    )
    return pl.pallas_call(
        _expert_kernel,
        grid_spec=grid_spec,
        out_shape=jax.ShapeDtypeStruct((n_rows, D_MODEL), F32),
        compiler_params=_params(("arbitrary", "arbitrary")),
        name="moe_experts",
    )(tile_e, n_used, xs, wg, wu, wd)


def _combine_kernel(dest_ref, h_ref, g_ref, y_ref, nfin_ref, o_ref, buf0, buf1, sem, *, final_norm):
    tm = h_ref.shape[0]

    def start(t, _):
        _row_copy(y_ref, buf0, dest_ref[TOP_K * t], t, sem).start()
        _row_copy(y_ref, buf1, dest_ref[TOP_K * t + 1], t, sem).start()
        return 0

    def wait(t, _):
        _row_copy(y_ref, buf0, 0, 0, sem).wait()
        _row_copy(y_ref, buf1, 0, 0, sem).wait()
        return 0

    lax.fori_loop(0, tm, start, 0, unroll=8)
    lax.fori_loop(0, tm, wait, 0, unroll=8)
    g = g_ref[...]
    y = h_ref[...] + (buf0[...] * g[:, 0:1] + buf1[...] * g[:, 1:2])
    o_ref[...] = _rms(y, nfin_ref[...]) if final_norm else y


def _combine_call(dest_flat, h, gate, yb, norm_final, final_norm):
    t = h.shape[0]
    tm = TOKEN_TILE
    row = lambda i: (i, 0)
    return pl.pallas_call(
        functools.partial(_combine_kernel, final_norm=final_norm),
        grid=(t // tm,),
        in_specs=[
            pl.BlockSpec((tm * TOP_K,), lambda i: (i,), memory_space=pltpu.SMEM),
            pl.BlockSpec((tm, D_MODEL), row),
            pl.BlockSpec((tm, TOP_K), row),
            pl.BlockSpec(memory_space=pl.ANY),
            _const_spec((1, D_MODEL)),
        ],
        out_specs=pl.BlockSpec((tm, D_MODEL), row),
        out_shape=jax.ShapeDtypeStruct((t, D_MODEL), F32),
        scratch_shapes=[pltpu.VMEM((tm, D_MODEL), F32), pltpu.VMEM((tm, D_MODEL), F32),
                        pltpu.SemaphoreType.DMA],
        compiler_params=_params(("arbitrary",)),
        name="moe_combine",
    )(dest_flat, h, gate, yb, norm_final)


def _moe_layout(top_e):
    rt = EXPERT_ROW_TILE
    e_flat = top_e.reshape(-1)
    n_assign = e_flat.shape[0]
    onehot = (e_flat[:, None] == jnp.arange(N_EXPERTS, dtype=jnp.int32)[None, :]).astype(jnp.int32)
    incl = jnp.cumsum(onehot, axis=0)
    counts = incl[-1]
    rank = jnp.sum((incl - onehot) * onehot, axis=1)
    padded = (counts + rt - 1) // rt * rt
    pad_end = jnp.cumsum(padded)
    pad_start = pad_end - padded
    dest = (jnp.sum(onehot * pad_start[None, :], axis=1) + rank).astype(jnp.int32)
    n_rows = -(-(n_assign + N_EXPERTS * (rt - 1)) // rt) * rt
    n_tiles = n_rows // rt
    tile_start = jnp.arange(n_tiles, dtype=jnp.int32) * rt
    tile_e = jnp.minimum(
        jnp.sum((pad_end[None, :] <= tile_start[:, None]).astype(jnp.int32), axis=1),
        N_EXPERTS - 1).astype(jnp.int32)
    n_used = (pad_end[-1] // rt).astype(jnp.int32).reshape(1)
    return dest, tile_e, n_used, n_rows


def _moe(streams, norm_ffn, rw, rb, wg, wu, wd, norm_final, final_norm):
    routed = [_router_call(h, norm_ffn, rw, rb) for h in streams]
    top_e = jnp.concatenate([r[1] for r in routed], axis=0)
    dest, tile_e, n_used, n_rows = _moe_layout(top_e)
    xs = jnp.zeros((n_rows, D_MODEL), F32)
    dests = []
    off = 0
    for h, (xw, _, _) in zip(streams, routed):
        n = h.shape[0] * TOP_K
        dests.append(dest[off:off + n])
        off += n
        xs = _dispatch_call(dests[-1], xw, xs)
    yb = _expert_call(tile_e, n_used, xs, wg, wu, wd)
    return [_combine_call(d, h, r[2], yb, norm_final, final_norm)
            for d, h, r in zip(dests, streams, routed)]


def _mixer_weights(l, norm_mix, w_in, mla_q_norm, w_uq, mla_kv_norm, w_uk, w_uv, w_sb_o, w_mla_o, w_out):
    w = w_in[l]
    o_gate = 3 * SB_WIDTH + MLA_Q_RANK + MLA_KV_RANK + MLA_ROPE_DIM
    w1 = jnp.concatenate(
        [w[:, :o_gate], jnp.zeros((D_MODEL, LANES - MLA_ROPE_DIM), F32)], axis=1).astype(BF16)
    uq = w_uq[l].reshape(MLA_Q_RANK, MLA_HEADS, MLA_NOPE_DIM + MLA_ROPE_DIM)
    uq = jnp.concatenate([
        uq[:, :, :MLA_NOPE_DIM].reshape(MLA_Q_RANK, -1),
        uq[:, :, MLA_NOPE_DIM:MLA_NOPE_DIM + ROPE_HALF].reshape(MLA_Q_RANK, -1),
        uq[:, :, MLA_NOPE_DIM + ROPE_HALF:].reshape(MLA_Q_RANK, -1)], axis=1).astype(BF16)
    uv = jnp.transpose(w_uv[l], (1, 0, 2))
    zeros = jnp.zeros_like(uv)
    even = (jnp.arange(MLA_HEADS) % 2 == 0)[:, None, None]
    uv = jnp.concatenate([jnp.where(even, uv, zeros), jnp.where(even, zeros, uv)], axis=2).astype(BF16)
    return {
        "norm_mix": norm_mix[l].reshape(1, D_MODEL),
        "w1": w1,
        "w_gates": w[:, o_gate:].astype(BF16),
        "q_norm": mla_q_norm[l].reshape(1, MLA_Q_RANK),
        "w_uq": uq,
        "kv_norm": mla_kv_norm[l].reshape(1, MLA_KV_RANK),
        "w_ukt": jnp.transpose(w_uk[l], (1, 2, 0)).astype(BF16),
        "w_uk": jnp.transpose(w_uk[l], (1, 0, 2)).astype(BF16),
        "w_uv": uv,
        "w_sb_o": w_sb_o[l].astype(BF16),
        "w_mla_o": w_mla_o[l].astype(BF16),
        "w_out": w_out[l].astype(BF16),
    }


def _rope_tables(pos, rows):
    inv_freq = jnp.power(ROPE_BASE, -jnp.arange(ROPE_HALF, dtype=F32) / ROPE_HALF)
    ang = pos.astype(F32)[:, None] * inv_freq[None, :]
    reps = (rows // pos.shape[0], LANES // ROPE_HALF)
    return jnp.tile(jnp.cos(ang), reps), jnp.tile(jnp.sin(ang), reps)


def kernel(x_prompt, x_sample, cache_sb_k, cache_sb_v, cache_mla_ckv, cache_mla_krope, norm_mix, w_in,
           mla_q_norm, w_uq, mla_kv_norm, w_uk, w_uv, w_sb_o, w_mla_o, w_out, norm_ffn, ffn_w_gate,
           ffn_w_up, ffn_w_down, router_w, router_b, moe_w_gate, moe_w_up, moe_w_down, norm_final):
    n_p, seq, _ = x_prompt.shape
    n_s, n_q, _ = x_sample.shape
    depth = w_in.shape[0]
    past = cache_sb_k.shape[2]
    assert seq % TOKEN_TILE == 0 and (n_s * n_q) % TOKEN_TILE == 0 and TOKEN_TILE % n_q == 0
    assert seq % KEY_TILE == 0 and past % DECODE_KEY_BLOCK == 0 and n_q <= LANES

    cos_p, sin_p = _rope_tables(jnp.arange(seq, dtype=jnp.int32), seq)
    cos_s, sin_s = _rope_tables(past + jnp.arange(n_q, dtype=jnp.int32), TOKEN_TILE)
    nfin = norm_final.reshape(1, D_MODEL)
    cache_kt = jnp.transpose(cache_sb_k, (0, 1, 3, 4, 2))
    cache_vt = jnp.transpose(cache_sb_v, (0, 1, 3, 4, 2))
    cache_rt = jnp.transpose(cache_mla_krope, (0, 1, 3, 2))

    hp = x_prompt.reshape(n_p * seq, D_MODEL)
    hs = x_sample.reshape(n_s * n_q, D_MODEL)
    rows_p, rows_s = [], []
    for l in range(depth):
        mw = _mixer_weights(l, norm_mix, w_in, mla_q_norm, w_uq, mla_kv_norm, w_uk, w_uv,
                            w_sb_o, w_mla_o, w_out)
        sbq, sbk, sbv, ckv, krope, qcat = _proj_call(hp, mw, cos_p, sin_p)
        o_sb = _sb_prompt_call(sbq, sbk, sbv, n_p, seq)
        o_lat = _mla_prompt_call(qcat, ckv, krope, mw["w_uk"], n_p, seq)
        hp = _merge_call(hp, o_sb, o_lat, mw)
        rows_p.append((sbk, sbv, ckv, krope))

        sbq, sbk, sbv, ckv, krope, qcat = _proj_call(hs, mw, cos_s, sin_s)
        o_sb = _sb_decode_call(sbq, sbk, sbv, cache_kt, cache_vt, l, n_q)
        o_lat = _mla_decode_call(qcat, ckv, krope, cache_mla_ckv, cache_rt, mw["w_ukt"], l, n_q)
        hs = _merge_call(hs, o_sb, o_lat, mw)
        rows_s.append((sbk, sbv, ckv, krope))

        last = l == depth - 1
        nf = norm_ffn[l].reshape(1, D_MODEL)
        i = l // 2
        if l % 2 == 0:
            wg, wu, wd = ffn_w_gate[i].astype(BF16), ffn_w_up[i].astype(BF16), ffn_w_down[i].astype(BF16)
            hp = _dense_ffn_call(hp, nf, wg, wu, wd, nfin, last)
            hs = _dense_ffn_call(hs, nf, wg, wu, wd, nfin, last)
        else:
            rw = jnp.concatenate([router_w[i], jnp.zeros((D_MODEL, LANES - N_EXPERTS), F32)], axis=1)
            rb = jnp.concatenate([router_b[i], jnp.zeros((LANES - N_EXPERTS,), F32)]).reshape(1, LANES)
            hp, hs = _moe([hp, hs], nf, rw, rb, moe_w_gate[i].astype(BF16), moe_w_up[i].astype(BF16),
                          moe_w_down[i].astype(BF16), nfin, last)

    def stack(rows, idx, shape):
        return jnp.stack([r[idx] for r in rows]).reshape((depth,) + shape)

    kv_p = (n_p, seq, SB_HEADS, SB_HEAD_DIM)
    kv_s = (n_s, n_q, SB_HEADS, SB_HEAD_DIM)
    return (hp.reshape(n_p, seq, D_MODEL), hs.reshape(n_s, n_q, D_MODEL),
            stack(rows_p, 0, kv_p), stack(rows_p, 1, kv_p),
            stack(rows_p, 2, (n_p, seq, MLA_KV_RANK)), stack(rows_p, 3, (n_p, seq, MLA_ROPE_DIM)),
            stack(rows_s, 0, kv_s), stack(rows_s, 1, kv_s),
            stack(rows_s, 2, (n_s, n_q, MLA_KV_RANK)), stack(rows_s, 3, (n_s, n_q, MLA_ROPE_DIM)))
```

```python
import functools

import jax
import jax.numpy as jnp
from jax import lax
from jax.experimental import pallas as pl
from jax.experimental.pallas import tpu as pltpu

F32 = jnp.float32
BF16 = jnp.bfloat16

D_MODEL = 1024
CHUNK = 64
SB_HEADS = 8
SB_HEAD_DIM = 64
SB_WIDTH = SB_HEADS * SB_HEAD_DIM
MLA_HEADS = 8
MLA_NOPE_DIM = 64
MLA_ROPE_DIM = 32
MLA_V_DIM = 64
MLA_Q_RANK = 384
MLA_KV_RANK = 256
MLA_WIDTH = MLA_HEADS * MLA_V_DIM
MLA_SCALE = (MLA_NOPE_DIM + MLA_ROPE_DIM) ** -0.5
SB_SCALE = SB_HEAD_DIM ** -0.5
ROPE_BASE = 10000.0
ROPE_HALF = MLA_ROPE_DIM // 2
N_EXPERTS = 8
TOP_K = 2
RMS_EPS = 1e-6
NEG_INF = -1e30
NEG_LOG2_E = -1.4426950408889634

LANES = 128
MXU_DIM = 256
VMEM_LIMIT_BYTES = 56 * 1024 * 1024

TOKEN_TILE = 256
Q_TILE = 128
KEY_TILE = MXU_DIM
DECODE_KEY_BLOCK = 2048
EXPERT_ROW_TILE = 512
EXPERT_F_SPLIT = 2
IN_PROJ_COLS = 3 * SB_WIDTH + MLA_Q_RANK + MLA_KV_RANK + LANES


def _params(semantics):
    return pltpu.CompilerParams(dimension_semantics=semantics, vmem_limit_bytes=VMEM_LIMIT_BYTES)


def _rms(x, g):
    return x * lax.rsqrt(jnp.mean(x * x, axis=-1, keepdims=True) + RMS_EPS) * g


def _dot(a, b):
    return jnp.dot(a, b, preferred_element_type=F32)


def _dot_nt(a, b):
    return lax.dot_general(a, b, (((1,), (1,)), ((), ())), preferred_element_type=F32)


def _softplus(z):
    return jnp.maximum(z, 0.0) + jnp.log(1.0 + jnp.exp2(jnp.abs(z) * NEG_LOG2_E))


def _later_sum_matrix(n):
    r = lax.broadcasted_iota(jnp.int32, (n, n), 0)
    c = lax.broadcasted_iota(jnp.int32, (n, n), 1)
    return jnp.where(r > c, 1.0, 0.0).astype(BF16)


def _const_spec(shape):
    nd = len(shape)
    return pl.BlockSpec(shape, lambda *_: (0,) * nd)


def _proj_kernel(x_ref, nm_ref, w1_ref, qn_ref, wuq_ref, kvn_ref, cos_ref, sin_ref,
                 sbq_ref, sbk_ref, sbv_ref, ckv_ref, krope_ref, qcat_ref):
    xn = _rms(x_ref[...], nm_ref[...]).astype(BF16)
    p = _dot_nt(xn, w1_ref[...])
    o = 0
    sbq_ref[...] = (p[:, o:o + SB_WIDTH] * SB_SCALE).astype(BF16)
    o += SB_WIDTH
    sbk_ref[...] = p[:, o:o + SB_WIDTH]
    o += SB_WIDTH
    sbv_ref[...] = p[:, o:o + SB_WIDTH]
    o += SB_WIDTH
    cq = p[:, o:o + MLA_Q_RANK]
    o += MLA_Q_RANK
    ckv_ref[...] = _rms(p[:, o:o + MLA_KV_RANK], kvn_ref[...])
    o += MLA_KV_RANK
    kr = p[:, o:o + LANES]
    cos = cos_ref[...]
    sin = sin_ref[...]
    c16 = cos[:, :ROPE_HALF]
    s16 = sin[:, :ROPE_HALF]
    k1 = kr[:, :ROPE_HALF]
    k2 = kr[:, ROPE_HALF:MLA_ROPE_DIM]
    krope_ref[...] = jnp.concatenate([k1 * c16 - k2 * s16, k2 * c16 + k1 * s16], axis=-1)

    cqn = _rms(cq, qn_ref[...]).astype(BF16)
    q = _dot(cqn, wuq_ref[...])
    nope_w = MLA_HEADS * MLA_NOPE_DIM
    x1 = q[:, nope_w:nope_w + LANES]
    x2 = q[:, nope_w + LANES:nope_w + 2 * LANES]
    r1 = (x1 * cos - x2 * sin) * MLA_SCALE
    r2 = (x2 * cos + x1 * sin) * MLA_SCALE
    fill = jnp.zeros((q.shape[0], LANES - MLA_NOPE_DIM - MLA_ROPE_DIM), F32)
    for h in range(MLA_HEADS):
        qcat_ref[h] = jnp.concatenate(
            [q[:, h * MLA_NOPE_DIM:(h + 1) * MLA_NOPE_DIM] * MLA_SCALE,
             r1[:, h * ROPE_HALF:(h + 1) * ROPE_HALF], r2[:, h * ROPE_HALF:(h + 1) * ROPE_HALF], fill],
            axis=-1).astype(BF16)


def _proj_call(x, mw, cos_tab, sin_tab):
    t = x.shape[0]
    tm = TOKEN_TILE
    n_pos_tiles = cos_tab.shape[0] // tm
    row = lambda i: (i, 0)
    pos = lambda i: (i % n_pos_tiles, 0)
    head_row = lambda i: (0, i, 0)
    return pl.pallas_call(
        _proj_kernel,
        grid=(t // tm,),
        in_specs=[
            pl.BlockSpec((tm, D_MODEL), row),
            _const_spec((1, D_MODEL)),
            _const_spec((IN_PROJ_COLS, D_MODEL)),
            _const_spec((1, MLA_Q_RANK)),
            _const_spec((MLA_Q_RANK, MLA_HEADS * (MLA_NOPE_DIM + MLA_ROPE_DIM))),
            _const_spec((1, MLA_KV_RANK)),
            pl.BlockSpec((tm, LANES), pos),
            pl.BlockSpec((tm, LANES), pos),
        ],
        out_specs=[
            pl.BlockSpec((tm, SB_WIDTH), row),
            pl.BlockSpec((tm, SB_WIDTH), row),
            pl.BlockSpec((tm, SB_WIDTH), row),
            pl.BlockSpec((tm, MLA_KV_RANK), row),
            pl.BlockSpec((tm, MLA_ROPE_DIM), row),
            pl.BlockSpec((MLA_HEADS, tm, LANES), head_row),
        ],
        out_shape=[
            jax.ShapeDtypeStruct((t, SB_WIDTH), BF16),
            jax.ShapeDtypeStruct((t, SB_WIDTH), F32),
            jax.ShapeDtypeStruct((t, SB_WIDTH), F32),
            jax.ShapeDtypeStruct((t, MLA_KV_RANK), F32),
            jax.ShapeDtypeStruct((t, MLA_ROPE_DIM), F32),
            jax.ShapeDtypeStruct((MLA_HEADS, t, LANES), BF16),
        ],
        compiler_params=_params(("parallel",)),
        name="in_proj",
    )(x, mw["norm_mix"], mw["w1"], mw["q_norm"], mw["w_uq"], mw["kv_norm"], cos_tab, sin_tab)


def _sb_weights_many(zs, seen, tri, runs):
    sps = [_softplus(z) for z in zs]
    ts = sps if seen is None else [jnp.where(seen, sp, 0.0) for sp in sps]
    laters = [_dot(t.astype(BF16), tri) for t in ts]
    ws = []
    for z, sp, later, run in zip(zs, sps, laters, runs):
        w = jnp.exp((z - sp) - (later + run))
        if seen is not None:
            w = jnp.where(seen, w, 0.0)
        ws.append(w.astype(BF16))
    return ws, [jnp.sum(t, axis=1, keepdims=True) for t in ts]


def _sb_weights(z, seen, tri, run):
    ws, tots = _sb_weights_many([z], seen, tri, [run])
    return ws[0], tots[0]


def _sb_prompt_kernel(q_ref, k_ref, v_ref, o_ref, kt_ref, vt_ref, kbf_ref, vbf_ref, qm_ref, acc_ref, run_ref):
    qi = pl.program_id(1)
    n_pairs = SB_HEADS // 2

    @pl.when(qi == 0)
    def _():
        k = k_ref[...]
        v = v_ref[...]
        kbf_ref[...] = k.astype(BF16)
        vbf_ref[...] = v.astype(BF16)
        kt_ref[...] = k.T
        vt_ref[...] = v.T

    tri = _later_sum_matrix(KEY_TILE)
    n_kb = (qi * Q_TILE + Q_TILE + KEY_TILE - 1) // KEY_TILE
    lane = lax.broadcasted_iota(jnp.int32, (Q_TILE, LANES), 1)
    first_head = lane < SB_HEAD_DIM
    for pair in range(n_pairs):
        q_pair = q_ref[:, pair * LANES:(pair + 1) * LANES]
        zero = jnp.zeros_like(q_pair)
        qm_ref[2 * pair] = jnp.where(first_head, q_pair, zero)
        qm_ref[2 * pair + 1] = jnp.where(first_head, zero, q_pair)
    acc_ref[...] = jnp.zeros(acc_ref.shape, F32)
    run_ref[...] = jnp.zeros(run_ref.shape, F32)

    def block(start, seen):
        heads = range(SB_HEADS)
        runs = [run_ref[h] for h in heads]
        zs = [_dot_nt(qm_ref[h], kbf_ref[pl.ds(start, KEY_TILE), (h // 2) * LANES:(h // 2 + 1) * LANES])
              for h in heads]
        ws, tots = _sb_weights_many(zs, seen, tri, runs)
        pv = [_dot(ws[h], vbf_ref[pl.ds(start, KEY_TILE), (h // 2) * LANES:(h // 2 + 1) * LANES])
              for h in heads]
        for h in heads:
            run_ref[h] = runs[h] + tots[h]
        for pair in range(n_pairs):
            acc_ref[pair] += jnp.where(first_head, pv[2 * pair], pv[2 * pair + 1])

    last = pl.multiple_of((n_kb - 1) * KEY_TILE, KEY_TILE)
    q_pos = qi * Q_TILE + lax.broadcasted_iota(jnp.int32, (Q_TILE, KEY_TILE), 0)
    k_pos = last + lax.broadcasted_iota(jnp.int32, (Q_TILE, KEY_TILE), 1)
    block(last, k_pos < q_pos)

    def body(j, _):
        block(pl.multiple_of((n_kb - 1 - j) * KEY_TILE, KEY_TILE), None)
        return 0

    lax.fori_loop(1, n_kb, body, 0)
    for pair in range(n_pairs):
        o_ref[:, pair * LANES:(pair + 1) * LANES] = acc_ref[pair].astype(BF16)


def _sb_prompt_call(q, k, v, n_streams, seq):
    nq = seq // Q_TILE
    return pl.pallas_call(
        _sb_prompt_kernel,
        grid=(n_streams, nq),
        in_specs=[
            pl.BlockSpec((Q_TILE, SB_WIDTH), lambda b, i: (b * nq + i, 0)),
            pl.BlockSpec((seq, SB_WIDTH), lambda b, i: (b, 0)),
            pl.BlockSpec((seq, SB_WIDTH), lambda b, i: (b, 0)),
        ],
        out_specs=[
            pl.BlockSpec((Q_TILE, SB_WIDTH), lambda b, i: (b * nq + i, 0)),
            pl.BlockSpec((None, SB_WIDTH, seq), lambda b, i: (b, 0, 0)),
            pl.BlockSpec((None, SB_WIDTH, seq), lambda b, i: (b, 0, 0)),
        ],
        out_shape=[
            jax.ShapeDtypeStruct(q.shape, BF16),
            jax.ShapeDtypeStruct((n_streams, SB_WIDTH, seq), F32),
            jax.ShapeDtypeStruct((n_streams, SB_WIDTH, seq), F32),
        ],
        scratch_shapes=[
            pltpu.VMEM((seq, SB_WIDTH), BF16),
            pltpu.VMEM((seq, SB_WIDTH), BF16),
            pltpu.VMEM((SB_HEADS, Q_TILE, LANES), BF16),
            pltpu.VMEM((SB_HEADS // 2, Q_TILE, LANES), F32),
            pltpu.VMEM((SB_HEADS, Q_TILE, 1), F32),
        ],
        compiler_params=_params(("parallel", "arbitrary")),
        name="sb_prompt",
    )(q, k, v)


def _mla_prompt_kernel(q_ref, ckv_ref, kr_ref, wuk_ref, o_ref, cbf_ref, kcat_ref, m_ref, l_ref, acc_ref):
    qi = pl.program_id(1)

    @pl.when(qi == 0)
    def _():
        c = ckv_ref[...].astype(BF16)
        cbf_ref[...] = c
        kr = kr_ref[...]
        fill = jnp.zeros((kr.shape[0], LANES - MLA_NOPE_DIM - MLA_ROPE_DIM), F32)
        for h in range(MLA_HEADS):
            kcat_ref[h] = jnp.concatenate([_dot(c, wuk_ref[h]), kr, fill], axis=-1).astype(BF16)

    m_ref[...] = jnp.full(m_ref.shape, NEG_INF, F32)
    l_ref[...] = jnp.zeros(l_ref.shape, F32)
    acc_ref[...] = jnp.zeros(acc_ref.shape, F32)
    n_kb = (qi * Q_TILE + Q_TILE + KEY_TILE - 1) // KEY_TILE

    def block(start, visible):
        c = cbf_ref[pl.ds(start, KEY_TILE), :]
        heads = range(MLA_HEADS)
        m_old = [m_ref[h] for h in heads]
        l_old = [l_ref[h] for h in heads]
        ss = [_dot_nt(q_ref[h], kcat_ref[h, pl.ds(start, KEY_TILE), :]) for h in heads]
        if visible is not None:
            ss = [jnp.where(visible, s, NEG_INF) for s in ss]
        m_new = [jnp.maximum(m_old[h], jnp.max(jnp.maximum(ss[h][:, :LANES], ss[h][:, LANES:]),
                                               axis=1, keepdims=True)) for h in heads]
        p_lo = [jnp.exp(ss[h][:, :LANES] - m_new[h]) for h in heads]
        p_hi = [jnp.exp(ss[h][:, LANES:] - m_new[h]) for h in heads]
        alpha = [jnp.exp(m_old[h] - m_new[h]) for h in heads]
        pv = [_dot(jnp.concatenate([p_lo[h], p_hi[h]], axis=1).astype(BF16), c) for h in heads]
        for h in heads:
            m_ref[h] = m_new[h]
            l_ref[h] = alpha[h][:, :1] * l_old[h] + jnp.sum(p_lo[h] + p_hi[h], axis=1, keepdims=True)
        for h in heads:
            acc_ref[h] = jnp.concatenate([alpha[h], alpha[h]], axis=1) * acc_ref[h] + pv[h]

    def body(j, _):
        block(pl.multiple_of(j * KEY_TILE, KEY_TILE), None)
        return 0

    lax.fori_loop(0, n_kb - 1, body, 0)
    last = pl.multiple_of((n_kb - 1) * KEY_TILE, KEY_TILE)
    q_chunk = (qi * Q_TILE + lax.broadcasted_iota(jnp.int32, (Q_TILE, KEY_TILE), 0)) // CHUNK
    k_chunk = (last + lax.broadcasted_iota(jnp.int32, (Q_TILE, KEY_TILE), 1)) // CHUNK
    block(last, k_chunk <= q_chunk)
    for h in range(MLA_HEADS):
        o_ref[h] = (acc_ref[h] / l_ref[h]).astype(BF16)


def _mla_prompt_call(qcat, ckv, krope, w_uk, n_streams, seq):
    nq = seq // Q_TILE
    qmap = lambda b, i: (0, b * nq + i, 0)
    return pl.pallas_call(
        _mla_prompt_kernel,
        grid=(n_streams, nq),
        in_specs=[
            pl.BlockSpec((MLA_HEADS, Q_TILE, LANES), qmap),
            pl.BlockSpec((seq, MLA_KV_RANK), lambda b, i: (b, 0)),
            pl.BlockSpec((seq, MLA_ROPE_DIM), lambda b, i: (b, 0)),
            _const_spec((MLA_HEADS, MLA_KV_RANK, MLA_NOPE_DIM)),
        ],
        out_specs=pl.BlockSpec((MLA_HEADS, Q_TILE, MLA_KV_RANK), qmap),
        out_shape=jax.ShapeDtypeStruct((MLA_HEADS, qcat.shape[1], MLA_KV_RANK), BF16),
        scratch_shapes=[
            pltpu.VMEM((seq, MLA_KV_RANK), BF16),
            pltpu.VMEM((MLA_HEADS, seq, LANES), BF16),
            pltpu.VMEM((MLA_HEADS, Q_TILE, LANES), F32),
            pltpu.VMEM((MLA_HEADS, Q_TILE, 1), F32),
            pltpu.VMEM((MLA_HEADS, Q_TILE, MLA_KV_RANK), F32),
        ],
        compiler_params=_params(("parallel", "arbitrary")),
        name="mla_prompt",
    )(qcat, ckv, krope, w_uk)


def _sb_weights_chain(zs, tri, run):
    sps = [_softplus(z) for z in zs]
    laters = [_dot(sp.astype(BF16), tri) for sp in sps]
    ws = []
    for z, sp, later in zip(zs, sps, laters):
        ws.append(jnp.exp((z - sp) - (later + run)).astype(BF16))
        run = run + jnp.sum(sp, axis=1, keepdims=True)
    return ws, run


def _sb_decode_kernel(q_ref, kn_ref, vn_ref, kc_ref, vc_ref, o_ref, acc_ref, run_ref):
    kb = pl.program_id(1)
    n_q = q_ref.shape[0]
    rows = SB_HEADS * n_q
    heads = range(SB_HEADS)
    tri = _later_sum_matrix(KEY_TILE)
    q = q_ref[...]
    qh = [q[:, h * SB_HEAD_DIM:(h + 1) * SB_HEAD_DIM] for h in heads]

    def scores(keys_of):
        return jnp.concatenate([_dot_nt(qh[h], keys_of(h)) for h in heads], axis=0)

    def weighted(w, values_of):
        return jnp.concatenate([_dot(w[h * n_q:(h + 1) * n_q], values_of(h)) for h in heads], axis=0)

    @pl.when(kb == 0)
    def _():
        pad = jnp.zeros((LANES - n_q, SB_HEAD_DIM), BF16)
        kn = kn_ref[...].astype(BF16)
        vn = vn_ref[...].astype(BF16)
        head = lambda a, h: jnp.concatenate([a[:, h * SB_HEAD_DIM:(h + 1) * SB_HEAD_DIM], pad], axis=0)
        z = scores(lambda h: head(kn, h))
        i = lax.broadcasted_iota(jnp.int32, (rows, LANES), 0) % n_q
        j = lax.broadcasted_iota(jnp.int32, (rows, LANES), 1)
        w, tot = _sb_weights(z, j < i, tri[:LANES, :LANES], jnp.zeros((rows, 1), F32))
        acc_ref[...] = weighted(w, lambda h: head(vn, h))
        run_ref[...] = tot

    n_sub = kc_ref.shape[2] // KEY_TILE
    z = jnp.concatenate([_dot(qh[h], kc_ref[h].astype(BF16)) for h in heads], axis=0)
    subs = list(range(n_sub - 1, -1, -1))
    ws, run = _sb_weights_chain([z[:, s * KEY_TILE:(s + 1) * KEY_TILE] for s in subs], tri, run_ref[...])
    w = jnp.concatenate(ws[::-1], axis=1)
    acc_ref[...] += jnp.concatenate(
        [_dot_nt(w[h * n_q:(h + 1) * n_q], vc_ref[h].astype(BF16)) for h in heads], axis=0)
    run_ref[...] = run

    @pl.when(kb == pl.num_programs(1) - 1)
    def _():
        a = acc_ref[...]
        o_ref[...] = jnp.concatenate([a[h * n_q:(h + 1) * n_q] for h in heads], axis=1).astype(BF16)


def _sb_decode_call(q, k_new, v_new, cache_k, cache_v, layer, n_q):
    t = q.shape[0]
    n_streams = t // n_q
    past = cache_k.shape[4]
    n_kb = past // DECODE_KEY_BLOCK
    row = lambda s, j: (s, 0)
    cmap = lambda s, j: (layer, s, 0, 0, n_kb - 1 - j)
    rows = SB_HEADS * n_q
    cache_block = (None, None, SB_HEADS, SB_HEAD_DIM, DECODE_KEY_BLOCK)
    return pl.pallas_call(
        _sb_decode_kernel,
        grid=(n_streams, n_kb),
        in_specs=[
            pl.BlockSpec((n_q, SB_WIDTH), row),
            pl.BlockSpec((n_q, SB_WIDTH), row),
            pl.BlockSpec((n_q, SB_WIDTH), row),
            pl.BlockSpec(cache_block, cmap),
            pl.BlockSpec(cache_block, cmap),
        ],
        out_specs=pl.BlockSpec((n_q, SB_WIDTH), row),
        out_shape=jax.ShapeDtypeStruct(q.shape, BF16),
        scratch_shapes=[pltpu.VMEM((rows, SB_HEAD_DIM), F32), pltpu.VMEM((rows, 1), F32)],
        compiler_params=_params(("parallel", "arbitrary")),
        name="sb_decode",
    )(q, k_new, v_new, cache_k, cache_v)


def _mla_decode_kernel(q_ref, cn_ref, rn_ref, cc_ref, rc_ref, wukt_ref, o_ref, m_ref, l_ref, acc_ref,
                       *, past):
    kb = pl.program_id(1)
    n_q = q_ref.shape[1]
    rows = MLA_HEADS * n_q
    ql = jnp.concatenate([_dot(q_ref[h][:, :MLA_NOPE_DIM], wukt_ref[h]) for h in range(MLA_HEADS)],
                         axis=0).astype(BF16)
    qr = jnp.concatenate([q_ref[h][:, MLA_NOPE_DIM:MLA_NOPE_DIM + MLA_ROPE_DIM] for h in range(MLA_HEADS)],
                         axis=0)

    def update(s, c):
        m_old = m_ref[...]
        m_new = jnp.maximum(m_old, jnp.max(s, axis=1, keepdims=True))
        p = jnp.exp(s - m_new)
        alpha = jnp.exp(m_old - m_new)
        l_ref[...] = alpha * l_ref[...] + jnp.sum(p, axis=1, keepdims=True)
        acc_ref[...] = alpha * acc_ref[...] + _dot(p.astype(BF16), c)
        m_ref[...] = m_new

    @pl.when(kb == 0)
    def _():
        m_ref[...] = jnp.full((rows, 1), NEG_INF, F32)
        l_ref[...] = jnp.zeros((rows, 1), F32)
        acc_ref[...] = jnp.zeros((rows, MLA_KV_RANK), F32)
        c = jnp.concatenate([cn_ref[...].astype(BF16), jnp.zeros((LANES - n_q, MLA_KV_RANK), BF16)], axis=0)
        r = jnp.concatenate([rn_ref[...].astype(BF16), jnp.zeros((LANES - n_q, MLA_ROPE_DIM), BF16)], axis=0)
        s = _dot_nt(ql, c) + _dot_nt(qr, r)
        i = lax.broadcasted_iota(jnp.int32, (rows, LANES), 0) % n_q
        j = lax.broadcasted_iota(jnp.int32, (rows, LANES), 1)
        vis = jnp.logical_and(j < n_q, (past + j) // CHUNK <= (past + i) // CHUNK)
        update(jnp.where(vis, s, NEG_INF), c)

    c = cc_ref[...].astype(BF16)
    update(_dot_nt(ql, c) + _dot(qr, rc_ref[...].astype(BF16)), c)

    @pl.when(kb == pl.num_programs(1) - 1)
    def _():
        o = acc_ref[...] / l_ref[...]
        o_ref[...] = o.reshape(MLA_HEADS, n_q, MLA_KV_RANK).astype(BF16)


def _mla_decode_call(qcat, ckv_new, krope_new, cache_ckv, cache_krope, w_ukt, layer, n_q):
    t = qcat.shape[1]
    n_streams = t // n_q
    past = cache_ckv.shape[2]
    n_kb = past // DECODE_KEY_BLOCK
    rows = MLA_HEADS * n_q
    qmap = lambda s, j: (0, s, 0)
    row = lambda s, j: (s, 0)
    cmap = lambda s, j: (layer, s, j, 0)
    rmap = lambda s, j: (layer, s, 0, j)
    return pl.pallas_call(
        functools.partial(_mla_decode_kernel, past=past),
        grid=(n_streams, n_kb),
        in_specs=[
            pl.BlockSpec((MLA_HEADS, n_q, LANES), qmap),
            pl.BlockSpec((n_q, MLA_KV_RANK), row),
            pl.BlockSpec((n_q, MLA_ROPE_DIM), row),
            pl.BlockSpec((None, None, DECODE_KEY_BLOCK, MLA_KV_RANK), cmap),
            pl.BlockSpec((None, None, MLA_ROPE_DIM, DECODE_KEY_BLOCK), rmap),
            _const_spec((MLA_HEADS, MLA_NOPE_DIM, MLA_KV_RANK)),
        ],
        out_specs=pl.BlockSpec((MLA_HEADS, n_q, MLA_KV_RANK), qmap),
        out_shape=jax.ShapeDtypeStruct((MLA_HEADS, t, MLA_KV_RANK), BF16),
        scratch_shapes=[
            pltpu.VMEM((rows, 1), F32),
            pltpu.VMEM((rows, 1), F32),
            pltpu.VMEM((rows, MLA_KV_RANK), F32),
        ],
        compiler_params=_params(("parallel", "arbitrary")),
        name="mla_decode",
    )(qcat, ckv_new, krope_new, cache_ckv, cache_krope, w_ukt)


def _merge_kernel(x_ref, osb_ref, olat_ref, nm_ref, wg_ref, wuv_ref, wsbo_ref, wmlao_ref, wout_ref, h_ref):
    x = x_ref[...]
    xn = _rms(x, nm_ref[...]).astype(BF16)
    g = _dot_nt(xn, wg_ref[...])
    a = _dot(osb_ref[...], wsbo_ref[...])
    pieces = []
    for pair in range(MLA_HEADS // 2):
        pieces.append(_dot(olat_ref[2 * pair], wuv_ref[2 * pair])
                      + _dot(olat_ref[2 * pair + 1], wuv_ref[2 * pair + 1]))
    o_mla = jnp.concatenate(pieces, axis=-1).astype(BF16)
    b = _dot(o_mla, wmlao_ref[...])
    merged = jax.nn.sigmoid(g[:, :D_MODEL]) * a + jax.nn.sigmoid(g[:, D_MODEL:]) * b
    h_ref[...] = x + _dot(merged.astype(BF16), wout_ref[...])


def _merge_call(x, o_sb, o_lat, mw):
    t = x.shape[0]
    tm = TOKEN_TILE
    row = lambda i: (i, 0)
    return pl.pallas_call(
        _merge_kernel,
        grid=(t // tm,),
        in_specs=[
            pl.BlockSpec((tm, D_MODEL), row),
            pl.BlockSpec((tm, SB_WIDTH), row),
            pl.BlockSpec((MLA_HEADS, tm, MLA_KV_RANK), lambda i: (0, i, 0)),
            _const_spec((1, D_MODEL)),
            _const_spec((2 * D_MODEL, D_MODEL)),
            _const_spec((MLA_HEADS, MLA_KV_RANK, LANES)),
            _const_spec((SB_WIDTH, D_MODEL)),
            _const_spec((MLA_WIDTH, D_MODEL)),
            _const_spec((D_MODEL, D_MODEL)),
        ],
        out_specs=pl.BlockSpec((tm, D_MODEL), row),
        out_shape=jax.ShapeDtypeStruct((t, D_MODEL), F32),
        compiler_params=_params(("parallel",)),
        name="merge_out",
    )(x, o_sb, o_lat, mw["norm_mix"], mw["w_gates"], mw["w_uv"], mw["w_sb_o"], mw["w_mla_o"], mw["w_out"])


def _dense_ffn_kernel(x_ref, nf_ref, wg_ref, wu_ref, wd_ref, nfin_ref, o_ref, *, n_split, final_norm):
    x = x_ref[...]
    xn = _rms(x, nf_ref[...]).astype(BF16)
    fc = wg_ref.shape[1] // n_split
    acc = x
    for c in range(n_split):
        gt = _dot(xn, wg_ref[:, c * fc:(c + 1) * fc])
        up = _dot(xn, wu_ref[:, c * fc:(c + 1) * fc])
        hid = (gt * jax.nn.sigmoid(gt) * up).astype(BF16)
        acc = acc + _dot(hid, wd_ref[c * fc:(c + 1) * fc, :])
    o_ref[...] = _rms(acc, nfin_ref[...]) if final_norm else acc


def _dense_ffn_call(x, norm_ffn, wg, wu, wd, norm_final, final_norm):
    t = x.shape[0]
    tm = TOKEN_TILE
    d_ff = wg.shape[1]
    n_split = 2 if d_ff % (2 * LANES) == 0 else 1
    row = lambda i: (i, 0)
    return pl.pallas_call(
        functools.partial(_dense_ffn_kernel, n_split=n_split, final_norm=final_norm),
        grid=(t // tm,),
        in_specs=[
            pl.BlockSpec((tm, D_MODEL), row),
            _const_spec((1, D_MODEL)),
            _const_spec((D_MODEL, d_ff)),
            _const_spec((D_MODEL, d_ff)),
            _const_spec((d_ff, D_MODEL)),
            _const_spec((1, D_MODEL)),
        ],
        out_specs=pl.BlockSpec((tm, D_MODEL), row),
        out_shape=jax.ShapeDtypeStruct((t, D_MODEL), F32),
        compiler_params=_params(("parallel",)),
        name="dense_ffn",
    )(x, norm_ffn, wg, wu, wd, norm_final)


def _router_kernel(x_ref, nf_ref, rw_ref, rb_ref, xw_ref, e_ref, g_ref):
    xn = _rms(x_ref[...], nf_ref[...])
    xw_ref[...] = xn

    logits = jnp.dot(xn, rw_ref[...], preferred_element_type=F32, precision=lax.Precision.HIGHEST)
    logits = logits + rb_ref[...]
    lane = lax.broadcasted_iota(jnp.int32, logits.shape, 1).astype(F32)
    lg = jnp.where(lane < N_EXPERTS, logits, -jnp.inf)
    m1 = jnp.max(lg, axis=1, keepdims=True)
    i1 = jnp.min(jnp.where(lg == m1, lane, float(LANES)), axis=1, keepdims=True)
    lg2 = jnp.where(lane == i1, -jnp.inf, lg)
    m2 = jnp.max(lg2, axis=1, keepdims=True)
    i2 = jnp.min(jnp.where(lg2 == m2, lane, float(LANES)), axis=1, keepdims=True)
    e2 = jnp.exp(m2 - m1)
    den = 1.0 + e2
    e_ref[...] = jnp.concatenate([i1, i2], axis=1).astype(jnp.int32)
    g_ref[...] = jnp.concatenate([1.0 / den, e2 / den], axis=1)


def _router_call(x, norm_ffn, rw, rb):
    t = x.shape[0]
    tm = TOKEN_TILE
    row = lambda i: (i, 0)
    return pl.pallas_call(
        _router_kernel,
        grid=(t // tm,),
        in_specs=[
            pl.BlockSpec((tm, D_MODEL), row),
            _const_spec((1, D_MODEL)),
            _const_spec((D_MODEL, LANES)),
            _const_spec((1, LANES)),
        ],
        out_specs=[
            pl.BlockSpec((tm, D_MODEL), row),
            pl.BlockSpec((tm, TOP_K), row),
            pl.BlockSpec((tm, TOP_K), row),
        ],
        out_shape=[
            jax.ShapeDtypeStruct((t, D_MODEL), F32),
            jax.ShapeDtypeStruct((t, TOP_K), jnp.int32),
            jax.ShapeDtypeStruct((t, TOP_K), F32),
        ],
        compiler_params=_params(("parallel",)),
        name="router",
    )(x, norm_ffn, rw, rb)


def _row_copy(src, dst, s, d, sem):
    return pltpu.make_async_copy(src.at[pl.ds(s, 1), :], dst.at[pl.ds(d, 1), :], sem)


def _dispatch_kernel(dest_ref, xw_ref, xs_in_ref, xs_ref, sem):
    del xs_in_ref
    tm = xw_ref.shape[0]

    def start(t, _):
        for k in range(TOP_K):
            _row_copy(xw_ref, xs_ref, t, dest_ref[TOP_K * t + k], sem).start()
        return 0

    def wait(t, _):
        for k in range(TOP_K):
            _row_copy(xw_ref, xs_ref, 0, 0, sem).wait()
        return 0

    lax.fori_loop(0, tm, start, 0, unroll=8)
    lax.fori_loop(0, tm, wait, 0, unroll=8)


def _dispatch_call(dest_flat, xw, xs):
    t = xw.shape[0]
    tm = TOKEN_TILE
    return pl.pallas_call(
        _dispatch_kernel,
        grid=(t // tm,),
        in_specs=[
            pl.BlockSpec((tm * TOP_K,), lambda i: (i,), memory_space=pltpu.SMEM),
            pl.BlockSpec((tm, D_MODEL), lambda i: (i, 0)),
            pl.BlockSpec(memory_space=pl.ANY),
        ],
        out_specs=pl.BlockSpec(memory_space=pl.ANY),
        out_shape=jax.ShapeDtypeStruct(xs.shape, xs.dtype),
        scratch_shapes=[pltpu.SemaphoreType.DMA],
        input_output_aliases={2: 0},
        compiler_params=_params(("arbitrary",)),
        name="moe_dispatch",
    )(dest_flat, xw, xs)


def _expert_kernel(tile_e_ref, n_used_ref, xs_ref, wg_ref, wu_ref, wd_ref, y_ref, x_ref):
    r = pl.program_id(0)
    f = pl.program_id(1)

    @pl.when(f == 0)
    def _():
        x_ref[...] = xs_ref[...].astype(BF16)
        y_ref[...] = jnp.zeros_like(y_ref)

    @pl.when(r < n_used_ref[0])
    def _():
        x = x_ref[...]
        gt = _dot(x, wg_ref[...])
        up = _dot(x, wu_ref[...])
        hid = (gt * jax.nn.sigmoid(gt) * up).astype(BF16)
        y_ref[...] += _dot(hid, wd_ref[...])


def _expert_call(tile_e, n_used, xs, wg, wu, wd):
    n_rows = xs.shape[0]
    rt = EXPERT_ROW_TILE
    nf = EXPERT_F_SPLIT
    d_e = wg.shape[2]
    fc = d_e // nf

    def fidx(r, f):
        return jnp.where(r % 2 == 0, f, nf - 1 - f)

    grid_spec = pltpu.PrefetchScalarGridSpec(
        num_scalar_prefetch=2,
        grid=(n_rows // rt, nf),
        in_specs=[
            pl.BlockSpec((rt, D_MODEL), lambda r, f, te, nu: (r, 0)),
            pl.BlockSpec((None, D_MODEL, fc), lambda r, f, te, nu: (te[r], 0, fidx(r, f))),
            pl.BlockSpec((None, D_MODEL, fc), lambda r, f, te, nu: (te[r], 0, fidx(r, f))),
            pl.BlockSpec((None, fc, D_MODEL), lambda r, f, te, nu: (te[r], fidx(r, f), 0)),
        ],
        out_specs=pl.BlockSpec((rt, D_MODEL), lambda r, f, te, nu: (r, 0)),
        scratch_shapes=[pltpu.VMEM((rt, D_MODEL), BF16)],
    )
    return pl.pallas_call(
        _expert_kernel,
        grid_spec=grid_spec,
        out_shape=jax.ShapeDtypeStruct((n_rows, D_MODEL), F32),
        compiler_params=_params(("arbitrary", "arbitrary")),
        name="moe_experts",
    )(tile_e, n_used, xs, wg, wu, wd)


def _combine_kernel(dest_ref, h_ref, g_ref, y_ref, nfin_ref, o_ref, buf0, buf1, sem, *, final_norm):
    tm = h_ref.shape[0]

    def start(t, _):
        _row_copy(y_ref, buf0, dest_ref[TOP_K * t], t, sem).start()
        _row_copy(y_ref, buf1, dest_ref[TOP_K * t + 1], t, sem).start()
        return 0

    def wait(t, _):
        _row_copy(y_ref, buf0, 0, 0, sem).wait()
        _row_copy(y_ref, buf1, 0, 0, sem).wait()
        return 0

    lax.fori_loop(0, tm, start, 0, unroll=8)
    lax.fori_loop(0, tm, wait, 0, unroll=8)
    g = g_ref[...]
    y = h_ref[...] + (buf0[...] * g[:, 0:1] + buf1[...] * g[:, 1:2])
    o_ref[...] = _rms(y, nfin_ref[...]) if final_norm else y


def _combine_call(dest_flat, h, gate, yb, norm_final, final_norm):
    t = h.shape[0]
    tm = TOKEN_TILE
    row = lambda i: (i, 0)
    return pl.pallas_call(
        functools.partial(_combine_kernel, final_norm=final_norm),
        grid=(t // tm,),
        in_specs=[
            pl.BlockSpec((tm * TOP_K,), lambda i: (i,), memory_space=pltpu.SMEM),
            pl.BlockSpec((tm, D_MODEL), row),
            pl.BlockSpec((tm, TOP_K), row),
            pl.BlockSpec(memory_space=pl.ANY),
            _const_spec((1, D_MODEL)),
        ],
        out_specs=pl.BlockSpec((tm, D_MODEL), row),
        out_shape=jax.ShapeDtypeStruct((t, D_MODEL), F32),
        scratch_shapes=[pltpu.VMEM((tm, D_MODEL), F32), pltpu.VMEM((tm, D_MODEL), F32),
                        pltpu.SemaphoreType.DMA],
        compiler_params=_params(("arbitrary",)),
        name="moe_combine",
    )(dest_flat, h, gate, yb, norm_final)


def _moe_layout(top_e):
    rt = EXPERT_ROW_TILE
    e_flat = top_e.reshape(-1)
    n_assign = e_flat.shape[0]
    onehot = (e_flat[:, None] == jnp.arange(N_EXPERTS, dtype=jnp.int32)[None, :]).astype(jnp.int32)
    incl = jnp.cumsum(onehot, axis=0)
    counts = incl[-1]
    rank = jnp.sum((incl - onehot) * onehot, axis=1)
    padded = (counts + rt - 1) // rt * rt
    pad_end = jnp.cumsum(padded)
    pad_start = pad_end - padded
    dest = (jnp.sum(onehot * pad_start[None, :], axis=1) + rank).astype(jnp.int32)
    n_rows = -(-(n_assign + N_EXPERTS * (rt - 1)) // rt) * rt
    n_tiles = n_rows // rt
    tile_start = jnp.arange(n_tiles, dtype=jnp.int32) * rt
    tile_e = jnp.minimum(
        jnp.sum((pad_end[None, :] <= tile_start[:, None]).astype(jnp.int32), axis=1),
        N_EXPERTS - 1).astype(jnp.int32)
    n_used = (pad_end[-1] // rt).astype(jnp.int32).reshape(1)
    return dest, tile_e, n_used, n_rows


def _moe(streams, norm_ffn, rw, rb, wg, wu, wd, norm_final, final_norm):
    routed = [_router_call(h, norm_ffn, rw, rb) for h in streams]
    top_e = jnp.concatenate([r[1] for r in routed], axis=0)
    dest, tile_e, n_used, n_rows = _moe_layout(top_e)
    xs = jnp.zeros((n_rows, D_MODEL), F32)
    dests = []
    off = 0
    for h, (xw, _, _) in zip(streams, routed):
        n = h.shape[0] * TOP_K
        dests.append(dest[off:off + n])
        off += n
        xs = _dispatch_call(dests[-1], xw, xs)
    yb = _expert_call(tile_e, n_used, xs, wg, wu, wd)
    return [_combine_call(d, h, r[2], yb, norm_final, final_norm)
            for d, h, r in zip(dests, streams, routed)]


def _mixer_weights(l, norm_mix, w_in, mla_q_norm, w_uq, mla_kv_norm, w_uk, w_uv, w_sb_o, w_mla_o, w_out):
    w = w_in[l]
    o_gate = 3 * SB_WIDTH + MLA_Q_RANK + MLA_KV_RANK + MLA_ROPE_DIM
    w = w.T
    w1 = jnp.concatenate(
        [w[:o_gate], jnp.zeros((LANES - MLA_ROPE_DIM, D_MODEL), F32)], axis=0).astype(BF16)
    uq = w_uq[l].reshape(MLA_Q_RANK, MLA_HEADS, MLA_NOPE_DIM + MLA_ROPE_DIM)
    uq = jnp.concatenate([
        uq[:, :, :MLA_NOPE_DIM].reshape(MLA_Q_RANK, -1),
        uq[:, :, MLA_NOPE_DIM:MLA_NOPE_DIM + ROPE_HALF].reshape(MLA_Q_RANK, -1),
        uq[:, :, MLA_NOPE_DIM + ROPE_HALF:].reshape(MLA_Q_RANK, -1)], axis=1).astype(BF16)
    uv = jnp.transpose(w_uv[l], (1, 0, 2))
    zeros = jnp.zeros_like(uv)
    even = (jnp.arange(MLA_HEADS) % 2 == 0)[:, None, None]
    uv = jnp.concatenate([jnp.where(even, uv, zeros), jnp.where(even, zeros, uv)], axis=2).astype(BF16)
    return {
        "norm_mix": norm_mix[l].reshape(1, D_MODEL),
        "w1": w1,
        "w_gates": w[o_gate:].astype(BF16),
        "q_norm": mla_q_norm[l].reshape(1, MLA_Q_RANK),
        "w_uq": uq,
        "kv_norm": mla_kv_norm[l].reshape(1, MLA_KV_RANK),
        "w_ukt": jnp.transpose(w_uk[l], (1, 2, 0)).astype(BF16),
        "w_uk": jnp.transpose(w_uk[l], (1, 0, 2)).astype(BF16),
        "w_uv": uv,
        "w_sb_o": w_sb_o[l].astype(BF16),
        "w_mla_o": w_mla_o[l].astype(BF16),
        "w_out": w_out[l].astype(BF16),
    }


def _rope_tables(pos, rows):
    inv_freq = jnp.power(ROPE_BASE, -jnp.arange(ROPE_HALF, dtype=F32) / ROPE_HALF)
    ang = pos.astype(F32)[:, None] * inv_freq[None, :]
    reps = (rows // pos.shape[0], LANES // ROPE_HALF)
    return jnp.tile(jnp.cos(ang), reps), jnp.tile(jnp.sin(ang), reps)


def kernel(x_prompt, x_sample, cache_sb_k, cache_sb_v, cache_mla_ckv, cache_mla_krope, norm_mix, w_in,
           mla_q_norm, w_uq, mla_kv_norm, w_uk, w_uv, w_sb_o, w_mla_o, w_out, norm_ffn, ffn_w_gate,
           ffn_w_up, ffn_w_down, router_w, router_b, moe_w_gate, moe_w_up, moe_w_down, norm_final):
    n_p, seq, _ = x_prompt.shape
    n_s, n_q, _ = x_sample.shape
    depth = w_in.shape[0]
    past = cache_sb_k.shape[2]
    assert seq % TOKEN_TILE == 0 and (n_s * n_q) % TOKEN_TILE == 0 and TOKEN_TILE % n_q == 0
    assert seq % KEY_TILE == 0 and past % DECODE_KEY_BLOCK == 0 and n_q <= LANES

    cos_p, sin_p = _rope_tables(jnp.arange(seq, dtype=jnp.int32), seq)
    cos_s, sin_s = _rope_tables(past + jnp.arange(n_q, dtype=jnp.int32), TOKEN_TILE)
    nfin = norm_final.reshape(1, D_MODEL)
    cache_kt = jnp.transpose(cache_sb_k, (0, 1, 3, 4, 2))
    cache_vt = jnp.transpose(cache_sb_v, (0, 1, 3, 4, 2))
    cache_rt = jnp.transpose(cache_mla_krope, (0, 1, 3, 2))

    hp = x_prompt.reshape(n_p * seq, D_MODEL)
    hs = x_sample.reshape(n_s * n_q, D_MODEL)
    rows_p, rows_s = [], []
    for l in range(depth):
        mw = _mixer_weights(l, norm_mix, w_in, mla_q_norm, w_uq, mla_kv_norm, w_uk, w_uv,
                            w_sb_o, w_mla_o, w_out)
        sbq, sbk, sbv, ckv, krope, qcat = _proj_call(hp, mw, cos_p, sin_p)
        o_sb, sbk_t, sbv_t = _sb_prompt_call(sbq, sbk, sbv, n_p, seq)
        o_lat = _mla_prompt_call(qcat, ckv, krope, mw["w_uk"], n_p, seq)
        hp = _merge_call(hp, o_sb, o_lat, mw)
        rows_p.append((sbk_t, sbv_t, ckv, krope))

        sbq, sbk, sbv, ckv, krope, qcat = _proj_call(hs, mw, cos_s, sin_s)
        o_sb = _sb_decode_call(sbq, sbk, sbv, cache_kt, cache_vt, l, n_q)
        o_lat = _mla_decode_call(qcat, ckv, krope, cache_mla_ckv, cache_rt, mw["w_ukt"], l, n_q)
        hs = _merge_call(hs, o_sb, o_lat, mw)
        rows_s.append((sbk, sbv, ckv, krope))

        last = l == depth - 1
        nf = norm_ffn[l].reshape(1, D_MODEL)
        i = l // 2
        if l % 2 == 0:
            wg, wu, wd = ffn_w_gate[i].astype(BF16), ffn_w_up[i].astype(BF16), ffn_w_down[i].astype(BF16)
            hp = _dense_ffn_call(hp, nf, wg, wu, wd, nfin, last)
            hs = _dense_ffn_call(hs, nf, wg, wu, wd, nfin, last)
        else:
            rw = jnp.concatenate([router_w[i], jnp.zeros((D_MODEL, LANES - N_EXPERTS), F32)], axis=1)
            rb = jnp.concatenate([router_b[i], jnp.zeros((LANES - N_EXPERTS,), F32)]).reshape(1, LANES)
            hp, hs = _moe([hp, hs], nf, rw, rb, moe_w_gate[i].astype(BF16), moe_w_up[i].astype(BF16),
                          moe_w_down[i].astype(BF16), nfin, last)

    def stack(rows, idx, shape):
        return jnp.stack([r[idx] for r in rows]).reshape((depth,) + shape)

    def stack_t(rows, idx):
        a = stack(rows, idx, (n_p, SB_HEADS, SB_HEAD_DIM, seq))
        return jnp.transpose(a, (0, 1, 4, 2, 3))

    kv_s = (n_s, n_q, SB_HEADS, SB_HEAD_DIM)
    return (hp.reshape(n_p, seq, D_MODEL), hs.reshape(n_s, n_q, D_MODEL),
            stack_t(rows_p, 0), stack_t(rows_p, 1),
            stack(rows_p, 2, (n_p, seq, MLA_KV_RANK)), stack(rows_p, 3, (n_p, seq, MLA_ROPE_DIM)),
            stack(rows_s, 0, kv_s), stack(rows_s, 1, kv_s),
            stack(rows_s, 2, (n_s, n_q, MLA_KV_RANK)), stack(rows_s, 3, (n_s, n_q, MLA_ROPE_DIM)))
```

```python
import functools

import jax
import jax.numpy as jnp
from jax import lax
from jax.experimental import pallas as pl
from jax.experimental.pallas import tpu as pltpu

F32 = jnp.float32
BF16 = jnp.bfloat16

D_MODEL = 1024
CHUNK = 64
SB_HEADS = 8
SB_HEAD_DIM = 64
SB_WIDTH = SB_HEADS * SB_HEAD_DIM
MLA_HEADS = 8
MLA_NOPE_DIM = 64
MLA_ROPE_DIM = 32
MLA_V_DIM = 64
MLA_Q_RANK = 384
MLA_KV_RANK = 256
MLA_WIDTH = MLA_HEADS * MLA_V_DIM
MLA_SCALE = (MLA_NOPE_DIM + MLA_ROPE_DIM) ** -0.5
SB_SCALE = SB_HEAD_DIM ** -0.5
ROPE_BASE = 10000.0
ROPE_HALF = MLA_ROPE_DIM // 2
N_EXPERTS = 8
TOP_K = 2
RMS_EPS = 1e-6
NEG_INF = -1e30
NEG_LOG2_E = -1.4426950408889634

LANES = 128
MXU_DIM = 256
VMEM_LIMIT_BYTES = 56 * 1024 * 1024

TOKEN_TILE = 256
PROJ_TILE = 512
PROJ_ROW_GROUPS = 1
Q_TILE = 128
KEY_TILE = MXU_DIM
DECODE_KEY_BLOCK = 2048
EXPERT_ROW_TILE = 512
EXPERT_F_SPLIT = 2
IN_PROJ_COLS = 3 * SB_WIDTH + MLA_Q_RANK + MLA_KV_RANK + LANES


def _params(semantics):
    return pltpu.CompilerParams(dimension_semantics=semantics, vmem_limit_bytes=VMEM_LIMIT_BYTES)


def _rms(x, g):
    return x * lax.rsqrt(jnp.mean(x * x, axis=-1, keepdims=True) + RMS_EPS) * g


def _dot(a, b):
    return jnp.dot(a, b, preferred_element_type=F32)


def _dot_nt(a, b):
    return lax.dot_general(a, b, (((1,), (1,)), ((), ())), preferred_element_type=F32)


def _softplus(z):
    return jnp.maximum(z, 0.0) + jnp.log(1.0 + jnp.exp2(jnp.abs(z) * NEG_LOG2_E))


def _later_sum_matrix(n):
    r = lax.broadcasted_iota(jnp.int32, (n, n), 0)
    c = lax.broadcasted_iota(jnp.int32, (n, n), 1)
    return jnp.where(r > c, 1.0, 0.0).astype(BF16)


def _const_spec(shape):
    nd = len(shape)
    return pl.BlockSpec(shape, lambda *_: (0,) * nd)


def _proj_kernel(x_ref, nm_ref, w1_ref, qn_ref, wuq_ref, kvn_ref, cos_ref, sin_ref,
                 sbq_ref, sbk_ref, sbv_ref, ckv_ref, krope_ref, qcat_ref):
    group = x_ref.shape[0] // PROJ_ROW_GROUPS
    groups = [slice(i * group, (i + 1) * group) for i in range(PROJ_ROW_GROUPS)]
    ps = [_dot_nt(_rms(x_ref[r, :], nm_ref[...]).astype(BF16), w1_ref[...]) for r in groups]
    cqns = []
    for r, p in zip(groups, ps):
        o = 0
        sbq_ref[r, :] = (p[:, o:o + SB_WIDTH] * SB_SCALE).astype(BF16)
        o += SB_WIDTH
        sbk_ref[r, :] = p[:, o:o + SB_WIDTH]
        o += SB_WIDTH
        sbv_ref[r, :] = p[:, o:o + SB_WIDTH]
        o += SB_WIDTH
        cqns.append(_rms(p[:, o:o + MLA_Q_RANK], qn_ref[...]).astype(BF16))
        o += MLA_Q_RANK
        ckv_ref[r, :] = _rms(p[:, o:o + MLA_KV_RANK], kvn_ref[...])
        o += MLA_KV_RANK
        kr = p[:, o:o + LANES]
        c16 = cos_ref[r, :ROPE_HALF]
        s16 = sin_ref[r, :ROPE_HALF]
        k1 = kr[:, :ROPE_HALF]
        k2 = kr[:, ROPE_HALF:MLA_ROPE_DIM]
        krope_ref[r, :] = jnp.concatenate([k1 * c16 - k2 * s16, k2 * c16 + k1 * s16], axis=-1)

    qs = [_dot(cqn, wuq_ref[...]) for cqn in cqns]
    nope_w = MLA_HEADS * MLA_NOPE_DIM
    fill = jnp.zeros((group, LANES - MLA_NOPE_DIM - MLA_ROPE_DIM), F32)
    for r, q in zip(groups, qs):
        cos = cos_ref[r, :]
        sin = sin_ref[r, :]
        x1 = q[:, nope_w:nope_w + LANES]
        x2 = q[:, nope_w + LANES:nope_w + 2 * LANES]
        r1 = (x1 * cos - x2 * sin) * MLA_SCALE
        r2 = (x2 * cos + x1 * sin) * MLA_SCALE
        for h in range(MLA_HEADS):
            qcat_ref[h, r, :] = jnp.concatenate(
                [q[:, h * MLA_NOPE_DIM:(h + 1) * MLA_NOPE_DIM] * MLA_SCALE,
                 r1[:, h * ROPE_HALF:(h + 1) * ROPE_HALF], r2[:, h * ROPE_HALF:(h + 1) * ROPE_HALF], fill],
                axis=-1).astype(BF16)


def _proj_call(x, mw, cos_tab, sin_tab):
    t = x.shape[0]
    tm = PROJ_TILE
    n_pos_tiles = cos_tab.shape[0] // tm
    row = lambda i: (i, 0)
    pos = lambda i: (i % n_pos_tiles, 0)
    head_row = lambda i: (0, i, 0)
    return pl.pallas_call(
        _proj_kernel,
        grid=(t // tm,),
        in_specs=[
            pl.BlockSpec((tm, D_MODEL), row),
            _const_spec((1, D_MODEL)),
            _const_spec((IN_PROJ_COLS, D_MODEL)),
            _const_spec((1, MLA_Q_RANK)),
            _const_spec((MLA_Q_RANK, MLA_HEADS * (MLA_NOPE_DIM + MLA_ROPE_DIM))),
            _const_spec((1, MLA_KV_RANK)),
            pl.BlockSpec((tm, LANES), pos),
            pl.BlockSpec((tm, LANES), pos),
        ],
        out_specs=[
            pl.BlockSpec((tm, SB_WIDTH), row),
            pl.BlockSpec((tm, SB_WIDTH), row),
            pl.BlockSpec((tm, SB_WIDTH), row),
            pl.BlockSpec((tm, MLA_KV_RANK), row),
            pl.BlockSpec((tm, MLA_ROPE_DIM), row),
            pl.BlockSpec((MLA_HEADS, tm, LANES), head_row),
        ],
        out_shape=[
            jax.ShapeDtypeStruct((t, SB_WIDTH), BF16),
            jax.ShapeDtypeStruct((t, SB_WIDTH), F32),
            jax.ShapeDtypeStruct((t, SB_WIDTH), F32),
            jax.ShapeDtypeStruct((t, MLA_KV_RANK), F32),
            jax.ShapeDtypeStruct((t, MLA_ROPE_DIM), F32),
            jax.ShapeDtypeStruct((MLA_HEADS, t, LANES), BF16),
        ],
        compiler_params=_params(("parallel",)),
        name="in_proj",
    )(x, mw["norm_mix"], mw["w1"], mw["q_norm"], mw["w_uq"], mw["kv_norm"], cos_tab, sin_tab)


def _sb_weights_many(zs, seen, tri, runs):
    sps = [_softplus(z) for z in zs]
    ts = sps if seen is None else [jnp.where(seen, sp, 0.0) for sp in sps]
    laters = [_dot(t.astype(BF16), tri) for t in ts]
    ws = []
    for z, sp, later, run in zip(zs, sps, laters, runs):
        w = jnp.exp((z - sp) - (later + run))
        if seen is not None:
            w = jnp.where(seen, w, 0.0)
        ws.append(w.astype(BF16))
    return ws, [jnp.sum(t, axis=1, keepdims=True) for t in ts]


def _sb_weights(z, seen, tri, run):
    ws, tots = _sb_weights_many([z], seen, tri, [run])
    return ws[0], tots[0]


def _sb_prompt_kernel(q_ref, k_ref, v_ref, o_ref, kt_ref, vt_ref, kbf_ref, vbf_ref, qm_ref, acc_ref, run_ref):
    qi = pl.program_id(1)
    n_pairs = SB_HEADS // 2

    @pl.when(qi == 0)
    def _():
        k = k_ref[...]
        v = v_ref[...]
        kbf_ref[...] = k.astype(BF16)
        vbf_ref[...] = v.astype(BF16)
        kt_ref[...] = k.T
        vt_ref[...] = v.T

    tri = _later_sum_matrix(KEY_TILE)
    n_kb = (qi * Q_TILE + Q_TILE + KEY_TILE - 1) // KEY_TILE
    lane = lax.broadcasted_iota(jnp.int32, (Q_TILE, LANES), 1)
    first_head = lane < SB_HEAD_DIM
    for pair in range(n_pairs):
        q_pair = q_ref[:, pair * LANES:(pair + 1) * LANES]
        zero = jnp.zeros_like(q_pair)
        qm_ref[2 * pair] = jnp.where(first_head, q_pair, zero)
        qm_ref[2 * pair + 1] = jnp.where(first_head, zero, q_pair)
    acc_ref[...] = jnp.zeros(acc_ref.shape, F32)
    run_ref[...] = jnp.zeros(run_ref.shape, F32)

    def block(start, seen):
        heads = range(SB_HEADS)
        runs = [run_ref[h] for h in heads]
        zs = [_dot_nt(qm_ref[h], kbf_ref[pl.ds(start, KEY_TILE), (h // 2) * LANES:(h // 2 + 1) * LANES])
              for h in heads]
        ws, tots = _sb_weights_many(zs, seen, tri, runs)
        pv = [_dot(ws[h], vbf_ref[pl.ds(start, KEY_TILE), (h // 2) * LANES:(h // 2 + 1) * LANES])
              for h in heads]
        for h in heads:
            run_ref[h] = runs[h] + tots[h]
        for pair in range(n_pairs):
            acc_ref[pair] += jnp.where(first_head, pv[2 * pair], pv[2 * pair + 1])

    last = pl.multiple_of((n_kb - 1) * KEY_TILE, KEY_TILE)
    q_pos = qi * Q_TILE + lax.broadcasted_iota(jnp.int32, (Q_TILE, KEY_TILE), 0)
    k_pos = last + lax.broadcasted_iota(jnp.int32, (Q_TILE, KEY_TILE), 1)
    block(last, k_pos < q_pos)

    def body(j, _):
        block(pl.multiple_of((n_kb - 1 - j) * KEY_TILE, KEY_TILE), None)
        return 0

    lax.fori_loop(1, n_kb, body, 0)
    for pair in range(n_pairs):
        o_ref[:, pair * LANES:(pair + 1) * LANES] = acc_ref[pair].astype(BF16)


def _sb_prompt_call(q, k, v, n_streams, seq):
    nq = seq // Q_TILE
    return pl.pallas_call(
        _sb_prompt_kernel,
        grid=(n_streams, nq),
        in_specs=[
            pl.BlockSpec((Q_TILE, SB_WIDTH), lambda b, i: (b * nq + i, 0)),
            pl.BlockSpec((seq, SB_WIDTH), lambda b, i: (b, 0)),
            pl.BlockSpec((seq, SB_WIDTH), lambda b, i: (b, 0)),
        ],
        out_specs=[
            pl.BlockSpec((Q_TILE, SB_WIDTH), lambda b, i: (b * nq + i, 0)),
            pl.BlockSpec((None, SB_WIDTH, seq), lambda b, i: (b, 0, 0)),
            pl.BlockSpec((None, SB_WIDTH, seq), lambda b, i: (b, 0, 0)),
        ],
        out_shape=[
            jax.ShapeDtypeStruct(q.shape, BF16),
            jax.ShapeDtypeStruct((n_streams, SB_WIDTH, seq), F32),
            jax.ShapeDtypeStruct((n_streams, SB_WIDTH, seq), F32),
        ],
        scratch_shapes=[
            pltpu.VMEM((seq, SB_WIDTH), BF16),
            pltpu.VMEM((seq, SB_WIDTH), BF16),
            pltpu.VMEM((SB_HEADS, Q_TILE, LANES), BF16),
            pltpu.VMEM((SB_HEADS // 2, Q_TILE, LANES), F32),
            pltpu.VMEM((SB_HEADS, Q_TILE, 1), F32),
        ],
        compiler_params=_params(("parallel", "arbitrary")),
        name="sb_prompt",
    )(q, k, v)


def _mla_prompt_kernel(q_ref, ckv_ref, kr_ref, wuk_ref, o_ref, cbf_ref, kcat_ref, m_ref, l_ref, acc_ref):
    qi = pl.program_id(1)

    @pl.when(qi == 0)
    def _():
        c = ckv_ref[...].astype(BF16)
        cbf_ref[...] = c
        kr = kr_ref[...]
        fill = jnp.zeros((kr.shape[0], LANES - MLA_NOPE_DIM - MLA_ROPE_DIM), F32)
        for h in range(MLA_HEADS):
            kcat_ref[h] = jnp.concatenate([_dot(c, wuk_ref[h]), kr, fill], axis=-1).astype(BF16)

    m_ref[...] = jnp.full(m_ref.shape, NEG_INF, F32)
    l_ref[...] = jnp.zeros(l_ref.shape, F32)
    acc_ref[...] = jnp.zeros(acc_ref.shape, F32)
    n_kb = (qi * Q_TILE + Q_TILE + KEY_TILE - 1) // KEY_TILE

    def block(start, visible):
        c = cbf_ref[pl.ds(start, KEY_TILE), :]
        heads = range(MLA_HEADS)
        m_old = [m_ref[h] for h in heads]
        l_old = [l_ref[h] for h in heads]
        ss = [_dot_nt(q_ref[h], kcat_ref[h, pl.ds(start, KEY_TILE), :]) for h in heads]
        if visible is not None:
            ss = [jnp.where(visible, s, NEG_INF) for s in ss]
        m_new = [jnp.maximum(m_old[h], jnp.max(jnp.maximum(ss[h][:, :LANES], ss[h][:, LANES:]),
                                               axis=1, keepdims=True)) for h in heads]
        p_lo = [jnp.exp(ss[h][:, :LANES] - m_new[h]) for h in heads]
        p_hi = [jnp.exp(ss[h][:, LANES:] - m_new[h]) for h in heads]
        alpha = [jnp.exp(m_old[h] - m_new[h]) for h in heads]
        pv = [_dot(jnp.concatenate([p_lo[h], p_hi[h]], axis=1).astype(BF16), c) for h in heads]
        for h in heads:
            m_ref[h] = m_new[h]
            l_ref[h] = alpha[h][:, :1] * l_old[h] + jnp.sum(p_lo[h] + p_hi[h], axis=1, keepdims=True)
        for h in heads:
            acc_ref[h] = jnp.concatenate([alpha[h], alpha[h]], axis=1) * acc_ref[h] + pv[h]

    def body(j, _):
        block(pl.multiple_of(j * KEY_TILE, KEY_TILE), None)
        return 0

    lax.fori_loop(0, n_kb - 1, body, 0)
    last = pl.multiple_of((n_kb - 1) * KEY_TILE, KEY_TILE)
    q_chunk = (qi * Q_TILE + lax.broadcasted_iota(jnp.int32, (Q_TILE, KEY_TILE), 0)) // CHUNK
    k_chunk = (last + lax.broadcasted_iota(jnp.int32, (Q_TILE, KEY_TILE), 1)) // CHUNK
    block(last, k_chunk <= q_chunk)
    for h in range(MLA_HEADS):
        o_ref[h] = (acc_ref[h] / l_ref[h]).astype(BF16)


def _mla_prompt_call(qcat, ckv, krope, w_uk, n_streams, seq):
    nq = seq // Q_TILE
    qmap = lambda b, i: (0, b * nq + i, 0)
    return pl.pallas_call(
        _mla_prompt_kernel,
        grid=(n_streams, nq),
        in_specs=[
            pl.BlockSpec((MLA_HEADS, Q_TILE, LANES), qmap),
            pl.BlockSpec((seq, MLA_KV_RANK), lambda b, i: (b, 0)),
            pl.BlockSpec((seq, MLA_ROPE_DIM), lambda b, i: (b, 0)),
            _const_spec((MLA_HEADS, MLA_KV_RANK, MLA_NOPE_DIM)),
        ],
        out_specs=pl.BlockSpec((MLA_HEADS, Q_TILE, MLA_KV_RANK), qmap),
        out_shape=jax.ShapeDtypeStruct((MLA_HEADS, qcat.shape[1], MLA_KV_RANK), BF16),
        scratch_shapes=[
            pltpu.VMEM((seq, MLA_KV_RANK), BF16),
            pltpu.VMEM((MLA_HEADS, seq, LANES), BF16),
            pltpu.VMEM((MLA_HEADS, Q_TILE, LANES), F32),
            pltpu.VMEM((MLA_HEADS, Q_TILE, 1), F32),
            pltpu.VMEM((MLA_HEADS, Q_TILE, MLA_KV_RANK), F32),
        ],
        compiler_params=_params(("parallel", "arbitrary")),
        name="mla_prompt",
    )(qcat, ckv, krope, w_uk)


def _sb_weights_chain(zs, tri, run):
    sps = [_softplus(z) for z in zs]
    laters = [_dot(sp.astype(BF16), tri) for sp in sps]
    ws = []
    for z, sp, later in zip(zs, sps, laters):
        ws.append(jnp.exp((z - sp) - (later + run)).astype(BF16))
        run = run + jnp.sum(sp, axis=1, keepdims=True)
    return ws, run


def _sb_decode_kernel(q_ref, kn_ref, vn_ref, kc_ref, vc_ref, o_ref, acc_ref, run_ref):
    kb = pl.program_id(1)
    n_q = q_ref.shape[0]
    rows = SB_HEADS * n_q
    heads = range(SB_HEADS)
    tri = _later_sum_matrix(KEY_TILE)
    q = q_ref[...]
    qh = [q[:, h * SB_HEAD_DIM:(h + 1) * SB_HEAD_DIM] for h in heads]

    def scores(keys_of):
        return jnp.concatenate([_dot_nt(qh[h], keys_of(h)) for h in heads], axis=0)

    def weighted(w, values_of):
        return jnp.concatenate([_dot(w[h * n_q:(h + 1) * n_q], values_of(h)) for h in heads], axis=0)

    @pl.when(kb == 0)
    def _():
        pad = jnp.zeros((LANES - n_q, SB_HEAD_DIM), BF16)
        kn = kn_ref[...].astype(BF16)
        vn = vn_ref[...].astype(BF16)
        head = lambda a, h: jnp.concatenate([a[:, h * SB_HEAD_DIM:(h + 1) * SB_HEAD_DIM], pad], axis=0)
        z = scores(lambda h: head(kn, h))
        i = lax.broadcasted_iota(jnp.int32, (rows, LANES), 0) % n_q
        j = lax.broadcasted_iota(jnp.int32, (rows, LANES), 1)
        w, tot = _sb_weights(z, j < i, tri[:LANES, :LANES], jnp.zeros((rows, 1), F32))
        acc_ref[...] = weighted(w, lambda h: head(vn, h))
        run_ref[...] = tot

    n_sub = kc_ref.shape[2] // KEY_TILE
    z = jnp.concatenate([_dot(qh[h], kc_ref[h].astype(BF16)) for h in heads], axis=0)
    subs = list(range(n_sub - 1, -1, -1))
    ws, run = _sb_weights_chain([z[:, s * KEY_TILE:(s + 1) * KEY_TILE] for s in subs], tri, run_ref[...])
    w = jnp.concatenate(ws[::-1], axis=1)
    acc_ref[...] += jnp.concatenate(
        [_dot_nt(w[h * n_q:(h + 1) * n_q], vc_ref[h].astype(BF16)) for h in heads], axis=0)
    run_ref[...] = run

    @pl.when(kb == pl.num_programs(1) - 1)
    def _():
        a = acc_ref[...]
        o_ref[...] = jnp.concatenate([a[h * n_q:(h + 1) * n_q] for h in heads], axis=1).astype(BF16)


def _sb_decode_call(q, k_new, v_new, cache_k, cache_v, layer, n_q):
    t = q.shape[0]
    n_streams = t // n_q
    past = cache_k.shape[4]
    n_kb = past // DECODE_KEY_BLOCK
    row = lambda s, j: (s, 0)
    cmap = lambda s, j: (layer, s, 0, 0, n_kb - 1 - j)
    rows = SB_HEADS * n_q
    cache_block = (None, None, SB_HEADS, SB_HEAD_DIM, DECODE_KEY_BLOCK)
    return pl.pallas_call(
        _sb_decode_kernel,
        grid=(n_streams, n_kb),
        in_specs=[
            pl.BlockSpec((n_q, SB_WIDTH), row),
            pl.BlockSpec((n_q, SB_WIDTH), row),
            pl.BlockSpec((n_q, SB_WIDTH), row),
            pl.BlockSpec(cache_block, cmap),
            pl.BlockSpec(cache_block, cmap),
        ],
        out_specs=pl.BlockSpec((n_q, SB_WIDTH), row),
        out_shape=jax.ShapeDtypeStruct(q.shape, BF16),
        scratch_shapes=[pltpu.VMEM((rows, SB_HEAD_DIM), F32), pltpu.VMEM((rows, 1), F32)],
        compiler_params=_params(("parallel", "arbitrary")),
        name="sb_decode",
    )(q, k_new, v_new, cache_k, cache_v)


def _mla_decode_kernel(q_ref, cn_ref, rn_ref, cc_ref, rc_ref, wukt_ref, o_ref, m_ref, l_ref, acc_ref,
                       *, past):
    kb = pl.program_id(1)
    n_q = q_ref.shape[1]
    rows = MLA_HEADS * n_q
    ql = jnp.concatenate([_dot(q_ref[h][:, :MLA_NOPE_DIM], wukt_ref[h]) for h in range(MLA_HEADS)],
                         axis=0).astype(BF16)
    qr = jnp.concatenate([q_ref[h][:, MLA_NOPE_DIM:MLA_NOPE_DIM + MLA_ROPE_DIM] for h in range(MLA_HEADS)],
                         axis=0)

    def update(s, c):
        m_old = m_ref[...]
        m_new = jnp.maximum(m_old, jnp.max(s, axis=1, keepdims=True))
        p = jnp.exp(s - m_new)
        alpha = jnp.exp(m_old - m_new)
        l_ref[...] = alpha * l_ref[...] + jnp.sum(p, axis=1, keepdims=True)
        acc_ref[...] = alpha * acc_ref[...] + _dot(p.astype(BF16), c)
        m_ref[...] = m_new

    @pl.when(kb == 0)
    def _():
        m_ref[...] = jnp.full((rows, 1), NEG_INF, F32)
        l_ref[...] = jnp.zeros((rows, 1), F32)
        acc_ref[...] = jnp.zeros((rows, MLA_KV_RANK), F32)
        c = jnp.concatenate([cn_ref[...].astype(BF16), jnp.zeros((LANES - n_q, MLA_KV_RANK), BF16)], axis=0)
        r = jnp.concatenate([rn_ref[...].astype(BF16), jnp.zeros((LANES - n_q, MLA_ROPE_DIM), BF16)], axis=0)
        s = _dot_nt(ql, c) + _dot_nt(qr, r)
        i = lax.broadcasted_iota(jnp.int32, (rows, LANES), 0) % n_q
        j = lax.broadcasted_iota(jnp.int32, (rows, LANES), 1)
        vis = jnp.logical_and(j < n_q, (past + j) // CHUNK <= (past + i) // CHUNK)
        update(jnp.where(vis, s, NEG_INF), c)

    c = cc_ref[...].astype(BF16)
    update(_dot_nt(ql, c) + _dot(qr, rc_ref[...].astype(BF16)), c)

    @pl.when(kb == pl.num_programs(1) - 1)
    def _():
        o = acc_ref[...] / l_ref[...]
        o_ref[...] = o.reshape(MLA_HEADS, n_q, MLA_KV_RANK).astype(BF16)


def _mla_decode_call(qcat, ckv_new, krope_new, cache_ckv, cache_krope, w_ukt, layer, n_q):
    t = qcat.shape[1]
    n_streams = t // n_q
    past = cache_ckv.shape[2]
    n_kb = past // DECODE_KEY_BLOCK
    rows = MLA_HEADS * n_q
    qmap = lambda s, j: (0, s, 0)
    row = lambda s, j: (s, 0)
    cmap = lambda s, j: (layer, s, j, 0)
    rmap = lambda s, j: (layer, s, 0, j)
    return pl.pallas_call(
        functools.partial(_mla_decode_kernel, past=past),
        grid=(n_streams, n_kb),
        in_specs=[
            pl.BlockSpec((MLA_HEADS, n_q, LANES), qmap),
            pl.BlockSpec((n_q, MLA_KV_RANK), row),
            pl.BlockSpec((n_q, MLA_ROPE_DIM), row),
            pl.BlockSpec((None, None, DECODE_KEY_BLOCK, MLA_KV_RANK), cmap),
            pl.BlockSpec((None, None, MLA_ROPE_DIM, DECODE_KEY_BLOCK), rmap),
            _const_spec((MLA_HEADS, MLA_NOPE_DIM, MLA_KV_RANK)),
        ],
        out_specs=pl.BlockSpec((MLA_HEADS, n_q, MLA_KV_RANK), qmap),
        out_shape=jax.ShapeDtypeStruct((MLA_HEADS, t, MLA_KV_RANK), BF16),
        scratch_shapes=[
            pltpu.VMEM((rows, 1), F32),
            pltpu.VMEM((rows, 1), F32),
            pltpu.VMEM((rows, MLA_KV_RANK), F32),
        ],
        compiler_params=_params(("parallel", "arbitrary")),
        name="mla_decode",
    )(qcat, ckv_new, krope_new, cache_ckv, cache_krope, w_ukt)


def _merge_kernel(x_ref, osb_ref, olat_ref, nm_ref, wg_ref, wuv_ref, wsbo_ref, wmlao_ref, wout_ref, h_ref):
    x = x_ref[...]
    xn = _rms(x, nm_ref[...]).astype(BF16)
    g = _dot_nt(xn, wg_ref[...])
    a = _dot(osb_ref[...], wsbo_ref[...])
    pieces = []
    for pair in range(MLA_HEADS // 2):
        pieces.append(_dot(olat_ref[2 * pair], wuv_ref[2 * pair])
                      + _dot(olat_ref[2 * pair + 1], wuv_ref[2 * pair + 1]))
    o_mla = jnp.concatenate(pieces, axis=-1).astype(BF16)
    b = _dot(o_mla, wmlao_ref[...])
    merged = jax.nn.sigmoid(g[:, :D_MODEL]) * a + jax.nn.sigmoid(g[:, D_MODEL:]) * b
    h_ref[...] = x + _dot(merged.astype(BF16), wout_ref[...])


def _merge_call(x, o_sb, o_lat, mw):
    t = x.shape[0]
    tm = TOKEN_TILE
    row = lambda i: (i, 0)
    return pl.pallas_call(
        _merge_kernel,
        grid=(t // tm,),
        in_specs=[
            pl.BlockSpec((tm, D_MODEL), row),
            pl.BlockSpec((tm, SB_WIDTH), row),
            pl.BlockSpec((MLA_HEADS, tm, MLA_KV_RANK), lambda i: (0, i, 0)),
            _const_spec((1, D_MODEL)),
            _const_spec((2 * D_MODEL, D_MODEL)),
            _const_spec((MLA_HEADS, MLA_KV_RANK, LANES)),
            _const_spec((SB_WIDTH, D_MODEL)),
            _const_spec((MLA_WIDTH, D_MODEL)),
            _const_spec((D_MODEL, D_MODEL)),
        ],
        out_specs=pl.BlockSpec((tm, D_MODEL), row),
        out_shape=jax.ShapeDtypeStruct((t, D_MODEL), F32),
        compiler_params=_params(("parallel",)),
        name="merge_out",
    )(x, o_sb, o_lat, mw["norm_mix"], mw["w_gates"], mw["w_uv"], mw["w_sb_o"], mw["w_mla_o"], mw["w_out"])


def _dense_ffn_kernel(x_ref, nf_ref, wg_ref, wu_ref, wd_ref, nfin_ref, o_ref, *, n_split, final_norm):
    x = x_ref[...]
    xn = _rms(x, nf_ref[...]).astype(BF16)
    fc = wg_ref.shape[1] // n_split
    acc = x
    for c in range(n_split):
        gt = _dot(xn, wg_ref[:, c * fc:(c + 1) * fc])
        up = _dot(xn, wu_ref[:, c * fc:(c + 1) * fc])
        hid = (gt * jax.nn.sigmoid(gt) * up).astype(BF16)
        acc = acc + _dot(hid, wd_ref[c * fc:(c + 1) * fc, :])
    o_ref[...] = _rms(acc, nfin_ref[...]) if final_norm else acc


def _dense_ffn_call(x, norm_ffn, wg, wu, wd, norm_final, final_norm):
    t = x.shape[0]
    tm = TOKEN_TILE
    d_ff = wg.shape[1]
    n_split = 2 if d_ff % (2 * LANES) == 0 else 1
    row = lambda i: (i, 0)
    return pl.pallas_call(
        functools.partial(_dense_ffn_kernel, n_split=n_split, final_norm=final_norm),
        grid=(t // tm,),
        in_specs=[
            pl.BlockSpec((tm, D_MODEL), row),
            _const_spec((1, D_MODEL)),
            _const_spec((D_MODEL, d_ff)),
            _const_spec((D_MODEL, d_ff)),
            _const_spec((d_ff, D_MODEL)),
            _const_spec((1, D_MODEL)),
        ],
        out_specs=pl.BlockSpec((tm, D_MODEL), row),
        out_shape=jax.ShapeDtypeStruct((t, D_MODEL), F32),
        compiler_params=_params(("parallel",)),
        name="dense_ffn",
    )(x, norm_ffn, wg, wu, wd, norm_final)


def _router_kernel(x_ref, nf_ref, rw_ref, rb_ref, xw_ref, e_ref, g_ref):
    xn = _rms(x_ref[...], nf_ref[...])
    xw_ref[...] = xn

    logits = jnp.dot(xn, rw_ref[...], preferred_element_type=F32, precision=lax.Precision.HIGHEST)
    logits = logits + rb_ref[...]
    lane = lax.broadcasted_iota(jnp.int32, logits.shape, 1).astype(F32)
    lg = jnp.where(lane < N_EXPERTS, logits, -jnp.inf)
    m1 = jnp.max(lg, axis=1, keepdims=True)
    i1 = jnp.min(jnp.where(lg == m1, lane, float(LANES)), axis=1, keepdims=True)
    lg2 = jnp.where(lane == i1, -jnp.inf, lg)
    m2 = jnp.max(lg2, axis=1, keepdims=True)
    i2 = jnp.min(jnp.where(lg2 == m2, lane, float(LANES)), axis=1, keepdims=True)
    e2 = jnp.exp(m2 - m1)
    den = 1.0 + e2
    e_ref[...] = jnp.concatenate([i1, i2], axis=1).astype(jnp.int32)
    g_ref[...] = jnp.concatenate([1.0 / den, e2 / den], axis=1)


def _router_call(x, norm_ffn, rw, rb):
    t = x.shape[0]
    tm = TOKEN_TILE
    row = lambda i: (i, 0)
    return pl.pallas_call(
        _router_kernel,
        grid=(t // tm,),
        in_specs=[
            pl.BlockSpec((tm, D_MODEL), row),
            _const_spec((1, D_MODEL)),
            _const_spec((D_MODEL, LANES)),
            _const_spec((1, LANES)),
        ],
        out_specs=[
            pl.BlockSpec((tm, D_MODEL), row),
            pl.BlockSpec((tm, TOP_K), row),
            pl.BlockSpec((tm, TOP_K), row),
        ],
        out_shape=[
            jax.ShapeDtypeStruct((t, D_MODEL), F32),
            jax.ShapeDtypeStruct((t, TOP_K), jnp.int32),
            jax.ShapeDtypeStruct((t, TOP_K), F32),
        ],
        compiler_params=_params(("parallel",)),
        name="router",
    )(x, norm_ffn, rw, rb)


def _row_copy(src, dst, s, d, sem):
    return pltpu.make_async_copy(src.at[pl.ds(s, 1), :], dst.at[pl.ds(d, 1), :], sem)


def _dispatch_kernel(dest_ref, xw_ref, xs_in_ref, xs_ref, sem):
    del xs_in_ref
    tm = xw_ref.shape[0]

    def start(t, _):
        for k in range(TOP_K):
            _row_copy(xw_ref, xs_ref, t, dest_ref[TOP_K * t + k], sem).start()
        return 0

    def wait(t, _):
        for k in range(TOP_K):
            _row_copy(xw_ref, xs_ref, 0, 0, sem).wait()
        return 0

    lax.fori_loop(0, tm, start, 0, unroll=8)
    lax.fori_loop(0, tm, wait, 0, unroll=8)


def _dispatch_call(dest_flat, xw, xs):
    t = xw.shape[0]
    tm = TOKEN_TILE
    return pl.pallas_call(
        _dispatch_kernel,
        grid=(t // tm,),
        in_specs=[
            pl.BlockSpec((tm * TOP_K,), lambda i: (i,), memory_space=pltpu.SMEM),
            pl.BlockSpec((tm, D_MODEL), lambda i: (i, 0)),
            pl.BlockSpec(memory_space=pl.ANY),
        ],
        out_specs=pl.BlockSpec(memory_space=pl.ANY),
        out_shape=jax.ShapeDtypeStruct(xs.shape, xs.dtype),
        scratch_shapes=[pltpu.SemaphoreType.DMA],
        input_output_aliases={2: 0},
        compiler_params=_params(("arbitrary",)),
        name="moe_dispatch",
    )(dest_flat, xw, xs)


def _expert_kernel(tile_e_ref, n_used_ref, xs_ref, wg_ref, wu_ref, wd_ref, y_ref, x_ref):
    r = pl.program_id(0)
    f = pl.program_id(1)

    @pl.when(f == 0)
    def _():
        x_ref[...] = xs_ref[...].astype(BF16)
        y_ref[...] = jnp.zeros_like(y_ref)

    @pl.when(r < n_used_ref[0])
    def _():
        x = x_ref[...]
        gt = _dot(x, wg_ref[...])
        up = _dot(x, wu_ref[...])
        hid = (gt * jax.nn.sigmoid(gt) * up).astype(BF16)
        y_ref[...] += _dot(hid, wd_ref[...])


def _expert_call(tile_e, n_used, xs, wg, wu, wd):
    n_rows = xs.shape[0]
    rt = EXPERT_ROW_TILE
    nf = EXPERT_F_SPLIT
    d_e = wg.shape[2]
    fc = d_e // nf

    def fidx(r, f):
        return jnp.where(r % 2 == 0, f, nf - 1 - f)

    grid_spec = pltpu.PrefetchScalarGridSpec(
        num_scalar_prefetch=2,
        grid=(n_rows // rt, nf),
        in_specs=[
            pl.BlockSpec((rt, D_MODEL), lambda r, f, te, nu: (r, 0)),
            pl.BlockSpec((None, D_MODEL, fc), lambda r, f, te, nu: (te[r], 0, fidx(r, f))),
            pl.BlockSpec((None, D_MODEL, fc), lambda r, f, te, nu: (te[r], 0, fidx(r, f))),
            pl.BlockSpec((None, fc, D_MODEL), lambda r, f, te, nu: (te[r], fidx(r, f), 0)),
        ],
        out_specs=pl.BlockSpec((rt, D_MODEL), lambda r, f, te, nu: (r, 0)),
        scratch_shapes=[pltpu.VMEM((rt, D_MODEL), BF16)],
    )
    return pl.pallas_call(
        _expert_kernel,
        grid_spec=grid_spec,
        out_shape=jax.ShapeDtypeStruct((n_rows, D_MODEL), F32),
        compiler_params=_params(("arbitrary", "arbitrary")),
        name="moe_experts",
    )(tile_e, n_used, xs, wg, wu, wd)


def _combine_kernel(dest_ref, h_ref, g_ref, y_ref, nfin_ref, o_ref, buf0, buf1, sem, *, final_norm):
    tm = h_ref.shape[0]

    def start(t, _):
        _row_copy(y_ref, buf0, dest_ref[TOP_K * t], t, sem).start()
        _row_copy(y_ref, buf1, dest_ref[TOP_K * t + 1], t, sem).start()
        return 0

    def wait(t, _):
        _row_copy(y_ref, buf0, 0, 0, sem).wait()
        _row_copy(y_ref, buf1, 0, 0, sem).wait()
        return 0

    lax.fori_loop(0, tm, start, 0, unroll=8)
    lax.fori_loop(0, tm, wait, 0, unroll=8)
    g = g_ref[...]
    y = h_ref[...] + (buf0[...] * g[:, 0:1] + buf1[...] * g[:, 1:2])
    o_ref[...] = _rms(y, nfin_ref[...]) if final_norm else y


def _combine_call(dest_flat, h, gate, yb, norm_final, final_norm):
    t = h.shape[0]
    tm = TOKEN_TILE
    row = lambda i: (i, 0)
    return pl.pallas_call(
        functools.partial(_combine_kernel, final_norm=final_norm),
        grid=(t // tm,),
        in_specs=[
            pl.BlockSpec((tm * TOP_K,), lambda i: (i,), memory_space=pltpu.SMEM),
            pl.BlockSpec((tm, D_MODEL), row),
            pl.BlockSpec((tm, TOP_K), row),
            pl.BlockSpec(memory_space=pl.ANY),
            _const_spec((1, D_MODEL)),
        ],
        out_specs=pl.BlockSpec((tm, D_MODEL), row),
        out_shape=jax.ShapeDtypeStruct((t, D_MODEL), F32),
        scratch_shapes=[pltpu.VMEM((tm, D_MODEL), F32), pltpu.VMEM((tm, D_MODEL), F32),
                        pltpu.SemaphoreType.DMA],
        compiler_params=_params(("arbitrary",)),
        name="moe_combine",
    )(dest_flat, h, gate, yb, norm_final)


def _moe_layout(top_e):
    rt = EXPERT_ROW_TILE
    e_flat = top_e.reshape(-1)
    n_assign = e_flat.shape[0]
    onehot = (e_flat[:, None] == jnp.arange(N_EXPERTS, dtype=jnp.int32)[None, :]).astype(jnp.int32)
    incl = jnp.cumsum(onehot, axis=0)
    counts = incl[-1]
    rank = jnp.sum((incl - onehot) * onehot, axis=1)
    padded = (counts + rt - 1) // rt * rt
    pad_end = jnp.cumsum(padded)
    pad_start = pad_end - padded
    dest = (jnp.sum(onehot * pad_start[None, :], axis=1) + rank).astype(jnp.int32)
    n_rows = -(-(n_assign + N_EXPERTS * (rt - 1)) // rt) * rt
    n_tiles = n_rows // rt
    tile_start = jnp.arange(n_tiles, dtype=jnp.int32) * rt
    tile_e = jnp.minimum(
        jnp.sum((pad_end[None, :] <= tile_start[:, None]).astype(jnp.int32), axis=1),
        N_EXPERTS - 1).astype(jnp.int32)
    n_used = (pad_end[-1] // rt).astype(jnp.int32).reshape(1)
    return dest, tile_e, n_used, n_rows


def _moe(streams, norm_ffn, rw, rb, wg, wu, wd, norm_final, final_norm):
    routed = [_router_call(h, norm_ffn, rw, rb) for h in streams]
    top_e = jnp.concatenate([r[1] for r in routed], axis=0)
    dest, tile_e, n_used, n_rows = _moe_layout(top_e)
    xs = jnp.zeros((n_rows, D_MODEL), F32)
    dests = []
    off = 0
    for h, (xw, _, _) in zip(streams, routed):
        n = h.shape[0] * TOP_K
        dests.append(dest[off:off + n])
        off += n
        xs = _dispatch_call(dests[-1], xw, xs)
    yb = _expert_call(tile_e, n_used, xs, wg, wu, wd)
    return [_combine_call(d, h, r[2], yb, norm_final, final_norm)
            for d, h, r in zip(dests, streams, routed)]


def _mixer_weights(l, norm_mix, w_in, mla_q_norm, w_uq, mla_kv_norm, w_uk, w_uv, w_sb_o, w_mla_o, w_out):
    w = w_in[l]
    o_gate = 3 * SB_WIDTH + MLA_Q_RANK + MLA_KV_RANK + MLA_ROPE_DIM
    w = w.T
    w1 = jnp.concatenate(
        [w[:o_gate], jnp.zeros((LANES - MLA_ROPE_DIM, D_MODEL), F32)], axis=0).astype(BF16)
    uq = w_uq[l].reshape(MLA_Q_RANK, MLA_HEADS, MLA_NOPE_DIM + MLA_ROPE_DIM)
    uq = jnp.concatenate([
        uq[:, :, :MLA_NOPE_DIM].reshape(MLA_Q_RANK, -1),
        uq[:, :, MLA_NOPE_DIM:MLA_NOPE_DIM + ROPE_HALF].reshape(MLA_Q_RANK, -1),
        uq[:, :, MLA_NOPE_DIM + ROPE_HALF:].reshape(MLA_Q_RANK, -1)], axis=1).astype(BF16)
    uv = jnp.transpose(w_uv[l], (1, 0, 2))
    zeros = jnp.zeros_like(uv)
    even = (jnp.arange(MLA_HEADS) % 2 == 0)[:, None, None]
    uv = jnp.concatenate([jnp.where(even, uv, zeros), jnp.where(even, zeros, uv)], axis=2).astype(BF16)
    return {
        "norm_mix": norm_mix[l].reshape(1, D_MODEL),
        "w1": w1,
        "w_gates": w[o_gate:].astype(BF16),
        "q_norm": mla_q_norm[l].reshape(1, MLA_Q_RANK),
        "w_uq": uq,
        "kv_norm": mla_kv_norm[l].reshape(1, MLA_KV_RANK),
        "w_ukt": jnp.transpose(w_uk[l], (1, 2, 0)).astype(BF16),
        "w_uk": jnp.transpose(w_uk[l], (1, 0, 2)).astype(BF16),
        "w_uv": uv,
        "w_sb_o": w_sb_o[l].astype(BF16),
        "w_mla_o": w_mla_o[l].astype(BF16),
        "w_out": w_out[l].astype(BF16),
    }


def _rope_tables(pos, rows):
    inv_freq = jnp.power(ROPE_BASE, -jnp.arange(ROPE_HALF, dtype=F32) / ROPE_HALF)
    ang = pos.astype(F32)[:, None] * inv_freq[None, :]
    reps = (rows // pos.shape[0], LANES // ROPE_HALF)
    return jnp.tile(jnp.cos(ang), reps), jnp.tile(jnp.sin(ang), reps)


def kernel(x_prompt, x_sample, cache_sb_k, cache_sb_v, cache_mla_ckv, cache_mla_krope, norm_mix, w_in,
           mla_q_norm, w_uq, mla_kv_norm, w_uk, w_uv, w_sb_o, w_mla_o, w_out, norm_ffn, ffn_w_gate,
           ffn_w_up, ffn_w_down, router_w, router_b, moe_w_gate, moe_w_up, moe_w_down, norm_final):
    n_p, seq, _ = x_prompt.shape
    n_s, n_q, _ = x_sample.shape
    depth = w_in.shape[0]
    past = cache_sb_k.shape[2]
    assert seq % PROJ_TILE == 0 and (n_s * n_q) % PROJ_TILE == 0 and PROJ_TILE % n_q == 0
    assert PROJ_TILE % TOKEN_TILE == 0
    assert seq % KEY_TILE == 0 and past % DECODE_KEY_BLOCK == 0 and n_q <= LANES

    cos_p, sin_p = _rope_tables(jnp.arange(seq, dtype=jnp.int32), seq)
    cos_s, sin_s = _rope_tables(past + jnp.arange(n_q, dtype=jnp.int32), PROJ_TILE)
    nfin = norm_final.reshape(1, D_MODEL)
    cache_kt = jnp.transpose(cache_sb_k, (0, 1, 3, 4, 2))
    cache_vt = jnp.transpose(cache_sb_v, (0, 1, 3, 4, 2))
    cache_rt = jnp.transpose(cache_mla_krope, (0, 1, 3, 2))

    hp = x_prompt.reshape(n_p * seq, D_MODEL)
    hs = x_sample.reshape(n_s * n_q, D_MODEL)
    rows_p, rows_s = [], []
    for l in range(depth):
        mw = _mixer_weights(l, norm_mix, w_in, mla_q_norm, w_uq, mla_kv_norm, w_uk, w_uv,
                            w_sb_o, w_mla_o, w_out)
        sbq, sbk, sbv, ckv, krope, qcat = _proj_call(hp, mw, cos_p, sin_p)
        o_sb, sbk_t, sbv_t = _sb_prompt_call(sbq, sbk, sbv, n_p, seq)
        o_lat = _mla_prompt_call(qcat, ckv, krope, mw["w_uk"], n_p, seq)
        hp = _merge_call(hp, o_sb, o_lat, mw)
        rows_p.append((sbk_t, sbv_t, ckv, krope))

        sbq, sbk, sbv, ckv, krope, qcat = _proj_call(hs, mw, cos_s, sin_s)
        o_sb = _sb_decode_call(sbq, sbk, sbv, cache_kt, cache_vt, l, n_q)
        o_lat = _mla_decode_call(qcat, ckv, krope, cache_mla_ckv, cache_rt, mw["w_ukt"], l, n_q)
        hs = _merge_call(hs, o_sb, o_lat, mw)
        rows_s.append((sbk, sbv, ckv, krope))

        last = l == depth - 1
        nf = norm_ffn[l].reshape(1, D_MODEL)
        i = l // 2
        if l % 2 == 0:
            wg, wu, wd = ffn_w_gate[i].astype(BF16), ffn_w_up[i].astype(BF16), ffn_w_down[i].astype(BF16)
            hp = _dense_ffn_call(hp, nf, wg, wu, wd, nfin, last)
            hs = _dense_ffn_call(hs, nf, wg, wu, wd, nfin, last)
        else:
            rw = jnp.concatenate([router_w[i], jnp.zeros((D_MODEL, LANES - N_EXPERTS), F32)], axis=1)
            rb = jnp.concatenate([router_b[i], jnp.zeros((LANES - N_EXPERTS,), F32)]).reshape(1, LANES)
            hp, hs = _moe([hp, hs], nf, rw, rb, moe_w_gate[i].astype(BF16), moe_w_up[i].astype(BF16),
                          moe_w_down[i].astype(BF16), nfin, last)

    def stack(rows, idx, shape):
        return jnp.stack([r[idx] for r in rows]).reshape((depth,) + shape)

    def stack_t(rows, idx):
        a = stack(rows, idx, (n_p, SB_HEADS, SB_HEAD_DIM, seq))
        return jnp.transpose(a, (0, 1, 4, 2, 3))

    kv_s = (n_s, n_q, SB_HEADS, SB_HEAD_DIM)
    return (hp.reshape(n_p, seq, D_MODEL), hs.reshape(n_s, n_q, D_MODEL),
            stack_t(rows_p, 0), stack_t(rows_p, 1),
            stack(rows_p, 2, (n_p, seq, MLA_KV_RANK)), stack(rows_p, 3, (n_p, seq, MLA_ROPE_DIM)),
            stack(rows_s, 0, kv_s), stack(rows_s, 1, kv_s),
            stack(rows_s, 2, (n_s, n_q, MLA_KV_RANK)), stack(rows_s, 3, (n_s, n_q, MLA_ROPE_DIM)))
```

```python
import functools

import jax
import jax.numpy as jnp
from jax import lax
from jax.experimental import pallas as pl
from jax.experimental.pallas import tpu as pltpu

F32 = jnp.float32
BF16 = jnp.bfloat16

D_MODEL = 1024
CHUNK = 64
SB_HEADS = 8
SB_HEAD_DIM = 64
SB_WIDTH = SB_HEADS * SB_HEAD_DIM
MLA_HEADS = 8
MLA_NOPE_DIM = 64
MLA_ROPE_DIM = 32
MLA_V_DIM = 64
MLA_Q_RANK = 384
MLA_KV_RANK = 256
MLA_WIDTH = MLA_HEADS * MLA_V_DIM
MLA_SCALE = (MLA_NOPE_DIM + MLA_ROPE_DIM) ** -0.5
SB_SCALE = SB_HEAD_DIM ** -0.5
ROPE_BASE = 10000.0
ROPE_HALF = MLA_ROPE_DIM // 2
N_EXPERTS = 8
TOP_K = 2
RMS_EPS = 1e-6
NEG_INF = -1e30
NEG_LOG2_E = -1.4426950408889634

LANES = 128
MXU_DIM = 256
VMEM_LIMIT_BYTES = 56 * 1024 * 1024

TOKEN_TILE = 256
PROJ_TILE = 512
PROJ_ROW_GROUPS = 1
Q_TILE = 128
KEY_TILE = MXU_DIM
DECODE_KEY_BLOCK = 2048
EXPERT_ROW_TILE = 512
EXPERT_F_SPLIT = 2
IN_PROJ_COLS = 3 * SB_WIDTH + MLA_Q_RANK + MLA_KV_RANK + LANES


def _params(semantics):
    return pltpu.CompilerParams(dimension_semantics=semantics, vmem_limit_bytes=VMEM_LIMIT_BYTES)


def _rms(x, g):
    return x * lax.rsqrt(jnp.mean(x * x, axis=-1, keepdims=True) + RMS_EPS) * g


def _dot(a, b):
    return jnp.dot(a, b, preferred_element_type=F32)


def _dot_nt(a, b):
    return lax.dot_general(a, b, (((1,), (1,)), ((), ())), preferred_element_type=F32)


def _softplus(z):
    return jnp.maximum(z, 0.0) + jnp.log(1.0 + jnp.exp2(jnp.abs(z) * NEG_LOG2_E))


def _later_sum_matrix(n):
    r = lax.broadcasted_iota(jnp.int32, (n, n), 0)
    c = lax.broadcasted_iota(jnp.int32, (n, n), 1)
    return jnp.where(r > c, 1.0, 0.0).astype(BF16)


def _const_spec(shape):
    nd = len(shape)
    return pl.BlockSpec(shape, lambda *_: (0,) * nd)


def _proj_kernel(x_ref, nm_ref, w1_ref, qn_ref, wuq_ref, kvn_ref, cos_ref, sin_ref,
                 sbq_ref, sbk_ref, sbv_ref, ckv_ref, krope_ref, qcat_ref):
    group = x_ref.shape[0] // PROJ_ROW_GROUPS
    groups = [slice(i * group, (i + 1) * group) for i in range(PROJ_ROW_GROUPS)]
    ps = [_dot_nt(_rms(x_ref[r, :], nm_ref[...]).astype(BF16), w1_ref[...]) for r in groups]
    cqns = []
    for r, p in zip(groups, ps):
        o = 0
        sbq_ref[r, :] = (p[:, o:o + SB_WIDTH] * SB_SCALE).astype(BF16)
        o += SB_WIDTH
        sbk_ref[r, :] = p[:, o:o + SB_WIDTH]
        o += SB_WIDTH
        sbv_ref[r, :] = p[:, o:o + SB_WIDTH]
        o += SB_WIDTH
        cqns.append(_rms(p[:, o:o + MLA_Q_RANK], qn_ref[...]).astype(BF16))
        o += MLA_Q_RANK
        ckv_ref[r, :] = _rms(p[:, o:o + MLA_KV_RANK], kvn_ref[...])
        o += MLA_KV_RANK
        kr = p[:, o:o + LANES]
        c16 = cos_ref[r, :ROPE_HALF]
        s16 = sin_ref[r, :ROPE_HALF]
        k1 = kr[:, :ROPE_HALF]
        k2 = kr[:, ROPE_HALF:MLA_ROPE_DIM]
        krope_ref[r, :] = jnp.concatenate([k1 * c16 - k2 * s16, k2 * c16 + k1 * s16], axis=-1)

    qs = [_dot(cqn, wuq_ref[...]) for cqn in cqns]
    nope_w = MLA_HEADS * MLA_NOPE_DIM
    fill = jnp.zeros((group, LANES - MLA_NOPE_DIM - MLA_ROPE_DIM), F32)
    for r, q in zip(groups, qs):
        cos = cos_ref[r, :]
        sin = sin_ref[r, :]
        x1 = q[:, nope_w:nope_w + LANES]
        x2 = q[:, nope_w + LANES:nope_w + 2 * LANES]
        r1 = (x1 * cos - x2 * sin) * MLA_SCALE
        r2 = (x2 * cos + x1 * sin) * MLA_SCALE
        for h in range(MLA_HEADS):
            qcat_ref[h, r, :] = jnp.concatenate(
                [q[:, h * MLA_NOPE_DIM:(h + 1) * MLA_NOPE_DIM] * MLA_SCALE,
                 r1[:, h * ROPE_HALF:(h + 1) * ROPE_HALF], r2[:, h * ROPE_HALF:(h + 1) * ROPE_HALF], fill],
                axis=-1).astype(BF16)


def _proj_call(x, mw, cos_tab, sin_tab):
    t = x.shape[0]
    tm = PROJ_TILE
    n_pos_tiles = cos_tab.shape[0] // tm
    row = lambda i: (i, 0)
    pos = lambda i: (i % n_pos_tiles, 0)
    head_row = lambda i: (0, i, 0)
    return pl.pallas_call(
        _proj_kernel,
        grid=(t // tm,),
        in_specs=[
            pl.BlockSpec((tm, D_MODEL), row),
            _const_spec((1, D_MODEL)),
            _const_spec((IN_PROJ_COLS, D_MODEL)),
            _const_spec((1, MLA_Q_RANK)),
            _const_spec((MLA_Q_RANK, MLA_HEADS * (MLA_NOPE_DIM + MLA_ROPE_DIM))),
            _const_spec((1, MLA_KV_RANK)),
            pl.BlockSpec((tm, LANES), pos),
            pl.BlockSpec((tm, LANES), pos),
        ],
        out_specs=[
            pl.BlockSpec((tm, SB_WIDTH), row),
            pl.BlockSpec((tm, SB_WIDTH), row),
            pl.BlockSpec((tm, SB_WIDTH), row),
            pl.BlockSpec((tm, MLA_KV_RANK), row),
            pl.BlockSpec((tm, MLA_ROPE_DIM), row),
            pl.BlockSpec((MLA_HEADS, tm, LANES), head_row),
        ],
        out_shape=[
            jax.ShapeDtypeStruct((t, SB_WIDTH), BF16),
            jax.ShapeDtypeStruct((t, SB_WIDTH), F32),
            jax.ShapeDtypeStruct((t, SB_WIDTH), F32),
            jax.ShapeDtypeStruct((t, MLA_KV_RANK), F32),
            jax.ShapeDtypeStruct((t, MLA_ROPE_DIM), F32),
            jax.ShapeDtypeStruct((MLA_HEADS, t, LANES), BF16),
        ],
        compiler_params=_params(("parallel",)),
        name="in_proj",
    )(x, mw["norm_mix"], mw["w1"], mw["q_norm"], mw["w_uq"], mw["kv_norm"], cos_tab, sin_tab)


def _sb_weights_many(zs, seen, tri, runs):
    sps = [_softplus(z) for z in zs]
    ts = sps if seen is None else [jnp.where(seen, sp, 0.0) for sp in sps]
    laters = [_dot(t.astype(BF16), tri) for t in ts]
    ws = []
    for z, sp, later, run in zip(zs, sps, laters, runs):
        w = jnp.exp((z - sp) - (later + run))
        if seen is not None:
            w = jnp.where(seen, w, 0.0)
        ws.append(w.astype(BF16))
    return ws, [jnp.sum(t, axis=1, keepdims=True) for t in ts]


def _sb_weights(z, seen, tri, run):
    ws, tots = _sb_weights_many([z], seen, tri, [run])
    return ws[0], tots[0]


def _sb_prompt_kernel(q_ref, k_ref, v_ref, o_ref, kt_ref, vt_ref, kbf_ref, vbf_ref, qm_ref, acc_ref, run_ref):
    qi = pl.program_id(1)
    n_pairs = SB_HEADS // 2

    @pl.when(qi == 0)
    def _():
        k = k_ref[...]
        v = v_ref[...]
        kbf_ref[...] = k.astype(BF16)
        vbf_ref[...] = v.astype(BF16)
        kt_ref[...] = k.T
        vt_ref[...] = v.T

    tri = _later_sum_matrix(KEY_TILE)
    n_kb = (qi * Q_TILE + Q_TILE + KEY_TILE - 1) // KEY_TILE
    lane = lax.broadcasted_iota(jnp.int32, (Q_TILE, LANES), 1)
    first_head = lane < SB_HEAD_DIM
    for pair in range(n_pairs):
        q_pair = q_ref[:, pair * LANES:(pair + 1) * LANES]
        zero = jnp.zeros_like(q_pair)
        qm_ref[2 * pair] = jnp.where(first_head, q_pair, zero)
        qm_ref[2 * pair + 1] = jnp.where(first_head, zero, q_pair)
    acc_ref[...] = jnp.zeros(acc_ref.shape, F32)
    run_ref[...] = jnp.zeros(run_ref.shape, F32)

    def block(start, seen):
        heads = range(SB_HEADS)
        runs = [run_ref[h] for h in heads]
        zs = [_dot_nt(qm_ref[h], kbf_ref[pl.ds(start, KEY_TILE), (h // 2) * LANES:(h // 2 + 1) * LANES])
              for h in heads]
        ws, tots = _sb_weights_many(zs, seen, tri, runs)
        pv = [_dot(ws[h], vbf_ref[pl.ds(start, KEY_TILE), (h // 2) * LANES:(h // 2 + 1) * LANES])
              for h in heads]
        for h in heads:
            run_ref[h] = runs[h] + tots[h]
        for pair in range(n_pairs):
            acc_ref[pair] += jnp.where(first_head, pv[2 * pair], pv[2 * pair + 1])

    last = pl.multiple_of((n_kb - 1) * KEY_TILE, KEY_TILE)
    q_pos = qi * Q_TILE + lax.broadcasted_iota(jnp.int32, (Q_TILE, KEY_TILE), 0)
    k_pos = last + lax.broadcasted_iota(jnp.int32, (Q_TILE, KEY_TILE), 1)
    block(last, k_pos < q_pos)

    def body(j, _):
        block(pl.multiple_of((n_kb - 1 - j) * KEY_TILE, KEY_TILE), None)
        return 0

    lax.fori_loop(1, n_kb, body, 0)
    for pair in range(n_pairs):
        o_ref[:, pair * LANES:(pair + 1) * LANES] = acc_ref[pair].astype(BF16)


def _sb_prompt_call(q, k, v, n_streams, seq):
    nq = seq // Q_TILE
    return pl.pallas_call(
        _sb_prompt_kernel,
        grid=(n_streams, nq),
        in_specs=[
            pl.BlockSpec((Q_TILE, SB_WIDTH), lambda b, i: (b * nq + i, 0)),
            pl.BlockSpec((seq, SB_WIDTH), lambda b, i: (b, 0)),
            pl.BlockSpec((seq, SB_WIDTH), lambda b, i: (b, 0)),
        ],
        out_specs=[
            pl.BlockSpec((Q_TILE, SB_WIDTH), lambda b, i: (b * nq + i, 0)),
            pl.BlockSpec((None, SB_WIDTH, seq), lambda b, i: (b, 0, 0)),
            pl.BlockSpec((None, SB_WIDTH, seq), lambda b, i: (b, 0, 0)),
        ],
        out_shape=[
            jax.ShapeDtypeStruct(q.shape, BF16),
            jax.ShapeDtypeStruct((n_streams, SB_WIDTH, seq), F32),
            jax.ShapeDtypeStruct((n_streams, SB_WIDTH, seq), F32),
        ],
        scratch_shapes=[
            pltpu.VMEM((seq, SB_WIDTH), BF16),
            pltpu.VMEM((seq, SB_WIDTH), BF16),
            pltpu.VMEM((SB_HEADS, Q_TILE, LANES), BF16),
            pltpu.VMEM((SB_HEADS // 2, Q_TILE, LANES), F32),
            pltpu.VMEM((SB_HEADS, Q_TILE, 1), F32),
        ],
        compiler_params=_params(("parallel", "arbitrary")),
        name="sb_prompt",
    )(q, k, v)


def _mla_prompt_kernel(q_ref, ckv_ref, kr_ref, wuk_ref, o_ref, cbf_ref, kcat_ref, m_ref, l_ref, acc_ref):
    qi = pl.program_id(1)

    @pl.when(qi == 0)
    def _():
        c = ckv_ref[...].astype(BF16)
        cbf_ref[...] = c
        kr = kr_ref[...]
        fill = jnp.zeros((kr.shape[0], LANES - MLA_NOPE_DIM - MLA_ROPE_DIM), F32)
        for h in range(MLA_HEADS):
            kcat_ref[h] = jnp.concatenate([_dot(c, wuk_ref[h]), kr, fill], axis=-1).astype(BF16)

    m_ref[...] = jnp.full(m_ref.shape, NEG_INF, F32)
    l_ref[...] = jnp.zeros(l_ref.shape, F32)
    acc_ref[...] = jnp.zeros(acc_ref.shape, F32)
    n_kb = (qi * Q_TILE + Q_TILE + KEY_TILE - 1) // KEY_TILE

    def block(start, visible):
        c = cbf_ref[pl.ds(start, KEY_TILE), :]
        heads = range(MLA_HEADS)
        m_old = [m_ref[h] for h in heads]
        l_old = [l_ref[h] for h in heads]
        ss = [_dot_nt(q_ref[h], kcat_ref[h, pl.ds(start, KEY_TILE), :]) for h in heads]
        if visible is not None:
            ss = [jnp.where(visible, s, NEG_INF) for s in ss]
        m_new = [jnp.maximum(m_old[h], jnp.max(jnp.maximum(ss[h][:, :LANES], ss[h][:, LANES:]),
                                               axis=1, keepdims=True)) for h in heads]
        p_lo = [jnp.exp(ss[h][:, :LANES] - m_new[h]) for h in heads]
        p_hi = [jnp.exp(ss[h][:, LANES:] - m_new[h]) for h in heads]
        alpha = [jnp.exp(m_old[h] - m_new[h]) for h in heads]
        pv = [_dot(jnp.concatenate([p_lo[h], p_hi[h]], axis=1).astype(BF16), c) for h in heads]
        for h in heads:
            m_ref[h] = m_new[h]
            l_ref[h] = alpha[h][:, :1] * l_old[h] + jnp.sum(p_lo[h] + p_hi[h], axis=1, keepdims=True)
        for h in heads:
            acc_ref[h] = jnp.concatenate([alpha[h], alpha[h]], axis=1) * acc_ref[h] + pv[h]

    def body(j, _):
        block(pl.multiple_of(j * KEY_TILE, KEY_TILE), None)
        return 0

    lax.fori_loop(0, n_kb - 1, body, 0)
    last = pl.multiple_of((n_kb - 1) * KEY_TILE, KEY_TILE)
    q_chunk = (qi * Q_TILE + lax.broadcasted_iota(jnp.int32, (Q_TILE, KEY_TILE), 0)) // CHUNK
    k_chunk = (last + lax.broadcasted_iota(jnp.int32, (Q_TILE, KEY_TILE), 1)) // CHUNK
    block(last, k_chunk <= q_chunk)
    for h in range(MLA_HEADS):
        o_ref[h] = (acc_ref[h] / l_ref[h]).astype(BF16)


def _mla_prompt_call(qcat, ckv, krope, w_uk, n_streams, seq):
    nq = seq // Q_TILE
    qmap = lambda b, i: (0, b * nq + i, 0)
    return pl.pallas_call(
        _mla_prompt_kernel,
        grid=(n_streams, nq),
        in_specs=[
            pl.BlockSpec((MLA_HEADS, Q_TILE, LANES), qmap),
            pl.BlockSpec((seq, MLA_KV_RANK), lambda b, i: (b, 0)),
            pl.BlockSpec((seq, MLA_ROPE_DIM), lambda b, i: (b, 0)),
            _const_spec((MLA_HEADS, MLA_KV_RANK, MLA_NOPE_DIM)),
        ],
        out_specs=pl.BlockSpec((MLA_HEADS, Q_TILE, MLA_KV_RANK), qmap),
        out_shape=jax.ShapeDtypeStruct((MLA_HEADS, qcat.shape[1], MLA_KV_RANK), BF16),
        scratch_shapes=[
            pltpu.VMEM((seq, MLA_KV_RANK), BF16),
            pltpu.VMEM((MLA_HEADS, seq, LANES), BF16),
            pltpu.VMEM((MLA_HEADS, Q_TILE, LANES), F32),
            pltpu.VMEM((MLA_HEADS, Q_TILE, 1), F32),
            pltpu.VMEM((MLA_HEADS, Q_TILE, MLA_KV_RANK), F32),
        ],
        compiler_params=_params(("parallel", "arbitrary")),
        name="mla_prompt",
    )(qcat, ckv, krope, w_uk)


def _sb_weights_chain(zs, tri, run):
    sps = [_softplus(z) for z in zs]
    laters = [_dot(sp.astype(BF16), tri) for sp in sps]
    ws = []
    for z, sp, later in zip(zs, sps, laters):
        ws.append(jnp.exp((z - sp) - (later + run)).astype(BF16))
        run = run + jnp.sum(sp, axis=1, keepdims=True)
    return ws, run


def _sb_decode_kernel(q_ref, kn_ref, vn_ref, kc_ref, vc_ref, o_ref, acc_ref, run_ref):
    kb = pl.program_id(1)
    n_q = q_ref.shape[0]
    rows = SB_HEADS * n_q
    heads = range(SB_HEADS)
    tri = _later_sum_matrix(KEY_TILE)
    q = q_ref[...]
    qh = [q[:, h * SB_HEAD_DIM:(h + 1) * SB_HEAD_DIM] for h in heads]

    def scores(keys_of):
        return jnp.concatenate([_dot_nt(qh[h], keys_of(h)) for h in heads], axis=0)

    def weighted(w, values_of):
        return jnp.concatenate([_dot(w[h * n_q:(h + 1) * n_q], values_of(h)) for h in heads], axis=0)

    @pl.when(kb == 0)
    def _():
        pad = jnp.zeros((LANES - n_q, SB_HEAD_DIM), BF16)
        kn = kn_ref[...].astype(BF16)
        vn = vn_ref[...].astype(BF16)
        head = lambda a, h: jnp.concatenate([a[:, h * SB_HEAD_DIM:(h + 1) * SB_HEAD_DIM], pad], axis=0)
        z = scores(lambda h: head(kn, h))
        i = lax.broadcasted_iota(jnp.int32, (rows, LANES), 0) % n_q
        j = lax.broadcasted_iota(jnp.int32, (rows, LANES), 1)
        w, tot = _sb_weights(z, j < i, tri[:LANES, :LANES], jnp.zeros((rows, 1), F32))
        acc_ref[...] = weighted(w, lambda h: head(vn, h))
        run_ref[...] = tot

    n_sub = kc_ref.shape[2] // KEY_TILE
    z = jnp.concatenate([_dot(qh[h], kc_ref[h].astype(BF16)) for h in heads], axis=0)
    subs = list(range(n_sub - 1, -1, -1))
    ws, run = _sb_weights_chain([z[:, s * KEY_TILE:(s + 1) * KEY_TILE] for s in subs], tri, run_ref[...])
    w = jnp.concatenate(ws[::-1], axis=1)
    acc_ref[...] += jnp.concatenate(
        [_dot_nt(w[h * n_q:(h + 1) * n_q], vc_ref[h].astype(BF16)) for h in heads], axis=0)
    run_ref[...] = run

    @pl.when(kb == pl.num_programs(1) - 1)
    def _():
        a = acc_ref[...]
        o_ref[...] = jnp.concatenate([a[h * n_q:(h + 1) * n_q] for h in heads], axis=1).astype(BF16)


def _sb_decode_call(q, k_new, v_new, cache_k, cache_v, layer, n_q):
    t = q.shape[0]
    n_streams = t // n_q
    past = cache_k.shape[4]
    n_kb = past // DECODE_KEY_BLOCK
    row = lambda s, j: (s, 0)
    cmap = lambda s, j: (layer, s, 0, 0, n_kb - 1 - j)
    rows = SB_HEADS * n_q
    cache_block = (None, None, SB_HEADS, SB_HEAD_DIM, DECODE_KEY_BLOCK)
    return pl.pallas_call(
        _sb_decode_kernel,
        grid=(n_streams, n_kb),
        in_specs=[
            pl.BlockSpec((n_q, SB_WIDTH), row),
            pl.BlockSpec((n_q, SB_WIDTH), row),
            pl.BlockSpec((n_q, SB_WIDTH), row),
            pl.BlockSpec(cache_block, cmap),
            pl.BlockSpec(cache_block, cmap),
        ],
        out_specs=pl.BlockSpec((n_q, SB_WIDTH), row),
        out_shape=jax.ShapeDtypeStruct(q.shape, BF16),
        scratch_shapes=[pltpu.VMEM((rows, SB_HEAD_DIM), F32), pltpu.VMEM((rows, 1), F32)],
        compiler_params=_params(("parallel", "arbitrary")),
        name="sb_decode",
    )(q, k_new, v_new, cache_k, cache_v)


def _mla_decode_kernel(q_ref, cn_ref, rn_ref, cc_ref, rc_ref, wukt_ref, o_ref, m_ref, l_ref, acc_ref,
                       *, past):
    kb = pl.program_id(1)
    n_q = q_ref.shape[1]
    rows = MLA_HEADS * n_q
    ql = jnp.concatenate([_dot(q_ref[h][:, :MLA_NOPE_DIM], wukt_ref[h]) for h in range(MLA_HEADS)],
                         axis=0).astype(BF16)
    qr = jnp.concatenate([q_ref[h][:, MLA_NOPE_DIM:MLA_NOPE_DIM + MLA_ROPE_DIM] for h in range(MLA_HEADS)],
                         axis=0)

    def update(s, c):
        m_old = m_ref[...]
        m_new = jnp.maximum(m_old, jnp.max(s, axis=1, keepdims=True))
        p = jnp.exp(s - m_new)
        alpha = jnp.exp(m_old - m_new)
        l_ref[...] = alpha * l_ref[...] + jnp.sum(p, axis=1, keepdims=True)
        acc_ref[...] = alpha * acc_ref[...] + _dot(p.astype(BF16), c)
        m_ref[...] = m_new

    @pl.when(kb == 0)
    def _():
        m_ref[...] = jnp.full((rows, 1), NEG_INF, F32)
        l_ref[...] = jnp.zeros((rows, 1), F32)
        acc_ref[...] = jnp.zeros((rows, MLA_KV_RANK), F32)
        c = jnp.concatenate([cn_ref[...].astype(BF16), jnp.zeros((LANES - n_q, MLA_KV_RANK), BF16)], axis=0)
        r = jnp.concatenate([rn_ref[...].astype(BF16), jnp.zeros((LANES - n_q, MLA_ROPE_DIM), BF16)], axis=0)
        s = _dot_nt(ql, c) + _dot_nt(qr, r)
        i = lax.broadcasted_iota(jnp.int32, (rows, LANES), 0) % n_q
        j = lax.broadcasted_iota(jnp.int32, (rows, LANES), 1)
        vis = jnp.logical_and(j < n_q, (past + j) // CHUNK <= (past + i) // CHUNK)
        update(jnp.where(vis, s, NEG_INF), c)

    c = cc_ref[...].astype(BF16)
    update(_dot_nt(ql, c) + _dot(qr, rc_ref[...].astype(BF16)), c)

    @pl.when(kb == pl.num_programs(1) - 1)
    def _():
        o = acc_ref[...] / l_ref[...]
        o_ref[...] = o.reshape(MLA_HEADS, n_q, MLA_KV_RANK).astype(BF16)


def _mla_decode_call(qcat, ckv_new, krope_new, cache_ckv, cache_krope, w_ukt, layer, n_q):
    t = qcat.shape[1]
    n_streams = t // n_q
    past = cache_ckv.shape[2]
    n_kb = past // DECODE_KEY_BLOCK
    rows = MLA_HEADS * n_q
    qmap = lambda s, j: (0, s, 0)
    row = lambda s, j: (s, 0)
    cmap = lambda s, j: (layer, s, j, 0)
    rmap = lambda s, j: (layer, s, 0, j)
    return pl.pallas_call(
        functools.partial(_mla_decode_kernel, past=past),
        grid=(n_streams, n_kb),
        in_specs=[
            pl.BlockSpec((MLA_HEADS, n_q, LANES), qmap),
            pl.BlockSpec((n_q, MLA_KV_RANK), row),
            pl.BlockSpec((n_q, MLA_ROPE_DIM), row),
            pl.BlockSpec((None, None, DECODE_KEY_BLOCK, MLA_KV_RANK), cmap),
            pl.BlockSpec((None, None, MLA_ROPE_DIM, DECODE_KEY_BLOCK), rmap),
            _const_spec((MLA_HEADS, MLA_NOPE_DIM, MLA_KV_RANK)),
        ],
        out_specs=pl.BlockSpec((MLA_HEADS, n_q, MLA_KV_RANK), qmap),
        out_shape=jax.ShapeDtypeStruct((MLA_HEADS, t, MLA_KV_RANK), BF16),
        scratch_shapes=[
            pltpu.VMEM((rows, 1), F32),
            pltpu.VMEM((rows, 1), F32),
            pltpu.VMEM((rows, MLA_KV_RANK), F32),
        ],
        compiler_params=_params(("parallel", "arbitrary")),
        name="mla_decode",
    )(qcat, ckv_new, krope_new, cache_ckv, cache_krope, w_ukt)


def _merge_kernel(x_ref, osb_ref, olat_ref, nm_ref, wg_ref, wuv_ref, wsbo_ref, wmlao_ref, wout_ref, h_ref):
    x = x_ref[...]
    xn = _rms(x, nm_ref[...]).astype(BF16)
    g = _dot_nt(xn, wg_ref[...])
    a = _dot(osb_ref[...], wsbo_ref[...])
    pieces = []
    for pair in range(MLA_HEADS // 2):
        pieces.append(_dot(olat_ref[2 * pair], wuv_ref[2 * pair])
                      + _dot(olat_ref[2 * pair + 1], wuv_ref[2 * pair + 1]))
    o_mla = jnp.concatenate(pieces, axis=-1).astype(BF16)
    b = _dot(o_mla, wmlao_ref[...])
    merged = jax.nn.sigmoid(g[:, :D_MODEL]) * a + jax.nn.sigmoid(g[:, D_MODEL:]) * b
    h_ref[...] = x + _dot(merged.astype(BF16), wout_ref[...])


def _merge_call(x, o_sb, o_lat, mw):
    t = x.shape[0]
    tm = PROJ_TILE
    row = lambda i: (i, 0)
    return pl.pallas_call(
        _merge_kernel,
        grid=(t // tm,),
        in_specs=[
            pl.BlockSpec((tm, D_MODEL), row),
            pl.BlockSpec((tm, SB_WIDTH), row),
            pl.BlockSpec((MLA_HEADS, tm, MLA_KV_RANK), lambda i: (0, i, 0)),
            _const_spec((1, D_MODEL)),
            _const_spec((2 * D_MODEL, D_MODEL)),
            _const_spec((MLA_HEADS, MLA_KV_RANK, LANES)),
            _const_spec((SB_WIDTH, D_MODEL)),
            _const_spec((MLA_WIDTH, D_MODEL)),
            _const_spec((D_MODEL, D_MODEL)),
        ],
        out_specs=pl.BlockSpec((tm, D_MODEL), row),
        out_shape=jax.ShapeDtypeStruct((t, D_MODEL), F32),
        compiler_params=_params(("parallel",)),
        name="merge_out",
    )(x, o_sb, o_lat, mw["norm_mix"], mw["w_gates"], mw["w_uv"], mw["w_sb_o"], mw["w_mla_o"], mw["w_out"])


def _dense_ffn_kernel(x_ref, nf_ref, wg_ref, wu_ref, wd_ref, nfin_ref, o_ref, *, n_split, final_norm):
    x = x_ref[...]
    xn = _rms(x, nf_ref[...]).astype(BF16)
    fc = wg_ref.shape[1] // n_split
    acc = x
    for c in range(n_split):
        gt = _dot(xn, wg_ref[:, c * fc:(c + 1) * fc])
        up = _dot(xn, wu_ref[:, c * fc:(c + 1) * fc])
        hid = (gt * jax.nn.sigmoid(gt) * up).astype(BF16)
        acc = acc + _dot(hid, wd_ref[c * fc:(c + 1) * fc, :])
    o_ref[...] = _rms(acc, nfin_ref[...]) if final_norm else acc


def _dense_ffn_call(x, norm_ffn, wg, wu, wd, norm_final, final_norm):
    t = x.shape[0]
    tm = TOKEN_TILE
    d_ff = wg.shape[1]
    n_split = 2 if d_ff % (2 * LANES) == 0 else 1
    row = lambda i: (i, 0)
    return pl.pallas_call(
        functools.partial(_dense_ffn_kernel, n_split=n_split, final_norm=final_norm),
        grid=(t // tm,),
        in_specs=[
            pl.BlockSpec((tm, D_MODEL), row),
            _const_spec((1, D_MODEL)),
            _const_spec((D_MODEL, d_ff)),
            _const_spec((D_MODEL, d_ff)),
            _const_spec((d_ff, D_MODEL)),
            _const_spec((1, D_MODEL)),
        ],
        out_specs=pl.BlockSpec((tm, D_MODEL), row),
        out_shape=jax.ShapeDtypeStruct((t, D_MODEL), F32),
        compiler_params=_params(("parallel",)),
        name="dense_ffn",
    )(x, norm_ffn, wg, wu, wd, norm_final)


def _router_kernel(x_ref, nf_ref, rw_ref, rb_ref, xw_ref, e_ref, g_ref):
    xn = _rms(x_ref[...], nf_ref[...])
    xw_ref[...] = xn

    logits = jnp.dot(xn, rw_ref[...], preferred_element_type=F32, precision=lax.Precision.HIGHEST)
    logits = logits + rb_ref[...]
    lane = lax.broadcasted_iota(jnp.int32, logits.shape, 1).astype(F32)
    lg = jnp.where(lane < N_EXPERTS, logits, -jnp.inf)
    m1 = jnp.max(lg, axis=1, keepdims=True)
    i1 = jnp.min(jnp.where(lg == m1, lane, float(LANES)), axis=1, keepdims=True)
    lg2 = jnp.where(lane == i1, -jnp.inf, lg)
    m2 = jnp.max(lg2, axis=1, keepdims=True)
    i2 = jnp.min(jnp.where(lg2 == m2, lane, float(LANES)), axis=1, keepdims=True)
    e2 = jnp.exp(m2 - m1)
    den = 1.0 + e2
    e_ref[...] = jnp.concatenate([i1, i2], axis=1).astype(jnp.int32)
    g_ref[...] = jnp.concatenate([1.0 / den, e2 / den], axis=1)


def _router_call(x, norm_ffn, rw, rb):
    t = x.shape[0]
    tm = TOKEN_TILE
    row = lambda i: (i, 0)
    return pl.pallas_call(
        _router_kernel,
        grid=(t // tm,),
        in_specs=[
            pl.BlockSpec((tm, D_MODEL), row),
            _const_spec((1, D_MODEL)),
            _const_spec((D_MODEL, LANES)),
            _const_spec((1, LANES)),
        ],
        out_specs=[
            pl.BlockSpec((tm, D_MODEL), row),
            pl.BlockSpec((tm, TOP_K), row),
            pl.BlockSpec((tm, TOP_K), row),
        ],
        out_shape=[
            jax.ShapeDtypeStruct((t, D_MODEL), F32),
            jax.ShapeDtypeStruct((t, TOP_K), jnp.int32),
            jax.ShapeDtypeStruct((t, TOP_K), F32),
        ],
        compiler_params=_params(("parallel",)),
        name="router",
    )(x, norm_ffn, rw, rb)


def _row_copy(src, dst, s, d, sem):
    return pltpu.make_async_copy(src.at[pl.ds(s, 1), :], dst.at[pl.ds(d, 1), :], sem)


def _dispatch_kernel(dest_ref, xw_ref, xs_in_ref, xs_ref, sem):
    del xs_in_ref
    tm = xw_ref.shape[0]

    def start(t, _):
        for k in range(TOP_K):
            _row_copy(xw_ref, xs_ref, t, dest_ref[TOP_K * t + k], sem).start()
        return 0

    def wait(t, _):
        for k in range(TOP_K):
            _row_copy(xw_ref, xs_ref, 0, 0, sem).wait()
        return 0

    lax.fori_loop(0, tm, start, 0, unroll=8)
    lax.fori_loop(0, tm, wait, 0, unroll=8)


def _dispatch_call(dest_flat, xw, xs):
    t = xw.shape[0]
    tm = TOKEN_TILE
    return pl.pallas_call(
        _dispatch_kernel,
        grid=(t // tm,),
        in_specs=[
            pl.BlockSpec((tm * TOP_K,), lambda i: (i,), memory_space=pltpu.SMEM),
            pl.BlockSpec((tm, D_MODEL), lambda i: (i, 0)),
            pl.BlockSpec(memory_space=pl.ANY),
        ],
        out_specs=pl.BlockSpec(memory_space=pl.ANY),
        out_shape=jax.ShapeDtypeStruct(xs.shape, xs.dtype),
        scratch_shapes=[pltpu.SemaphoreType.DMA],
        input_output_aliases={2: 0},
        compiler_params=_params(("arbitrary",)),
        name="moe_dispatch",
    )(dest_flat, xw, xs)


def _expert_kernel(tile_e_ref, n_used_ref, xs_ref, wg_ref, wu_ref, wd_ref, y_ref, x_ref):
    r = pl.program_id(0)
    f = pl.program_id(1)

    @pl.when(f == 0)
    def _():
        x_ref[...] = xs_ref[...].astype(BF16)
        y_ref[...] = jnp.zeros_like(y_ref)

    @pl.when(r < n_used_ref[0])
    def _():
        x = x_ref[...]
        gt = _dot(x, wg_ref[...])
        up = _dot(x, wu_ref[...])
        hid = (gt * jax.nn.sigmoid(gt) * up).astype(BF16)
        y_ref[...] += _dot(hid, wd_ref[...])


def _expert_call(tile_e, n_used, xs, wg, wu, wd):
    n_rows = xs.shape[0]
    rt = EXPERT_ROW_TILE
    nf = EXPERT_F_SPLIT
    d_e = wg.shape[2]
    fc = d_e // nf

    def fidx(r, f):
        return jnp.where(r % 2 == 0, f, nf - 1 - f)

    grid_spec = pltpu.PrefetchScalarGridSpec(
        num_scalar_prefetch=2,
        grid=(n_rows // rt, nf),
        in_specs=[
            pl.BlockSpec((rt, D_MODEL), lambda r, f, te, nu: (r, 0)),
            pl.BlockSpec((None, D_MODEL, fc), lambda r, f, te, nu: (te[r], 0, fidx(r, f))),
            pl.BlockSpec((None, D_MODEL, fc), lambda r, f, te, nu: (te[r], 0, fidx(r, f))),
            pl.BlockSpec((None, fc, D_MODEL), lambda r, f, te, nu: (te[r], fidx(r, f), 0)),
        ],
        out_specs=pl.BlockSpec((rt, D_MODEL), lambda r, f, te, nu: (r, 0)),
        scratch_shapes=[pltpu.VMEM((rt, D_MODEL), BF16)],
    )
    return pl.pallas_call(
        _expert_kernel,
        grid_spec=grid_spec,
        out_shape=jax.ShapeDtypeStruct((n_rows, D_MODEL), F32),
        compiler_params=_params(("arbitrary", "arbitrary")),
        name="moe_experts",
    )(tile_e, n_used, xs, wg, wu, wd)


def _combine_kernel(dest_ref, h_ref, g_ref, y_ref, nfin_ref, o_ref, buf0, buf1, sem, *, final_norm):
    tm = h_ref.shape[0]

    def start(t, _):
        _row_copy(y_ref, buf0, dest_ref[TOP_K * t], t, sem).start()
        _row_copy(y_ref, buf1, dest_ref[TOP_K * t + 1], t, sem).start()
        return 0

    def wait(t, _):
        _row_copy(y_ref, buf0, 0, 0, sem).wait()
        _row_copy(y_ref, buf1, 0, 0, sem).wait()
        return 0

    lax.fori_loop(0, tm, start, 0, unroll=8)
    lax.fori_loop(0, tm, wait, 0, unroll=8)
    g = g_ref[...]
    y = h_ref[...] + (buf0[...] * g[:, 0:1] + buf1[...] * g[:, 1:2])
    o_ref[...] = _rms(y, nfin_ref[...]) if final_norm else y


def _combine_call(dest_flat, h, gate, yb, norm_final, final_norm):
    t = h.shape[0]
    tm = TOKEN_TILE
    row = lambda i: (i, 0)
    return pl.pallas_call(
        functools.partial(_combine_kernel, final_norm=final_norm),
        grid=(t // tm,),
        in_specs=[
            pl.BlockSpec((tm * TOP_K,), lambda i: (i,), memory_space=pltpu.SMEM),
            pl.BlockSpec((tm, D_MODEL), row),
            pl.BlockSpec((tm, TOP_K), row),
            pl.BlockSpec(memory_space=pl.ANY),
            _const_spec((1, D_MODEL)),
        ],
        out_specs=pl.BlockSpec((tm, D_MODEL), row),
        out_shape=jax.ShapeDtypeStruct((t, D_MODEL), F32),
        scratch_shapes=[pltpu.VMEM((tm, D_MODEL), F32), pltpu.VMEM((tm, D_MODEL), F32),
                        pltpu.SemaphoreType.DMA],
        compiler_params=_params(("arbitrary",)),
        name="moe_combine",
    )(dest_flat, h, gate, yb, norm_final)


def _moe_layout(top_e):
    rt = EXPERT_ROW_TILE
    e_flat = top_e.reshape(-1)
    n_assign = e_flat.shape[0]
    onehot = (e_flat[:, None] == jnp.arange(N_EXPERTS, dtype=jnp.int32)[None, :]).astype(jnp.int32)
    incl = jnp.cumsum(onehot, axis=0)
    counts = incl[-1]
    rank = jnp.sum((incl - onehot) * onehot, axis=1)
    padded = (counts + rt - 1) // rt * rt
    pad_end = jnp.cumsum(padded)
    pad_start = pad_end - padded
    dest = (jnp.sum(onehot * pad_start[None, :], axis=1) + rank).astype(jnp.int32)
    n_rows = -(-(n_assign + N_EXPERTS * (rt - 1)) // rt) * rt
    n_tiles = n_rows // rt
    tile_start = jnp.arange(n_tiles, dtype=jnp.int32) * rt
    tile_e = jnp.minimum(
        jnp.sum((pad_end[None, :] <= tile_start[:, None]).astype(jnp.int32), axis=1),
        N_EXPERTS - 1).astype(jnp.int32)
    n_used = (pad_end[-1] // rt).astype(jnp.int32).reshape(1)
    return dest, tile_e, n_used, n_rows


def _moe(streams, norm_ffn, rw, rb, wg, wu, wd, norm_final, final_norm):
    routed = [_router_call(h, norm_ffn, rw, rb) for h in streams]
    top_e = jnp.concatenate([r[1] for r in routed], axis=0)
    dest, tile_e, n_used, n_rows = _moe_layout(top_e)
    xs = jnp.zeros((n_rows, D_MODEL), F32)
    dests = []
    off = 0
    for h, (xw, _, _) in zip(streams, routed):
        n = h.shape[0] * TOP_K
        dests.append(dest[off:off + n])
        off += n
        xs = _dispatch_call(dests[-1], xw, xs)
    yb = _expert_call(tile_e, n_used, xs, wg, wu, wd)
    return [_combine_call(d, h, r[2], yb, norm_final, final_norm)
            for d, h, r in zip(dests, streams, routed)]


def _mixer_weights(l, norm_mix, w_in, mla_q_norm, w_uq, mla_kv_norm, w_uk, w_uv, w_sb_o, w_mla_o, w_out):
    w = w_in[l]
    o_gate = 3 * SB_WIDTH + MLA_Q_RANK + MLA_KV_RANK + MLA_ROPE_DIM
    w = w.T
    w1 = jnp.concatenate(
        [w[:o_gate], jnp.zeros((LANES - MLA_ROPE_DIM, D_MODEL), F32)], axis=0).astype(BF16)
    uq = w_uq[l].reshape(MLA_Q_RANK, MLA_HEADS, MLA_NOPE_DIM + MLA_ROPE_DIM)
    uq = jnp.concatenate([
        uq[:, :, :MLA_NOPE_DIM].reshape(MLA_Q_RANK, -1),
        uq[:, :, MLA_NOPE_DIM:MLA_NOPE_DIM + ROPE_HALF].reshape(MLA_Q_RANK, -1),
        uq[:, :, MLA_NOPE_DIM + ROPE_HALF:].reshape(MLA_Q_RANK, -1)], axis=1).astype(BF16)
    uv = jnp.transpose(w_uv[l], (1, 0, 2))
    zeros = jnp.zeros_like(uv)
    even = (jnp.arange(MLA_HEADS) % 2 == 0)[:, None, None]
    uv = jnp.concatenate([jnp.where(even, uv, zeros), jnp.where(even, zeros, uv)], axis=2).astype(BF16)
    return {
        "norm_mix": norm_mix[l].reshape(1, D_MODEL),
        "w1": w1,
        "w_gates": w[o_gate:].astype(BF16),
        "q_norm": mla_q_norm[l].reshape(1, MLA_Q_RANK),
        "w_uq": uq,
        "kv_norm": mla_kv_norm[l].reshape(1, MLA_KV_RANK),
        "w_ukt": jnp.transpose(w_uk[l], (1, 2, 0)).astype(BF16),
        "w_uk": jnp.transpose(w_uk[l], (1, 0, 2)).astype(BF16),
        "w_uv": uv,
        "w_sb_o": w_sb_o[l].astype(BF16),
        "w_mla_o": w_mla_o[l].astype(BF16),
        "w_out": w_out[l].astype(BF16),
    }


def _rope_tables(pos, rows):
    inv_freq = jnp.power(ROPE_BASE, -jnp.arange(ROPE_HALF, dtype=F32) / ROPE_HALF)
    ang = pos.astype(F32)[:, None] * inv_freq[None, :]
    reps = (rows // pos.shape[0], LANES // ROPE_HALF)
    return jnp.tile(jnp.cos(ang), reps), jnp.tile(jnp.sin(ang), reps)


def kernel(x_prompt, x_sample, cache_sb_k, cache_sb_v, cache_mla_ckv, cache_mla_krope, norm_mix, w_in,
           mla_q_norm, w_uq, mla_kv_norm, w_uk, w_uv, w_sb_o, w_mla_o, w_out, norm_ffn, ffn_w_gate,
           ffn_w_up, ffn_w_down, router_w, router_b, moe_w_gate, moe_w_up, moe_w_down, norm_final):
    n_p, seq, _ = x_prompt.shape
    n_s, n_q, _ = x_sample.shape
    depth = w_in.shape[0]
    past = cache_sb_k.shape[2]
    assert seq % PROJ_TILE == 0 and (n_s * n_q) % PROJ_TILE == 0 and PROJ_TILE % n_q == 0
    assert PROJ_TILE % TOKEN_TILE == 0
    assert seq % KEY_TILE == 0 and past % DECODE_KEY_BLOCK == 0 and n_q <= LANES

    cos_p, sin_p = _rope_tables(jnp.arange(seq, dtype=jnp.int32), seq)
    cos_s, sin_s = _rope_tables(past + jnp.arange(n_q, dtype=jnp.int32), PROJ_TILE)
    nfin = norm_final.reshape(1, D_MODEL)
    cache_kt = jnp.transpose(cache_sb_k, (0, 1, 3, 4, 2))
    cache_vt = jnp.transpose(cache_sb_v, (0, 1, 3, 4, 2))
    cache_rt = jnp.transpose(cache_mla_krope, (0, 1, 3, 2))

    hp = x_prompt.reshape(n_p * seq, D_MODEL)
    hs = x_sample.reshape(n_s * n_q, D_MODEL)
    rows_p, rows_s = [], []
    for l in range(depth):
        mw = _mixer_weights(l, norm_mix, w_in, mla_q_norm, w_uq, mla_kv_norm, w_uk, w_uv,
                            w_sb_o, w_mla_o, w_out)
        sbq, sbk, sbv, ckv, krope, qcat = _proj_call(hp, mw, cos_p, sin_p)
        o_sb, sbk_t, sbv_t = _sb_prompt_call(sbq, sbk, sbv, n_p, seq)
        o_lat = _mla_prompt_call(qcat, ckv, krope, mw["w_uk"], n_p, seq)
        hp = _merge_call(hp, o_sb, o_lat, mw)
        rows_p.append((sbk_t, sbv_t, ckv, krope))

        sbq, sbk, sbv, ckv, krope, qcat = _proj_call(hs, mw, cos_s, sin_s)
        o_sb = _sb_decode_call(sbq, sbk, sbv, cache_kt, cache_vt, l, n_q)
        o_lat = _mla_decode_call(qcat, ckv, krope, cache_mla_ckv, cache_rt, mw["w_ukt"], l, n_q)
        hs = _merge_call(hs, o_sb, o_lat, mw)
        rows_s.append((sbk, sbv, ckv, krope))

        last = l == depth - 1
        nf = norm_ffn[l].reshape(1, D_MODEL)
        i = l // 2
        if l % 2 == 0:
            wg, wu, wd = ffn_w_gate[i].astype(BF16), ffn_w_up[i].astype(BF16), ffn_w_down[i].astype(BF16)
            hp = _dense_ffn_call(hp, nf, wg, wu, wd, nfin, last)
            hs = _dense_ffn_call(hs, nf, wg, wu, wd, nfin, last)
        else:
            rw = jnp.concatenate([router_w[i], jnp.zeros((D_MODEL, LANES - N_EXPERTS), F32)], axis=1)
            rb = jnp.concatenate([router_b[i], jnp.zeros((LANES - N_EXPERTS,), F32)]).reshape(1, LANES)
            hp, hs = _moe([hp, hs], nf, rw, rb, moe_w_gate[i].astype(BF16), moe_w_up[i].astype(BF16),
                          moe_w_down[i].astype(BF16), nfin, last)

    def stack(rows, idx, shape):
        return jnp.stack([r[idx] for r in rows]).reshape((depth,) + shape)

    def stack_t(rows, idx):
        a = stack(rows, idx, (n_p, SB_HEADS, SB_HEAD_DIM, seq))
        return jnp.transpose(a, (0, 1, 4, 2, 3))

    kv_s = (n_s, n_q, SB_HEADS, SB_HEAD_DIM)
    return (hp.reshape(n_p, seq, D_MODEL), hs.reshape(n_s, n_q, D_MODEL),
            stack_t(rows_p, 0), stack_t(rows_p, 1),
            stack(rows_p, 2, (n_p, seq, MLA_KV_RANK)), stack(rows_p, 3, (n_p, seq, MLA_ROPE_DIM)),
            stack(rows_s, 0, kv_s), stack(rows_s, 1, kv_s),
            stack(rows_s, 2, (n_s, n_q, MLA_KV_RANK)), stack(rows_s, 3, (n_s, n_q, MLA_ROPE_DIM)))
```
